```python
import math
import jax, jax.numpy as jnp
from jax import lax
import numpy as np

D_MODEL = 2048
BATCH = 4
SEQ = 2048
DEPTH = 1
DEC_BATCH = 128
DEC_SEQ = 8
PAST_LEN = 16384
PAGE_SIZE = 128

D_INNER = D_MODEL
SSD_HEADDIM = 64
SSD_HEADS = D_INNER // SSD_HEADDIM
SSD_GROUPS = 8
SSD_HG = SSD_HEADS // SSD_GROUPS
SSD_STATE = 128
SSD_CONV = 4
SSD_CHUNK = 128
CONV_DIM = D_INNER + 2 * SSD_GROUPS * SSD_STATE
D_CONV = D_MODEL // 2
CONF_KERNEL = 31
MEM_LEN = 256
MEM_HEADS = 4
MEM_HEAD_DIM = D_MODEL // 8
ATT_W = MEM_HEADS * MEM_HEAD_DIM
D_FF = 4 * D_MODEL
ALPHA = (2.0 * DEPTH) ** 0.25
BETA = (8.0 * DEPTH) ** -0.25
EPS = 1e-5
N_BRANCH = 3
O_Z = 0
O_XBC = O_Z + D_INNER
O_DT = O_XBC + CONV_DIM
O_GLU = O_DT + SSD_HEADS
O_Q = O_GLU + 2 * D_CONV
O_GATE = O_Q + ATT_W
N_IN = O_GATE + N_BRANCH * D_MODEL

kernel_name = "hybrid_ssd_conformer_memxattn_step"


def layernorm(x, g, b):
    xf = x.astype(jnp.float32)
    mu = jnp.mean(xf, -1, keepdims=True)
    var = jnp.mean(jnp.square(xf - mu), -1, keepdims=True)
    return ((xf - mu) * lax.rsqrt(var + EPS) * g.astype(jnp.float32) + b.astype(jnp.float32)).astype(x.dtype)


def gated_rmsnorm(y, z, g):
    v = y.astype(jnp.float32) * jax.nn.silu(z.astype(jnp.float32))
    shp = v.shape
    v = v.reshape(shp[:-1] + (SSD_GROUPS, D_INNER // SSD_GROUPS))
    v = v * lax.rsqrt(jnp.mean(jnp.square(v), -1, keepdims=True) + EPS)
    return (v.reshape(shp) * g.astype(jnp.float32)).astype(z.dtype)


def causal_dwconv(u, buf, w, b):
    full = jnp.concatenate([buf.astype(u.dtype), u], axis=1)
    c = u.shape[-1]
    out = lax.conv_general_dilated(full, w.astype(u.dtype)[:, None, :], window_strides=(1,), padding='VALID',
                                   dimension_numbers=('NWC', 'WIO', 'NWC'), feature_group_count=c)
    return out + b.astype(u.dtype), full[:, -(w.shape[0] - 1):]


def ssd_scan(x, dt, a_head, bm, cm, h0):
    bsz, L = x.shape[:2]
    q = min(SSD_CHUNK, L)
    nc = -(-L // q)
    pad = nc * q - L
    padf = lambda t: jnp.pad(t.astype(jnp.float32), [(0, 0), (0, pad)] + [(0, 0)] * (t.ndim - 2))
    x = padf(x).reshape(bsz, nc, q, SSD_GROUPS, SSD_HG, SSD_HEADDIM)
    dt = padf(dt).reshape(bsz, nc, q, SSD_GROUPS, SSD_HG)
    bm = padf(bm).reshape(bsz, nc, q, SSD_GROUPS, SSD_STATE)
    cm = padf(cm).reshape(bsz, nc, q, SSD_GROUPS, SSD_STATE)
    a = dt * a_head.astype(jnp.float32).reshape(SSD_GROUPS, SSD_HG)
    acs = jnp.cumsum(a, axis=2)
    acs_t = jnp.moveaxis(acs, 2, -1)
    diff = acs_t[..., :, None] - acs_t[..., None, :]
    causal = jnp.tril(jnp.ones((q, q), bool))
    decay = jnp.exp(jnp.where(causal, diff, -jnp.inf))
    xdt = x * dt[..., None]
    cb = jnp.einsum('bcqgn,bckgn->bcgqk', cm, bm)
    y_diag = jnp.einsum('bcgqk,bcghqk,bckghp->bcqghp', cb, decay, xdt)
    decay_s = jnp.exp(acs[:, :, -1:] - acs)
    states = jnp.einsum('bckgn,bckgh,bckghp->bcghpn', bm, decay_s, xdt)
    chunk_decay = jnp.exp(acs[:, :, -1])
    h_init = h0.astype(jnp.float32).reshape(bsz, SSD_GROUPS, SSD_HG, SSD_HEADDIM, SSD_STATE)

    def step(h, inp):
        dec, s = inp
        return dec[..., None, None] * h + s, h

    h_fin, h_starts = lax.scan(step, h_init, (jnp.moveaxis(chunk_decay, 1, 0), jnp.moveaxis(states, 1, 0)))
    h_starts = jnp.moveaxis(h_starts, 0, 1)
    y_off = jnp.einsum('bcqgn,bcghpn,bcqgh->bcqghp', cm, h_starts, jnp.exp(acs))
    y = (y_diag + y_off).reshape(bsz, nc * q, SSD_HEADS, SSD_HEADDIM)[:, :L]
    return y, h_fin.reshape(bsz, SSD_HEADS, SSD_HEADDIM, SSD_STATE)


def mixer(h, ssd_conv0, ssm0, cc0, mem_k, mem_v, p, l):
    bsz, L, _ = h.shape
    proj = h @ p['w_in'][l]
    z = proj[..., O_Z:O_XBC]
    xbc = proj[..., O_XBC:O_DT]
    dt_raw = proj[..., O_DT:O_GLU]
    glu = proj[..., O_GLU:O_Q]
    qm = proj[..., O_Q:O_GATE]
    gate_pre = proj[..., O_GATE:]
    xbc, ssd_conv1 = causal_dwconv(xbc, ssd_conv0, p['ssd_conv_w'][l], p['ssd_conv_b'][l])
    xbc = jax.nn.silu(xbc)
    xs = xbc[..., :D_INNER].reshape(bsz, L, SSD_HEADS, SSD_HEADDIM)
    bm = xbc[..., D_INNER:D_INNER + SSD_GROUPS * SSD_STATE].reshape(bsz, L, SSD_GROUPS, SSD_STATE)
    cm = xbc[..., D_INNER + SSD_GROUPS * SSD_STATE:].reshape(bsz, L, SSD_GROUPS, SSD_STATE)
    dt = jax.nn.softplus(dt_raw.astype(jnp.float32) + p['ssd_dt_bias'][l].astype(jnp.float32))
    a_head = -jnp.exp(p['ssd_a_log'][l].astype(jnp.float32))
    y, ssm1 = ssd_scan(xs, dt, a_head, bm, cm, ssm0)
    y = y + p['ssd_d'][l].astype(jnp.float32)[:, None] * xs.astype(jnp.float32)
    y = gated_rmsnorm(y.reshape(bsz, L, D_INNER), z, p['ssd_norm_g'][l])
    br_ssd = y @ p['w_br_ssd'][l]
    u = glu[..., :D_CONV] * jax.nn.sigmoid(glu[..., D_CONV:])
    c, cc1 = causal_dwconv(u, cc0, p['conf_dw_w'][l], p['conf_dw_b'][l])
    c = jax.nn.silu(layernorm(c, p['conf_ln_g'][l], p['conf_ln_b'][l]))
    br_conf = c @ p['w_br_conf'][l]
    qh = qm.reshape(bsz, L, MEM_HEADS, MEM_HEAD_DIM)
    s = jnp.einsum('blhd,bmhd->bhlm', qh, mem_k).astype(jnp.float32) * (MEM_HEAD_DIM ** -0.5)
    pr = jax.nn.softmax(s, axis=-1).astype(mem_v.dtype)
    o = jnp.einsum('bhlm,bmhd->blhd', pr, mem_v).reshape(bsz, L, ATT_W)
    br_mem = o @ p['w_br_mem'][l]
    g = jax.nn.sigmoid(gate_pre.reshape(bsz, L, N_BRANCH, D_MODEL) + p['b_gate'][l])
    merged = g[:, :, 0] * br_ssd + g[:, :, 1] * br_conf + g[:, :, 2] * br_mem
    return merged @ p['w_out'][l], ssd_conv1, ssm1, cc1


def mlp(h, p, l):
    a = jnp.square(jax.nn.relu(h @ p['w_ff1'][l] + p['b_ff1'][l]))
    return a @ p['w_ff2'][l] + p['b_ff2'][l]


def trunk(x, ssd_conv0, ssm0, cc0, mem_k, mem_v, p):
    h = layernorm(x, p['ln_in_g'], p['ln_in_b'])
    n_ssd_conv, n_ssm, n_cc = [], [], []
    for l in range(DEPTH):
        mix, sc1, ss1, cc1 = mixer(h, ssd_conv0[l], ssm0[l], cc0[l], mem_k[l], mem_v[l], p, l)
        h = layernorm(ALPHA * h + mix, p['ln1_g'][l], p['ln1_b'][l])
        h = layernorm(ALPHA * h + mlp(h, p, l), p['ln2_g'][l], p['ln2_b'][l])
        n_ssd_conv.append(sc1)
        n_ssm.append(ss1)
        n_cc.append(cc1)
    return h, jnp.stack(n_ssm), jnp.stack(n_ssd_conv), jnp.stack(n_cc)


def setup_inputs(seed: int = 0) -> dict:
    key = jax.random.key(seed)
    ks = iter(jax.random.split(key, 40))
    nrm = lambda shape, scale: jax.random.normal(next(ks), shape, jnp.float32) * scale
    gain = lambda shape: 1.0 + nrm(shape, 0.02)
    dt0 = jnp.exp(jax.random.uniform(next(ks), (DEPTH, SSD_HEADS), jnp.float32, math.log(1e-3), math.log(1e-1)))
    return {
        'x_prompt': nrm((BATCH, SEQ, D_MODEL), 1.0),
        'x_sample': nrm((DEC_BATCH, DEC_SEQ, D_MODEL), 1.0),
        'mem_prompt': nrm((BATCH, MEM_LEN, D_MODEL), 1.0),
        'state_ssm': nrm((DEPTH, DEC_BATCH, SSD_HEADS, SSD_HEADDIM, SSD_STATE), 0.1),
        'state_ssd_conv': nrm((DEPTH, DEC_BATCH, SSD_CONV - 1, CONV_DIM), 1.0),
        'state_conf_conv': nrm((DEPTH, DEC_BATCH, CONF_KERNEL - 1, D_CONV), 0.5),
        'cache_mem_k': nrm((DEPTH, DEC_BATCH, MEM_LEN, MEM_HEADS, MEM_HEAD_DIM), 1.0),
        'cache_mem_v': nrm((DEPTH, DEC_BATCH, MEM_LEN, MEM_HEADS, MEM_HEAD_DIM), 1.0),
        'ln_in_g': gain((D_MODEL,)),
        'ln_in_b': nrm((D_MODEL,), 0.02),
        'w_in': nrm((DEPTH, D_MODEL, N_IN), D_MODEL ** -0.5),
        'b_gate': nrm((DEPTH, N_BRANCH, D_MODEL), 0.02),
        'ssd_conv_w': nrm((DEPTH, SSD_CONV, CONV_DIM), SSD_CONV ** -0.5),
        'ssd_conv_b': nrm((DEPTH, CONV_DIM), 0.02),
        'ssd_dt_bias': dt0 + jnp.log(-jnp.expm1(-dt0)),
        'ssd_a_log': jnp.log(jax.random.uniform(next(ks), (DEPTH, SSD_HEADS), jnp.float32, 1.0, 16.0)),
        'ssd_d': 1.0 + nrm((DEPTH, SSD_HEADS), 0.1),
        'ssd_norm_g': gain((DEPTH, D_INNER)),
        'conf_dw_w': nrm((DEPTH, CONF_KERNEL, D_CONV), CONF_KERNEL ** -0.5),
        'conf_dw_b': nrm((DEPTH, D_CONV), 0.02),
        'conf_ln_g': gain((DEPTH, D_CONV)),
        'conf_ln_b': nrm((DEPTH, D_CONV), 0.02),
        'w_mem_k': nrm((DEPTH, D_MODEL, ATT_W), D_MODEL ** -0.5),
        'w_mem_v': nrm((DEPTH, D_MODEL, ATT_W), BETA * D_MODEL ** -0.5),
        'w_br_ssd': nrm((DEPTH, D_INNER, D_MODEL), BETA * D_INNER ** -0.5),
        'w_br_conf': nrm((DEPTH, D_CONV, D_MODEL), BETA * D_CONV ** -0.5),
        'w_br_mem': nrm((DEPTH, ATT_W, D_MODEL), BETA * ATT_W ** -0.5),
        'w_out': nrm((DEPTH, D_MODEL, D_MODEL), BETA * D_MODEL ** -0.5),
        'ln1_g': gain((DEPTH, D_MODEL)),
        'ln1_b': nrm((DEPTH, D_MODEL), 0.02),
        'w_ff1': nrm((DEPTH, D_MODEL, D_FF), BETA * D_MODEL ** -0.5),
        'b_ff1': nrm((DEPTH, D_FF), 0.02),
        'w_ff2': nrm((DEPTH, D_FF, D_MODEL), BETA * D_FF ** -0.5),
        'b_ff2': nrm((DEPTH, D_MODEL), 0.02),
        'ln2_g': gain((DEPTH, D_MODEL)),
        'ln2_b': nrm((DEPTH, D_MODEL), 0.02),
    }


def reference(x_prompt, x_sample, mem_prompt, state_ssm, state_ssd_conv, state_conf_conv, cache_mem_k, cache_mem_v,
              ln_in_g, ln_in_b, w_in, b_gate, ssd_conv_w, ssd_conv_b, ssd_dt_bias, ssd_a_log, ssd_d, ssd_norm_g,
              conf_dw_w, conf_dw_b, conf_ln_g, conf_ln_b, w_mem_k, w_mem_v, w_br_ssd, w_br_conf, w_br_mem, w_out,
              ln1_g, ln1_b, w_ff1, b_ff1, w_ff2, b_ff2, ln2_g, ln2_b):
    p = {'ln_in_g': ln_in_g, 'ln_in_b': ln_in_b, 'w_in': w_in, 'b_gate': b_gate,
         'ssd_conv_w': ssd_conv_w, 'ssd_conv_b': ssd_conv_b, 'ssd_dt_bias': ssd_dt_bias, 'ssd_a_log': ssd_a_log,
         'ssd_d': ssd_d, 'ssd_norm_g': ssd_norm_g, 'conf_dw_w': conf_dw_w, 'conf_dw_b': conf_dw_b,
         'conf_ln_g': conf_ln_g, 'conf_ln_b': conf_ln_b, 'w_br_ssd': w_br_ssd, 'w_br_conf': w_br_conf,
         'w_br_mem': w_br_mem, 'w_out': w_out, 'ln1_g': ln1_g, 'ln1_b': ln1_b, 'w_ff1': w_ff1, 'b_ff1': b_ff1,
         'w_ff2': w_ff2, 'b_ff2': b_ff2, 'ln2_g': ln2_g, 'ln2_b': ln2_b}
    dt_ = x_prompt.dtype
    p_mem_k = jnp.einsum('bmd,ldk->lbmk', mem_prompt, w_mem_k).reshape(DEPTH, BATCH, MEM_LEN, MEM_HEADS, MEM_HEAD_DIM)
    p_mem_v = jnp.einsum('bmd,ldk->lbmk', mem_prompt, w_mem_v).reshape(DEPTH, BATCH, MEM_LEN, MEM_HEADS, MEM_HEAD_DIM)
    z_ssd_conv = jnp.zeros((DEPTH, BATCH, SSD_CONV - 1, CONV_DIM), dt_)
    z_ssm = jnp.zeros((DEPTH, BATCH, SSD_HEADS, SSD_HEADDIM, SSD_STATE), jnp.float32)
    z_cc = jnp.zeros((DEPTH, BATCH, CONF_KERNEL - 1, D_CONV), dt_)
    y_prompt, p_ssm, p_ssd_conv, p_cc = trunk(x_prompt, z_ssd_conv, z_ssm, z_cc, p_mem_k, p_mem_v, p)
    y_sample, s_ssm, s_ssd_conv, s_cc = trunk(x_sample, state_ssd_conv, state_ssm, state_conf_conv,
                                              cache_mem_k, cache_mem_v, p)
    return (y_prompt, y_sample, p_ssm, p_ssd_conv, p_cc, p_mem_k, p_mem_v, s_ssm, s_ssd_conv, s_cc)
```

```python
import functools
import math

import jax
import jax.numpy as jnp
from jax import lax
from jax.experimental import pallas as pl
from jax.experimental.pallas import tpu as pltpu

F32 = jnp.float32
BF16 = jnp.bfloat16

D_MODEL = 2048
SSD_HEADDIM = 64
SSD_HEADS = 32
SSD_GROUPS = 8
SSD_HG = 4
SSD_STATE = 128
SSD_CONV = 4
CONV_DIM = 4096
D_CONV = 1024
CONF_KERNEL = 31
MEM_LEN = 256
MEM_HEADS = 4
MEM_HEAD_DIM = 256
ATT_W = 1024
D_FF = 8192
DEPTH = 1
ALPHA = (2.0 * DEPTH) ** 0.25
EPS = 1e-5

O_XBC = 2048
O_DT = 6144
O_GLU = 6176
P_GATE = 6144
P_GLU = 12288
P_Q = 14336
P_W = 15360

GROUP_W = SSD_HG * SSD_HEADDIM
SSD_ROWS = 128
LANES = 128
VMEM_LIMIT = 56 * 1024 * 1024


def _cparams(n_axes):
    return pltpu.CompilerParams(dimension_semantics=("arbitrary",) * n_axes, vmem_limit_bytes=VMEM_LIMIT)


def _layernorm(x, g, b):
    mu = jnp.mean(x, axis=-1, keepdims=True)
    xc = x - mu
    var = jnp.mean(xc * xc, axis=-1, keepdims=True)
    return xc * lax.rsqrt(var + EPS) * g + b


def _sigmoid(x):
    return 1.0 / (1.0 + jnp.exp(-x))


def _silu(x):
    return x * _sigmoid(x)


def _dot(a, b):
    return jnp.dot(a.astype(BF16), b.astype(BF16), preferred_element_type=F32)


def _dot_nt(a, b):
    return lax.dot_general(a.astype(BF16), b.astype(BF16), (((1,), (1,)), ((), ())), preferred_element_type=F32)


def _dot_exact(a, b):
    return jnp.dot(a, b, precision=lax.Precision.HIGHEST, preferred_element_type=F32)


def _iota_div(shape, axis, divisor):
    shift = divisor.bit_length() - 1
    assert 1 << shift == divisor
    return lax.shift_right_logical(lax.broadcasted_iota(jnp.int32, shape, axis), shift)


PROJ_TM = 1024
PROJ_TN = 512
PROJ_LN_ROWS = 128
PROJ_NA = O_DT // PROJ_TN
PROJ_NB = (P_W - O_DT) // PROJ_TN


def _proj_kernel(x_ref, g_ref, b_ref, wa_ref, wb_ref, wdt_ref, out_ref, dt_ref, xn_ref):
    j = pl.program_id(1)

    @pl.when(j == 0)
    def _():
        def ln_rows(r, carry):
            r0 = pl.multiple_of(r * PROJ_LN_ROWS, PROJ_LN_ROWS)
            xn = _layernorm(x_ref[pl.ds(r0, PROJ_LN_ROWS), :], g_ref[...], b_ref[...])
            xn_ref[pl.ds(r0, PROJ_LN_ROWS), :] = xn.astype(BF16)
            return carry
        lax.fori_loop(0, x_ref.shape[0] // PROJ_LN_ROWS, ln_rows, 0)
        dt_ref[...] = _dot(xn_ref[...], wdt_ref[...])

    @pl.when(j < PROJ_NA)
    def _():
        out_ref[...] = _dot(xn_ref[...], wa_ref[...])

    @pl.when(j >= PROJ_NA)
    def _():
        out_ref[...] = _dot(xn_ref[...], wb_ref[...])


def _proj_out_tile(j):
    jb = j - PROJ_NA
    n_glu_q = (P_W - P_GLU) // PROJ_TN
    tile_b = jnp.where(jb < n_glu_q, P_GLU // PROJ_TN + jb, P_GATE // PROJ_TN + jb - n_glu_q)
    return jnp.where(j < PROJ_NA, j, tile_b)


def _proj(x2d, ln_g, ln_b, w_in, w_in_b):
    m = x2d.shape[0]
    tm = min(PROJ_TM, m)
    grid = (m // tm, PROJ_NA + PROJ_NB)
    return pl.pallas_call(
        _proj_kernel,
        grid=grid,
        in_specs=[
            pl.BlockSpec((tm, D_MODEL), lambda i, j: (i, 0), pipeline_mode=pl.Buffered(1)),
            pl.BlockSpec((1, D_MODEL), lambda i, j: (0, 0)),
            pl.BlockSpec((1, D_MODEL), lambda i, j: (0, 0)),
            pl.BlockSpec((D_MODEL, PROJ_TN), lambda i, j: (0, jnp.minimum(j, PROJ_NA - 1))),
            pl.BlockSpec((D_MODEL, PROJ_TN), lambda i, j: (0, jnp.maximum(j - PROJ_NA, 0))),
            pl.BlockSpec((D_MODEL, LANES), lambda i, j: (0, O_DT // LANES)),
        ],
        out_specs=[
            pl.BlockSpec((tm, PROJ_TN), lambda i, j: (i, _proj_out_tile(j))),
            pl.BlockSpec((tm, LANES), lambda i, j: (i, 0)),
        ],
        out_shape=[jax.ShapeDtypeStruct((m, P_W), F32), jax.ShapeDtypeStruct((m, LANES), F32)],
        scratch_shapes=[pltpu.VMEM((tm, D_MODEL), BF16)],
        compiler_params=_cparams(2),
        name="proj",
    )(x2d, ln_g, ln_b, w_in, w_in_b, w_in)


def _matmul_kernel(x_ref, w_ref, out_ref):
    out_ref[...] = _dot(x_ref[...], w_ref[...])


def _matmul(x2d, w, tn=512):
    m, k = x2d.shape
    n = w.shape[1]
    return pl.pallas_call(
        _matmul_kernel,
        grid=(n // tn,),
        in_specs=[pl.BlockSpec((m, k), lambda j: (0, 0)), pl.BlockSpec((k, tn), lambda j: (0, j))],
        out_specs=pl.BlockSpec((m, tn), lambda j: (0, j)),
        out_shape=jax.ShapeDtypeStruct((m, n), F32),
        compiler_params=_cparams(1),
        name="memkv",
    )(x2d, w)


def _expand_heads(v, lane0, width):
    col = _iota_div((1, width), 1, SSD_HEADDIM)
    out = jnp.broadcast_to(v[:, lane0 + SSD_HG - 1:lane0 + SSD_HG], (v.shape[0], width))
    for h in range(SSD_HG - 2, -1, -1):
        out = jnp.where(col == h, jnp.broadcast_to(v[:, lane0 + h:lane0 + h + 1], (v.shape[0], width)), out)
    return out


def _conv4(pad_ref, u, hist_ref, hist_out_ref, w_ref, b_ref, first, l_q):
    @pl.when(first)
    def _():
        pad_ref[:, 5:8, :] = hist_ref[...]

    @pl.when(jnp.logical_not(first))
    def _():
        pad_ref[:, 5:8, :] = pad_ref[:, 5 + l_q:8 + l_q, :]

    pad_ref[:, 8:8 + l_q, :] = u
    acc = b_ref[...] + w_ref[0:1, :] * pad_ref[:, 5:5 + l_q, :]
    for j in range(1, SSD_CONV):
        acc = acc + w_ref[j:j + 1, :] * pad_ref[:, 5 + j:5 + j + l_q, :]
    hist_out_ref[...] = pad_ref[:, 5 + l_q:8 + l_q, :]
    return _silu(acc)


def _ssd_kernel(z_ref, x_ref, bm_ref, cm_ref, dt_ref, hx_ref, hb_ref, hc_ref, h0_ref,
                wx_ref, wb_ref, wc_ref, bx_ref, bb_ref, bc_ref, dtb_ref, alog_ref, dsk_ref, ng_ref,
                y_ref, ox_ref, ob_ref, oc_ref, h_ref,
                px_ref, pb_ref, pc_ref, yoff_ref, *, n_seq, l_q, g_step):
    rows = n_seq * l_q
    gs = pl.program_id(1)
    first = pl.program_id(2) == 0

    xs = _conv4(px_ref, x_ref[...], hx_ref, ox_ref, wx_ref, bx_ref, first, l_q).reshape(rows, g_step * GROUP_W)
    bm = _conv4(pb_ref, bm_ref[...], hb_ref, ob_ref, wb_ref, bb_ref, first, l_q).reshape(rows, g_step * SSD_STATE)
    cm = _conv4(pc_ref, cm_ref[...], hc_ref, oc_ref, wc_ref, bc_ref, first, l_q).reshape(rows, g_step * SSD_STATE)

    @pl.when(first)
    def _():
        h_ref[...] = h0_ref[...]

    lane = lax.broadcasted_iota(jnp.int32, (1, LANES), 1)
    dt_raw = dt_ref[...].reshape(rows, LANES) + dtb_ref[...]
    dt = jnp.maximum(dt_raw, 0.0) + jnp.log1p(jnp.exp(-jnp.abs(dt_raw)))
    dt = jnp.where(lane < SSD_HEADS, dt, 0.0)
    a = dt * (-jnp.exp(alog_ref[...]))
    if g_step != SSD_GROUPS:
        src = lax.broadcasted_iota(jnp.int32, (LANES, LANES), 0)
        dst = lax.broadcasted_iota(jnp.int32, (LANES, LANES), 1)
        pick = (src == dst + gs * (g_step * SSD_HG)).astype(F32)
        dt = _dot_exact(dt, pick)
        a = _dot_exact(a, pick)
    rq = lax.broadcasted_iota(jnp.int32, (rows, rows), 0)
    ck = lax.broadcasted_iota(jnp.int32, (rows, rows), 1)
    same = _iota_div((rows, rows), 0, l_q) == _iota_div((rows, rows), 1, l_q)
    causal = jnp.logical_and(same, ck <= rq)
    acs = _dot_exact(causal.astype(F32), a)
    tot = _dot_exact(same.astype(F32), a)
    acs_t = acs.T
    seq_of_col = _iota_div((1, rows), 1, l_q)

    for gi in range(g_step):
        lane0 = gi * SSD_HG
        xg = xs[:, gi * GROUP_W:(gi + 1) * GROUP_W]
        bg = bm[:, gi * SSD_STATE:(gi + 1) * SSD_STATE]
        cg = cm[:, gi * SSD_STATE:(gi + 1) * SSD_STATE]
        acs_x = _expand_heads(acs, lane0, GROUP_W)
        tot_x = _expand_heads(tot, lane0, GROUP_W)
        xdt = xg * _expand_heads(dt, lane0, GROUP_W)
        xdt_b = xdt.astype(BF16)

        cb = _dot_nt(cg, bg)
        head_of_col = _iota_div((1, GROUP_W), 1, SSD_HEADDIM)
        y = jnp.zeros((rows, GROUP_W), F32)
        for h in range(SSD_HG):
            diff = acs[:, lane0 + h:lane0 + h + 1] - acs_t[lane0 + h:lane0 + h + 1, :]
            decay = jnp.where(causal, jnp.exp(jnp.where(causal, diff, 0.0)), 0.0)
            yh = jnp.dot((cb * decay).astype(BF16), xdt_b, preferred_element_type=F32)
            y = jnp.where(head_of_col == h, yh, y)

        xd_t = (xdt * jnp.exp(tot_x - acs_x)).T
        for s in range(n_seq):
            h_old = h_ref[s, lane0:lane0 + SSD_HG].reshape(GROUP_W, SSD_STATE)
            yoff_ref[s * l_q:(s + 1) * l_q, :] = _dot_nt(cg[s * l_q:(s + 1) * l_q, :], h_old)
            xd_s = xd_t if n_seq == 1 else jnp.where(seq_of_col == s, xd_t, 0.0)
            upd = _dot(xd_s, bg)
            dec = jnp.concatenate(
                [jnp.broadcast_to(jnp.exp(tot[s * l_q:s * l_q + 1, lane0 + h:lane0 + h + 1]),
                                  (SSD_HEADDIM, SSD_STATE)) for h in range(SSD_HG)], axis=0)
            h_ref[s, lane0:lane0 + SSD_HG] = (h_old * dec + upd).reshape(SSD_HG, SSD_HEADDIM, SSD_STATE)
        y = y + yoff_ref[...] * jnp.exp(acs_x) + dsk_ref[:, gi * GROUP_W:(gi + 1) * GROUP_W] * xg

        zg = z_ref[:, :, gi * GROUP_W:(gi + 1) * GROUP_W].reshape(rows, GROUP_W)
        v = y * _silu(zg)
        v = v * lax.rsqrt(jnp.mean(v * v, axis=-1, keepdims=True) + EPS)
        v = v * ng_ref[:, gi * GROUP_W:(gi + 1) * GROUP_W]
        y_ref[:, gi * GROUP_W:(gi + 1) * GROUP_W] = v.astype(BF16)


def _ssd(proj3, dt3, hist, h0, p, *, n_seq, l_q, g_step):
    nb, l_seq, _ = proj3.shape
    wx = g_step * GROUP_W
    wn = g_step * SSD_STATE
    grid = (nb // n_seq, SSD_GROUPS // g_step, l_seq // l_q)
    ob = (O_XBC + D_MODEL) // wn
    oc = ob + (SSD_GROUPS * SSD_STATE) // wn
    cb = D_MODEL // wn
    cc = cb + (SSD_GROUPS * SSD_STATE) // wn

    def rows_spec(width, first_block):
        return pl.BlockSpec((n_seq, l_q, width), lambda n, g, c: (n, c, first_block + g))

    def hist_spec(width, first_block):
        return pl.BlockSpec((n_seq, SSD_CONV - 1, width), lambda n, g, c: (n, 0, first_block + g))

    def par_spec(r, width, first_block):
        return pl.BlockSpec((r, width), lambda n, g, c: (0, first_block + g))

    state_spec = pl.BlockSpec((n_seq, g_step * SSD_HG, SSD_HEADDIM, SSD_STATE), lambda n, g, c: (n, g, 0, 0))
    full_lane = pl.BlockSpec((1, LANES), lambda n, g, c: (0, 0))
    kern = functools.partial(_ssd_kernel, n_seq=n_seq, l_q=l_q, g_step=g_step)
    return pl.pallas_call(
        kern,
        grid=grid,
        in_specs=[
            rows_spec(wx, 0), rows_spec(wx, O_XBC // wx), rows_spec(wn, ob), rows_spec(wn, oc),
            pl.BlockSpec((n_seq, l_q, LANES), lambda n, g, c: (n, c, 0)),
            hist_spec(wx, 0), hist_spec(wn, cb), hist_spec(wn, cc),
            state_spec,
            par_spec(SSD_CONV, wx, 0), par_spec(SSD_CONV, wn, cb), par_spec(SSD_CONV, wn, cc),
            par_spec(1, wx, 0), par_spec(1, wn, cb), par_spec(1, wn, cc),
            full_lane, full_lane, par_spec(1, wx, 0), par_spec(1, wx, 0),
        ],
        out_specs=[
            pl.BlockSpec((SSD_ROWS, wx), lambda n, g, c: (n * grid[2] + c, g)),
            hist_spec(wx, 0), hist_spec(wn, 0), hist_spec(wn, 0),
            state_spec,
        ],
        out_shape=[
            jax.ShapeDtypeStruct((nb * l_seq, D_MODEL), BF16),
            jax.ShapeDtypeStruct((nb, SSD_CONV - 1, D_MODEL), F32),
            jax.ShapeDtypeStruct((nb, SSD_CONV - 1, SSD_GROUPS * SSD_STATE), F32),
            jax.ShapeDtypeStruct((nb, SSD_CONV - 1, SSD_GROUPS * SSD_STATE), F32),
            jax.ShapeDtypeStruct((nb, SSD_HEADS, SSD_HEADDIM, SSD_STATE), F32),
        ],
        scratch_shapes=[
            pltpu.VMEM((n_seq, 8 + l_q, wx), F32),
            pltpu.VMEM((n_seq, 8 + l_q, wn), F32),
            pltpu.VMEM((n_seq, 8 + l_q, wn), F32),
            pltpu.VMEM((SSD_ROWS, GROUP_W), F32),
        ],
        compiler_params=_cparams(3),
        name="ssd",
    )(proj3, proj3, proj3, proj3, dt3, hist, hist, hist, h0,
      p["conv_w"], p["conv_w"], p["conv_w"], p["conv_b"], p["conv_b"], p["conv_b"],
      p["dt_bias"], p["a_log"], p["d_skip"], p["norm_g"])


CONF_HIST = CONF_KERNEL - 1
CONF_PAD0 = 32


def _conf_kernel(a_ref, b_ref, hist_ref, w_ref, bias_ref, g_ref, beta_ref, c_ref, hist_out_ref, pad_ref, cf_ref,
                 *, n_seq, l_t, rb):
    first = pl.program_id(1) == 0
    lo = CONF_PAD0 - CONF_HIST

    @pl.when(first)
    def _():
        pad_ref[:, lo:CONF_PAD0, :] = hist_ref[...]

    @pl.when(jnp.logical_not(first))
    def _():
        pad_ref[:, lo:CONF_PAD0, :] = pad_ref[:, lo + l_t:CONF_PAD0 + l_t, :]

    pad_ref[:, CONF_PAD0:CONF_PAD0 + l_t, :] = a_ref[...] * _sigmoid(b_ref[...])
    hist_out_ref[...] = pad_ref[:, lo + l_t:CONF_PAD0 + l_t, :]

    for s in range(n_seq):
        for r0 in range(0, l_t, rb):
            acc = jnp.broadcast_to(bias_ref[...], (rb, D_CONV))
            for j in range(CONF_KERNEL):
                acc = acc + w_ref[j:j + 1, :] * pad_ref[s, r0 + lo + j:r0 + lo + j + rb, :]
            cf_ref[s * l_t + r0:s * l_t + r0 + rb, :] = _silu(_layernorm(acc, g_ref[...], beta_ref[...]))
    c_ref[...] = cf_ref[...].astype(BF16)


def _conf(proj3, hist, p, *, n_seq, l_t):
    nb, l_seq, _ = proj3.shape
    rb = min(32, l_t)
    grid = (nb // n_seq, l_seq // l_t)
    par = lambda r: pl.BlockSpec((r, D_CONV), lambda n, t: (0, 0))
    hist_spec = pl.BlockSpec((n_seq, CONF_HIST, D_CONV), lambda n, t: (n, 0, 0))
    kern = functools.partial(_conf_kernel, n_seq=n_seq, l_t=l_t, rb=rb)
    return pl.pallas_call(
        kern,
        grid=grid,
        in_specs=[
            pl.BlockSpec((n_seq, l_t, D_CONV), lambda n, t: (n, t, P_GLU // D_CONV)),
            pl.BlockSpec((n_seq, l_t, D_CONV), lambda n, t: (n, t, P_GLU // D_CONV + 1)),
            hist_spec, par(CONF_KERNEL), par(1), par(1), par(1),
        ],
        out_specs=[pl.BlockSpec((n_seq * l_t, D_CONV), lambda n, t: (n * grid[1] + t, 0)), hist_spec],
        out_shape=[jax.ShapeDtypeStruct((nb * l_seq, D_CONV), BF16),
                   jax.ShapeDtypeStruct((nb, CONF_HIST, D_CONV), F32)],
        scratch_shapes=[pltpu.VMEM((n_seq, CONF_PAD0 + l_t, D_CONV), F32), pltpu.VMEM((n_seq * l_t, D_CONV), F32)],
        compiler_params=_cparams(2),
        name="conf",
    )(proj3, proj3, hist, p["conf_w"], p["conf_b"], p["conf_ln_g"], p["conf_ln_b"])


def _attn_kernel(q_ref, k_ref, v_ref, o_ref, *, n_seq):
    scale = MEM_HEAD_DIM ** -0.5
    for h in range(MEM_HEADS):
        cols = slice(h * MEM_HEAD_DIM, (h + 1) * MEM_HEAD_DIM)
        outs = []
        for s in range(n_seq):
            sc = _dot_nt(q_ref[s, :, cols], k_ref[s, :, cols]) * scale
            e = jnp.exp(sc - jnp.max(sc, axis=-1, keepdims=True))
            pr = e / jnp.sum(e, axis=-1, keepdims=True)
            outs.append(_dot(pr, v_ref[s, :, cols]))
        o_ref[:, cols] = jnp.concatenate(outs, axis=0).astype(BF16)


def _attn(proj3, mem_k, mem_v, *, n_seq, l_t):
    nb, l_seq, _ = proj3.shape
    grid = (nb // n_seq, l_seq // l_t)
    kv_spec = pl.BlockSpec((n_seq, MEM_LEN, ATT_W), lambda n, t: (n, 0, 0))
    return pl.pallas_call(
        functools.partial(_attn_kernel, n_seq=n_seq),
        grid=grid,
        in_specs=[pl.BlockSpec((n_seq, l_t, ATT_W), lambda n, t: (n, t, P_Q // ATT_W)), kv_spec, kv_spec],
        out_specs=pl.BlockSpec((n_seq * l_t, ATT_W), lambda n, t: (n * grid[1] + t, 0)),
        out_shape=jax.ShapeDtypeStruct((nb * l_seq, ATT_W), BF16),
        compiler_params=_cparams(2),
        name="attn",
    )(proj3, mem_k, mem_v)


MERGE_TM = 256


def _merge_kernel(x_ref, y_ref, c_ref, o_ref, g0_ref, g1_ref, g2_ref, bg_ref, wssd_ref, wconf_ref, wmem_ref,
                  wout_ref, lg_ref, lb_ref, l1g_ref, l1b_ref, h1_ref):
    merged = _sigmoid(g0_ref[...] + bg_ref[0:1, :]) * _dot(y_ref[...], wssd_ref[...])
    merged = merged + _sigmoid(g1_ref[...] + bg_ref[1:2, :]) * _dot(c_ref[...], wconf_ref[...])
    merged = merged + _sigmoid(g2_ref[...] + bg_ref[2:3, :]) * _dot(o_ref[...], wmem_ref[...])
    mix = _dot(merged, wout_ref[...])
    h = _layernorm(x_ref[...], lg_ref[...], lb_ref[...])
    h1_ref[...] = _layernorm(ALPHA * h + mix, l1g_ref[...], l1b_ref[...])


def _merge(x2d, y2d, c2d, o2d, proj2d, p):
    m = x2d.shape[0]
    tm = MERGE_TM
    rows = lambda w, blk: pl.BlockSpec((tm, w), lambda i: (i, blk))
    whole = lambda a: pl.BlockSpec(a.shape, lambda i: (0, 0), pipeline_mode=pl.Buffered(1))
    vec = pl.BlockSpec((1, D_MODEL), lambda i: (0, 0))
    g_blk = P_GATE // D_MODEL
    return pl.pallas_call(
        _merge_kernel,
        grid=(m // tm,),
        in_specs=[
            rows(D_MODEL, 0), rows(D_MODEL, 0), rows(D_CONV, 0), rows(ATT_W, 0),
            rows(D_MODEL, g_blk), rows(D_MODEL, g_blk + 1), rows(D_MODEL, g_blk + 2),
            pl.BlockSpec((3, D_MODEL), lambda i: (0, 0)),
            whole(p["w_br_ssd"]), whole(p["w_br_conf"]), whole(p["w_br_mem"]), whole(p["w_out"]),
            vec, vec, vec, vec,
        ],
        out_specs=rows(D_MODEL, 0),
        out_shape=jax.ShapeDtypeStruct((m, D_MODEL), F32),
        compiler_params=_cparams(1),
        name="merge",
    )(x2d, y2d, c2d, o2d, proj2d, proj2d, proj2d, p["b_gate"],
      p["w_br_ssd"], p["w_br_conf"], p["w_br_mem"], p["w_out"],
      p["ln_in_g"], p["ln_in_b"], p["ln1_g"], p["ln1_b"])


MLP_TM = 512
MLP_TF = 512


def _mlp_kernel(h_ref, w1_ref, b1_ref, w2_ref, b2_ref, g_ref, b_ref, out_ref, hb_ref):
    f = pl.program_id(1)

    @pl.when(f == 0)
    def _():
        hb_ref[...] = h_ref[...].astype(BF16)

    a = jnp.maximum(jnp.dot(hb_ref[...], w1_ref[...], preferred_element_type=F32) + b1_ref[...], 0.0)
    part = _dot(a * a, w2_ref[...])

    @pl.when(f == 0)
    def _():
        out_ref[...] = part

    @pl.when(f > 0)
    def _():
        out_ref[...] += part

    @pl.when(f == pl.num_programs(1) - 1)
    def _():
        out_ref[...] = _layernorm(ALPHA * h_ref[...] + (out_ref[...] + b2_ref[...]), g_ref[...], b_ref[...])


def _mlp(h2d, p):
    m = h2d.shape[0]
    tm, tf = MLP_TM, MLP_TF
    vec = pl.BlockSpec((1, D_MODEL), lambda i, f: (0, 0))
    return pl.pallas_call(
        _mlp_kernel,
        grid=(m // tm, D_FF // tf),
        in_specs=[
            pl.BlockSpec((tm, D_MODEL), lambda i, f: (i, 0)),
            pl.BlockSpec((D_MODEL, tf), lambda i, f: (0, f)),
            pl.BlockSpec((1, tf), lambda i, f: (0, f)),
            pl.BlockSpec((tf, D_MODEL), lambda i, f: (f, 0)),
            vec, vec, vec,
        ],
        out_specs=pl.BlockSpec((tm, D_MODEL), lambda i, f: (i, 0)),
        out_shape=jax.ShapeDtypeStruct((m, D_MODEL), F32),
        scratch_shapes=[pltpu.VMEM((tm, D_MODEL), BF16)],
        compiler_params=_cparams(2),
        name="mlp",
    )(h2d, p["w_ff1"], p["b_ff1"], p["w_ff2"], p["b_ff2"], p["ln2_g"], p["ln2_b"])


def _trunk(x, ssd_hist, ssm0, conf_hist, mem_k, mem_v, p, *, ssd_cfg, conf_cfg, attn_cfg):
    nb, l_seq, _ = x.shape
    x2d = x.reshape(nb * l_seq, D_MODEL)
    proj2d, dt2d = _proj(x2d, p["ln_in_g"], p["ln_in_b"], p["w_in"], p["w_in_b"])
    proj3 = proj2d.reshape(nb, l_seq, P_W)
    dt3 = dt2d.reshape(nb, l_seq, LANES)
    y, hx, hb, hc, ssm1 = _ssd(proj3, dt3, ssd_hist, ssm0, p, **ssd_cfg)
    c, conf_hist1 = _conf(proj3, conf_hist, p, **conf_cfg)
    o = _attn(proj3, mem_k, mem_v, **attn_cfg)
    h1 = _merge(x2d, y, c, o, proj2d, p)
    out = _mlp(h1, p).reshape(nb, l_seq, D_MODEL)
    return out, ssm1, jnp.concatenate([hx, hb, hc], axis=-1), conf_hist1


def kernel(x_prompt, x_sample, mem_prompt, state_ssm, state_ssd_conv, state_conf_conv, cache_mem_k, cache_mem_v, ln_in_g, ln_in_b, w_in, b_gate, ssd_conv_w, ssd_conv_b, ssd_dt_bias, ssd_a_log, ssd_d, ssd_norm_g, conf_dw_w, conf_dw_b, conf_ln_g, conf_ln_b, w_mem_k, w_mem_v, w_br_ssd, w_br_conf, w_br_mem, w_out, ln1_g, ln1_b, w_ff1, b_ff1, w_ff2, b_ff2, ln2_g, ln2_b):
    layer = 0
    row = lambda v: v.reshape(1, -1)
    lane_pad = lambda v: jnp.pad(v.reshape(1, -1), ((0, 0), (0, LANES - v.size)))
    p = {
        "ln_in_g": row(ln_in_g), "ln_in_b": row(ln_in_b),
        "w_in": w_in[layer], "w_in_b": w_in[layer][:, O_GLU:],
        "b_gate": b_gate[layer],
        "conv_w": ssd_conv_w[layer], "conv_b": row(ssd_conv_b[layer]),
        "dt_bias": lane_pad(ssd_dt_bias[layer]), "a_log": lane_pad(ssd_a_log[layer]),
        "d_skip": row(jnp.repeat(ssd_d[layer], SSD_HEADDIM)), "norm_g": row(ssd_norm_g[layer]),
        "conf_w": conf_dw_w[layer], "conf_b": row(conf_dw_b[layer]),
        "conf_ln_g": row(conf_ln_g[layer]), "conf_ln_b": row(conf_ln_b[layer]),
        "w_br_ssd": w_br_ssd[layer].astype(BF16), "w_br_conf": w_br_conf[layer].astype(BF16),
        "w_br_mem": w_br_mem[layer].astype(BF16), "w_out": w_out[layer].astype(BF16),
        "ln1_g": row(ln1_g[layer]), "ln1_b": row(ln1_b[layer]),
        "w_ff1": w_ff1[layer].astype(BF16), "b_ff1": row(b_ff1[layer]),
        "w_ff2": w_ff2[layer].astype(BF16), "b_ff2": row(b_ff2[layer]),
        "ln2_g": row(ln2_g[layer]), "ln2_b": row(ln2_b[layer]),
    }
    n_p, l_p, _ = x_prompt.shape
    n_s, l_s, _ = x_sample.shape

    mem2d = mem_prompt.reshape(n_p * MEM_LEN, D_MODEL)
    p_mem_k = _matmul(mem2d, w_mem_k[layer]).reshape(n_p, MEM_LEN, ATT_W)
    p_mem_v = _matmul(mem2d, w_mem_v[layer]).reshape(n_p, MEM_LEN, ATT_W)
    y_prompt, p_ssm, p_ssd_conv, p_cc = _trunk(
        x_prompt,
        jnp.zeros((n_p, SSD_CONV - 1, CONV_DIM), F32),
        jnp.zeros((n_p, SSD_HEADS, SSD_HEADDIM, SSD_STATE), F32),
        jnp.zeros((n_p, CONF_HIST, D_CONV), F32),
        p_mem_k, p_mem_v, p,
        ssd_cfg=dict(n_seq=1, l_q=SSD_ROWS, g_step=SSD_GROUPS),
        conf_cfg=dict(n_seq=1, l_t=512),
        attn_cfg=dict(n_seq=1, l_t=512))

    y_sample, s_ssm, s_ssd_conv, s_cc = _trunk(
        x_sample, state_ssd_conv[layer], state_ssm[layer], state_conf_conv[layer],
        cache_mem_k[layer].reshape(n_s, MEM_LEN, ATT_W), cache_mem_v[layer].reshape(n_s, MEM_LEN, ATT_W), p,
        ssd_cfg=dict(n_seq=SSD_ROWS // l_s, l_q=l_s, g_step=1),
        conf_cfg=dict(n_seq=16, l_t=l_s),
        attn_cfg=dict(n_seq=4, l_t=l_s))

    kv_shape = (DEPTH, n_p, MEM_LEN, MEM_HEADS, MEM_HEAD_DIM)
    return (y_prompt, y_sample, p_ssm[None], p_ssd_conv[None], p_cc[None],
            p_mem_k.reshape(kv_shape), p_mem_v.reshape(kv_shape),
            s_ssm[None], s_ssd_conv[None], s_cc[None])
```

```python
import functools
import math

import jax
import jax.numpy as jnp
from jax import lax
from jax.experimental import pallas as pl
from jax.experimental.pallas import tpu as pltpu

F32 = jnp.float32
BF16 = jnp.bfloat16

D_MODEL = 2048
SSD_HEADDIM = 64
SSD_HEADS = 32
SSD_GROUPS = 8
SSD_HG = 4
SSD_STATE = 128
SSD_CONV = 4
CONV_DIM = 4096
D_CONV = 1024
CONF_KERNEL = 31
MEM_LEN = 256
MEM_HEADS = 4
MEM_HEAD_DIM = 256
ATT_W = 1024
D_FF = 8192
DEPTH = 1
ALPHA = (2.0 * DEPTH) ** 0.25
EPS = 1e-5

O_XBC = 2048
O_DT = 6144
O_GLU = 6176
P_GATE = 6144
P_GLU = 12288
P_Q = 14336
P_W = 15360

GROUP_W = SSD_HG * SSD_HEADDIM
SSD_ROWS = 128
LANES = 128
VMEM_LIMIT = 56 * 1024 * 1024


def _cparams(n_axes):
    return pltpu.CompilerParams(dimension_semantics=("arbitrary",) * n_axes, vmem_limit_bytes=VMEM_LIMIT)


def _layernorm(x, g, b):
    mu = jnp.mean(x, axis=-1, keepdims=True)
    xc = x - mu
    var = jnp.mean(xc * xc, axis=-1, keepdims=True)
    return xc * lax.rsqrt(var + EPS) * g + b


def _sigmoid(x):
    return 1.0 / (1.0 + jnp.exp(-x))


def _silu(x):
    return x * _sigmoid(x)


def _dot(a, b):
    return jnp.dot(a.astype(BF16), b.astype(BF16), preferred_element_type=F32)


def _dot_nt(a, b):
    return lax.dot_general(a.astype(BF16), b.astype(BF16), (((1,), (1,)), ((), ())), preferred_element_type=F32)


def _dot_exact(a, b):
    return jnp.dot(a, b, precision=lax.Precision.HIGHEST, preferred_element_type=F32)


def _iota_div(shape, axis, divisor):
    shift = divisor.bit_length() - 1
    assert 1 << shift == divisor
    return lax.shift_right_logical(lax.broadcasted_iota(jnp.int32, shape, axis), shift)


PROJ_TM = 1024
PROJ_TN = 1024
PROJ_LN_ROWS = 128
PROJ_NA = O_DT // PROJ_TN
PROJ_NB = (P_W - O_DT) // PROJ_TN


def _proj_kernel(x_ref, g_ref, b_ref, wt_ref, wdt_ref, out_ref, dt_ref, xn_ref):
    @pl.when(pl.program_id(1) == 0)
    def _():
        def ln_rows(r, carry):
            r0 = pl.multiple_of(r * PROJ_LN_ROWS, PROJ_LN_ROWS)
            xn = _layernorm(x_ref[pl.ds(r0, PROJ_LN_ROWS), :], g_ref[...], b_ref[...])
            xn_ref[pl.ds(r0, PROJ_LN_ROWS), :] = xn.astype(BF16)
            return carry
        lax.fori_loop(0, x_ref.shape[0] // PROJ_LN_ROWS, ln_rows, 0)
        dt_ref[...] = _dot_nt(xn_ref[...], wdt_ref[...])

    out_ref[...] = _dot_nt(xn_ref[...], wt_ref[...])


def _proj_w_row(j):
    assert O_GLU % 8 == 0 and PROJ_TN % 8 == 0
    return 8 * jnp.where(j < PROJ_NA, j * (PROJ_TN // 8), O_GLU // 8 + (j - PROJ_NA) * (PROJ_TN // 8))


def _proj_out_tile(j):
    jb = j - PROJ_NA
    n_glu_q = (P_W - P_GLU) // PROJ_TN
    tile_b = jnp.where(jb < n_glu_q, P_GLU // PROJ_TN + jb, P_GATE // PROJ_TN + jb - n_glu_q)
    return jnp.where(j < PROJ_NA, j, tile_b)


def _proj(x2d, ln_g, ln_b, w_in_t):
    m = x2d.shape[0]
    tm = min(PROJ_TM, m)
    grid = (m // tm, PROJ_NA + PROJ_NB)
    return pl.pallas_call(
        _proj_kernel,
        grid=grid,
        in_specs=[
            pl.BlockSpec((tm, D_MODEL), lambda i, j: (i, 0), pipeline_mode=pl.Buffered(1)),
            pl.BlockSpec((1, D_MODEL), lambda i, j: (0, 0)),
            pl.BlockSpec((1, D_MODEL), lambda i, j: (0, 0)),
            pl.BlockSpec((pl.Element(PROJ_TN), pl.Element(D_MODEL)), lambda i, j: (_proj_w_row(j), 0)),
            pl.BlockSpec((LANES, D_MODEL), lambda i, j: (O_DT // LANES, 0)),
        ],
        out_specs=[
            pl.BlockSpec((tm, PROJ_TN), lambda i, j: (i, _proj_out_tile(j))),
            pl.BlockSpec((tm, LANES), lambda i, j: (i, 0)),
        ],
        out_shape=[jax.ShapeDtypeStruct((m, P_W), F32), jax.ShapeDtypeStruct((m, LANES), F32)],
        scratch_shapes=[pltpu.VMEM((tm, D_MODEL), BF16)],
        compiler_params=_cparams(2),
        name="proj",
    )(x2d, ln_g, ln_b, w_in_t, w_in_t)


def _matmul_kernel(x_ref, w_ref, out_ref):
    out_ref[...] = _dot(x_ref[...], w_ref[...])


def _matmul(x2d, w, tn=512):
    m, k = x2d.shape
    n = w.shape[1]
    return pl.pallas_call(
        _matmul_kernel,
        grid=(n // tn,),
        in_specs=[pl.BlockSpec((m, k), lambda j: (0, 0)), pl.BlockSpec((k, tn), lambda j: (0, j))],
        out_specs=pl.BlockSpec((m, tn), lambda j: (0, j)),
        out_shape=jax.ShapeDtypeStruct((m, n), F32),
        compiler_params=_cparams(1),
        name="memkv",
    )(x2d, w)


def _split3(v):
    hi = v.astype(BF16)
    r1 = v - hi.astype(F32)
    mid = r1.astype(BF16)
    lo = (r1 - mid.astype(F32)).astype(BF16)
    return hi, mid, lo


def _expand_heads(pieces, onehot):
    out = jnp.dot(pieces[0], onehot, preferred_element_type=F32)
    out = out + jnp.dot(pieces[1], onehot, preferred_element_type=F32)
    return out + jnp.dot(pieces[2], onehot, preferred_element_type=F32)


def _conv4(pad_ref, u, hist_ref, hist_out_ref, w_ref, b_ref, first, l_q):
    @pl.when(first)
    def _():
        pad_ref[:, 5:8, :] = hist_ref[...]

    @pl.when(jnp.logical_not(first))
    def _():
        pad_ref[:, 5:8, :] = pad_ref[:, 5 + l_q:8 + l_q, :]

    pad_ref[:, 8:8 + l_q, :] = u
    acc = b_ref[...] + w_ref[0:1, :] * pad_ref[:, 5:5 + l_q, :]
    for j in range(1, SSD_CONV):
        acc = acc + w_ref[j:j + 1, :] * pad_ref[:, 5 + j:5 + j + l_q, :]
    hist_out_ref[...] = pad_ref[:, 5 + l_q:8 + l_q, :]
    return _silu(acc)


def _ssd_kernel(z_ref, x_ref, bm_ref, cm_ref, dt_ref, hx_ref, hb_ref, hc_ref, h0_ref,
                wx_ref, wb_ref, wc_ref, bx_ref, bb_ref, bc_ref, dtb_ref, alog_ref, dsk_ref, ng_ref,
                y_ref, ox_ref, ob_ref, oc_ref, h_ref,
                px_ref, pb_ref, pc_ref, yoff_ref, *, n_seq, l_q, g_step):
    rows = n_seq * l_q
    gs = pl.program_id(1)
    first = pl.program_id(2) == 0

    xs = _conv4(px_ref, x_ref[...], hx_ref, ox_ref, wx_ref, bx_ref, first, l_q).reshape(rows, g_step * GROUP_W)
    bm = _conv4(pb_ref, bm_ref[...], hb_ref, ob_ref, wb_ref, bb_ref, first, l_q).reshape(rows, g_step * SSD_STATE)
    cm = _conv4(pc_ref, cm_ref[...], hc_ref, oc_ref, wc_ref, bc_ref, first, l_q).reshape(rows, g_step * SSD_STATE)

    @pl.when(first)
    def _():
        h_ref[...] = h0_ref[...]

    lane = lax.broadcasted_iota(jnp.int32, (1, LANES), 1)
    dt_raw = dt_ref[...].reshape(rows, LANES) + dtb_ref[...]
    dt = jnp.maximum(dt_raw, 0.0) + jnp.log1p(jnp.exp(-jnp.abs(dt_raw)))
    dt = jnp.where(lane < SSD_HEADS, dt, 0.0)
    a = dt * (-jnp.exp(alog_ref[...]))
    if g_step != SSD_GROUPS:
        src = lax.broadcasted_iota(jnp.int32, (LANES, LANES), 0)
        dst = lax.broadcasted_iota(jnp.int32, (LANES, LANES), 1)
        pick = (src == dst + gs * (g_step * SSD_HG)).astype(F32)
        dt = _dot_exact(dt, pick)
        a = _dot_exact(a, pick)
    rq = lax.broadcasted_iota(jnp.int32, (rows, rows), 0)
    ck = lax.broadcasted_iota(jnp.int32, (rows, rows), 1)
    same = _iota_div((rows, rows), 0, l_q) == _iota_div((rows, rows), 1, l_q)
    causal = jnp.logical_and(same, ck <= rq)
    acs = _dot_exact(causal.astype(F32), a)
    tot = _dot_exact(same.astype(F32), a)
    acs_t = acs.T
    seq_of_col = _iota_div((1, rows), 1, l_q)

    dt_p = _split3(dt)
    to_end_p = _split3(jnp.exp(tot - acs))
    from_start_p = _split3(jnp.exp(acs))
    head_of_col = _iota_div((1, GROUP_W), 1, SSD_HEADDIM)

    for gi in range(g_step):
        lane0 = gi * SSD_HG
        xg = xs[:, gi * GROUP_W:(gi + 1) * GROUP_W]
        bg = bm[:, gi * SSD_STATE:(gi + 1) * SSD_STATE]
        cg = cm[:, gi * SSD_STATE:(gi + 1) * SSD_STATE]
        onehot = (lax.broadcasted_iota(jnp.int32, (LANES, GROUP_W), 0)
                  == _iota_div((LANES, GROUP_W), 1, SSD_HEADDIM) + lane0).astype(BF16)
        xdt = xg * _expand_heads(dt_p, onehot)
        xdt_b = xdt.astype(BF16)

        cbm = jnp.where(causal, _dot_nt(cg, bg), 0.0)
        m_heads, x_heads = [], []
        for h in range(SSD_HG):
            diff = acs[:, lane0 + h:lane0 + h + 1] - acs_t[lane0 + h:lane0 + h + 1, :]
            m_heads.append((cbm * jnp.exp(jnp.where(causal, diff, 0.0))).astype(BF16))
            x_heads.append(jnp.where(head_of_col == h, xdt_b, jnp.zeros_like(xdt_b)))
        y = jnp.dot(jnp.concatenate(m_heads, axis=1), jnp.concatenate(x_heads, axis=0),
                    preferred_element_type=F32)

        xd_t = (xdt * _expand_heads(to_end_p, onehot)).T
        for s in range(n_seq):
            h_old = h_ref[s, lane0:lane0 + SSD_HG].reshape(GROUP_W, SSD_STATE)
            yoff_ref[s * l_q:(s + 1) * l_q, :] = _dot_nt(cg[s * l_q:(s + 1) * l_q, :], h_old)
            xd_s = xd_t if n_seq == 1 else jnp.where(seq_of_col == s, xd_t, 0.0)
            upd = _dot(xd_s, bg)
            dec = jnp.concatenate(
                [jnp.broadcast_to(jnp.exp(tot[s * l_q:s * l_q + 1, lane0 + h:lane0 + h + 1]),
                                  (SSD_HEADDIM, SSD_STATE)) for h in range(SSD_HG)], axis=0)
            h_ref[s, lane0:lane0 + SSD_HG] = (h_old * dec + upd).reshape(SSD_HG, SSD_HEADDIM, SSD_STATE)
        y = y + yoff_ref[...] * _expand_heads(from_start_p, onehot) + dsk_ref[:, gi * GROUP_W:(gi + 1) * GROUP_W] * xg

        zg = z_ref[:, :, gi * GROUP_W:(gi + 1) * GROUP_W].reshape(rows, GROUP_W)
        v = y * _silu(zg)
        v = v * lax.rsqrt(jnp.mean(v * v, axis=-1, keepdims=True) + EPS)
        v = v * ng_ref[:, gi * GROUP_W:(gi + 1) * GROUP_W]
        y_ref[:, gi * GROUP_W:(gi + 1) * GROUP_W] = v.astype(BF16)


def _ssd(proj3, dt3, hist, h0, p, *, n_seq, l_q, g_step):
    nb, l_seq, _ = proj3.shape
    wx = g_step * GROUP_W
    wn = g_step * SSD_STATE
    grid = (nb // n_seq, SSD_GROUPS // g_step, l_seq // l_q)
    ob = (O_XBC + D_MODEL) // wn
    oc = ob + (SSD_GROUPS * SSD_STATE) // wn
    cb = D_MODEL // wn
    cc = cb + (SSD_GROUPS * SSD_STATE) // wn

    def rows_spec(width, first_block):
        return pl.BlockSpec((n_seq, l_q, width), lambda n, g, c: (n, c, first_block + g))

    def hist_spec(width, first_block):
        return pl.BlockSpec((n_seq, SSD_CONV - 1, width), lambda n, g, c: (n, 0, first_block + g))

    def par_spec(r, width, first_block):
        return pl.BlockSpec((r, width), lambda n, g, c: (0, first_block + g))

    state_spec = pl.BlockSpec((n_seq, g_step * SSD_HG, SSD_HEADDIM, SSD_STATE), lambda n, g, c: (n, g, 0, 0))
    full_lane = pl.BlockSpec((1, LANES), lambda n, g, c: (0, 0))
    kern = functools.partial(_ssd_kernel, n_seq=n_seq, l_q=l_q, g_step=g_step)
    return pl.pallas_call(
        kern,
        grid=grid,
        in_specs=[
            rows_spec(wx, 0), rows_spec(wx, O_XBC // wx), rows_spec(wn, ob), rows_spec(wn, oc),
            pl.BlockSpec((n_seq, l_q, LANES), lambda n, g, c: (n, c, 0)),
            hist_spec(wx, 0), hist_spec(wn, cb), hist_spec(wn, cc),
            state_spec,
            par_spec(SSD_CONV, wx, 0), par_spec(SSD_CONV, wn, cb), par_spec(SSD_CONV, wn, cc),
            par_spec(1, wx, 0), par_spec(1, wn, cb), par_spec(1, wn, cc),
            full_lane, full_lane, par_spec(1, wx, 0), par_spec(1, wx, 0),
        ],
        out_specs=[
            pl.BlockSpec((SSD_ROWS, wx), lambda n, g, c: (n * grid[2] + c, g)),
            hist_spec(wx, 0), hist_spec(wn, 0), hist_spec(wn, 0),
            state_spec,
        ],
        out_shape=[
            jax.ShapeDtypeStruct((nb * l_seq, D_MODEL), BF16),
            jax.ShapeDtypeStruct((nb, SSD_CONV - 1, D_MODEL), F32),
            jax.ShapeDtypeStruct((nb, SSD_CONV - 1, SSD_GROUPS * SSD_STATE), F32),
            jax.ShapeDtypeStruct((nb, SSD_CONV - 1, SSD_GROUPS * SSD_STATE), F32),
            jax.ShapeDtypeStruct((nb, SSD_HEADS, SSD_HEADDIM, SSD_STATE), F32),
        ],
        scratch_shapes=[
            pltpu.VMEM((n_seq, 8 + l_q, wx), F32),
            pltpu.VMEM((n_seq, 8 + l_q, wn), F32),
            pltpu.VMEM((n_seq, 8 + l_q, wn), F32),
            pltpu.VMEM((SSD_ROWS, GROUP_W), F32),
        ],
        compiler_params=_cparams(3),
        name="ssd",
    )(proj3, proj3, proj3, proj3, dt3, hist, hist, hist, h0,
      p["conv_w"], p["conv_w"], p["conv_w"], p["conv_b"], p["conv_b"], p["conv_b"],
      p["dt_bias"], p["a_log"], p["d_skip"], p["norm_g"])


CONF_HIST = CONF_KERNEL - 1
CONF_PAD0 = 32


SUBLANES = 8
CONF_NORM_ROWS = 64


def _conf_kernel(a_ref, b_ref, hist_ref, w_ref, bias_ref, g_ref, beta_ref, c_ref, hist_out_ref,
                 pad_ref, ph_ref, cf_ref, wrep_ref, *, n_seq, l_t, rb):
    first = pl.program_id(1) == 0
    lo = CONF_PAD0 - CONF_HIST
    rows = CONF_PAD0 + l_t

    @pl.when(first)
    def _():
        pad_ref[:, lo:CONF_PAD0, :] = hist_ref[...]

    @pl.when(jnp.logical_not(first))
    def _():
        pad_ref[:, lo:CONF_PAD0, :] = pad_ref[:, lo + l_t:CONF_PAD0 + l_t, :]

    pad_ref[:, CONF_PAD0:CONF_PAD0 + l_t, :] = a_ref[...] * _sigmoid(b_ref[...])
    hist_out_ref[...] = pad_ref[:, lo + l_t:CONF_PAD0 + l_t, :]

    for p in range(1, SUBLANES):
        ph_ref[p - 1, :, 0:rows - SUBLANES, :] = pad_ref[:, p:p + rows - SUBLANES, :]

    for j in range(CONF_KERNEL):
        wrep_ref[j] = jnp.broadcast_to(w_ref[j:j + 1, :], (SUBLANES, D_CONV))

    n_rb = l_t // rb

    def conv_block(i, carry):
        s = i // n_rb
        r0 = (i % n_rb) * rb
        n_tiles = rb // SUBLANES
        acc = [jnp.broadcast_to(bias_ref[...], (SUBLANES, D_CONV))] * n_tiles
        for j in range(CONF_KERNEL):
            p = (lo + j) % SUBLANES
            src = pad_ref if p == 0 else ph_ref.at[p - 1]
            wj = wrep_ref[j]
            for k in range(n_tiles):
                start = pl.multiple_of(r0 + (lo + j - p) + k * SUBLANES, SUBLANES)
                acc[k] = acc[k] + wj * src[s, pl.ds(start, SUBLANES), :]
        out_row = pl.multiple_of(s * l_t + r0, SUBLANES)
        cf_ref[pl.ds(out_row, rb), :] = jnp.concatenate(acc, axis=0)
        return carry

    lax.fori_loop(0, n_seq * n_rb, conv_block, 0)

    nb_rows = min(CONF_NORM_ROWS, n_seq * l_t)

    def norm_block(i, carry):
        r0 = pl.multiple_of(i * nb_rows, nb_rows)
        v = _silu(_layernorm(cf_ref[pl.ds(r0, nb_rows), :], g_ref[...], beta_ref[...]))
        c_ref[pl.ds(r0, nb_rows), :] = v.astype(BF16)
        return carry

    lax.fori_loop(0, (n_seq * l_t) // nb_rows, norm_block, 0)


def _conf(proj3, hist, p, *, n_seq, l_t):
    nb, l_seq, _ = proj3.shape
    rb = min(32, l_t)
    grid = (nb // n_seq, l_seq // l_t)
    par = lambda r: pl.BlockSpec((r, D_CONV), lambda n, t: (0, 0))
    hist_spec = pl.BlockSpec((n_seq, CONF_HIST, D_CONV), lambda n, t: (n, 0, 0))
    kern = functools.partial(_conf_kernel, n_seq=n_seq, l_t=l_t, rb=rb)
    return pl.pallas_call(
        kern,
        grid=grid,
        in_specs=[
            pl.BlockSpec((n_seq, l_t, D_CONV), lambda n, t: (n, t, P_GLU // D_CONV)),
            pl.BlockSpec((n_seq, l_t, D_CONV), lambda n, t: (n, t, P_GLU // D_CONV + 1)),
            hist_spec, par(CONF_KERNEL), par(1), par(1), par(1),
        ],
        out_specs=[pl.BlockSpec((n_seq * l_t, D_CONV), lambda n, t: (n * grid[1] + t, 0)), hist_spec],
        out_shape=[jax.ShapeDtypeStruct((nb * l_seq, D_CONV), BF16),
                   jax.ShapeDtypeStruct((nb, CONF_HIST, D_CONV), F32)],
        scratch_shapes=[pltpu.VMEM((n_seq, CONF_PAD0 + l_t, D_CONV), F32),
                        pltpu.VMEM((SUBLANES - 1, n_seq, CONF_PAD0 + l_t, D_CONV), F32),
                        pltpu.VMEM((n_seq * l_t, D_CONV), F32),
                        pltpu.VMEM((CONF_KERNEL, SUBLANES, D_CONV), F32)],
        compiler_params=_cparams(2),
        name="conf",
    )(proj3, proj3, hist, p["conf_w"], p["conf_b"], p["conf_ln_g"], p["conf_ln_b"])


def _attn_kernel(q_ref, k_ref, v_ref, o_ref, *, n_seq):
    scale = MEM_HEAD_DIM ** -0.5
    few_rows = q_ref.shape[1] < LANES
    for h in range(MEM_HEADS):
        cols = slice(h * MEM_HEAD_DIM, (h + 1) * MEM_HEAD_DIM)
        if few_rows:
            scores = [_dot_nt(k_ref[s, :, cols], q_ref[s, :, cols]) * scale for s in range(n_seq)]
            axis = 0
        else:
            scores = [_dot_nt(q_ref[s, :, cols], k_ref[s, :, cols]) * scale for s in range(n_seq)]
            axis = 1
        probs = []
        for sc in scores:
            e = jnp.exp(sc - jnp.max(sc, axis=axis, keepdims=True))
            probs.append((e / jnp.sum(e, axis=axis, keepdims=True)).astype(BF16))
        if few_rows:
            outs = [lax.dot_general(pr, v_ref[s, :, cols].astype(BF16), (((0,), (0,)), ((), ())),
                                    preferred_element_type=F32) for s, pr in enumerate(probs)]
        else:
            outs = [_dot(pr, v_ref[s, :, cols]) for s, pr in enumerate(probs)]
        o_ref[:, cols] = jnp.concatenate(outs, axis=0).astype(BF16)


def _attn(proj3, mem_k, mem_v, *, n_seq, l_t):
    nb, l_seq, _ = proj3.shape
    grid = (nb // n_seq, l_seq // l_t)
    kv_spec = pl.BlockSpec((n_seq, MEM_LEN, ATT_W), lambda n, t: (n, 0, 0))
    return pl.pallas_call(
        functools.partial(_attn_kernel, n_seq=n_seq),
        grid=grid,
        in_specs=[pl.BlockSpec((n_seq, l_t, ATT_W), lambda n, t: (n, t, P_Q // ATT_W)), kv_spec, kv_spec],
        out_specs=pl.BlockSpec((n_seq * l_t, ATT_W), lambda n, t: (n * grid[1] + t, 0)),
        out_shape=jax.ShapeDtypeStruct((nb * l_seq, ATT_W), BF16),
        compiler_params=_cparams(2),
        name="attn",
    )(proj3, mem_k, mem_v)


MERGE_TM = 256


def _merge_kernel(x_ref, y_ref, c_ref, o_ref, g0_ref, g1_ref, g2_ref, bg_ref, wssd_ref, wconf_ref, wmem_ref,
                  wout_ref, lg_ref, lb_ref, l1g_ref, l1b_ref, h1_ref):
    merged = _sigmoid(g0_ref[...] + bg_ref[0:1, :]) * _dot(y_ref[...], wssd_ref[...])
    merged = merged + _sigmoid(g1_ref[...] + bg_ref[1:2, :]) * _dot(c_ref[...], wconf_ref[...])
    merged = merged + _sigmoid(g2_ref[...] + bg_ref[2:3, :]) * _dot(o_ref[...], wmem_ref[...])
    mix = _dot(merged, wout_ref[...])
    h = _layernorm(x_ref[...], lg_ref[...], lb_ref[...])
    h1_ref[...] = _layernorm(ALPHA * h + mix, l1g_ref[...], l1b_ref[...])


def _merge(x2d, y2d, c2d, o2d, proj2d, p):
    m = x2d.shape[0]
    tm = MERGE_TM
    rows = lambda w, blk: pl.BlockSpec((tm, w), lambda i: (i, blk))
    whole = lambda a: pl.BlockSpec(a.shape, lambda i: (0, 0), pipeline_mode=pl.Buffered(1))
    vec = pl.BlockSpec((1, D_MODEL), lambda i: (0, 0))
    g_blk = P_GATE // D_MODEL
    return pl.pallas_call(
        _merge_kernel,
        grid=(m // tm,),
        in_specs=[
            rows(D_MODEL, 0), rows(D_MODEL, 0), rows(D_CONV, 0), rows(ATT_W, 0),
            rows(D_MODEL, g_blk), rows(D_MODEL, g_blk + 1), rows(D_MODEL, g_blk + 2),
            pl.BlockSpec((3, D_MODEL), lambda i: (0, 0)),
            whole(p["w_br_ssd"]), whole(p["w_br_conf"]), whole(p["w_br_mem"]), whole(p["w_out"]),
            vec, vec, vec, vec,
        ],
        out_specs=rows(D_MODEL, 0),
        out_shape=jax.ShapeDtypeStruct((m, D_MODEL), F32),
        compiler_params=_cparams(1),
        name="merge",
    )(x2d, y2d, c2d, o2d, proj2d, proj2d, proj2d, p["b_gate"],
      p["w_br_ssd"], p["w_br_conf"], p["w_br_mem"], p["w_out"],
      p["ln_in_g"], p["ln_in_b"], p["ln1_g"], p["ln1_b"])


MLP_TM = 512
MLP_TF = 1024


def _mlp_kernel(h_ref, w1_ref, b1_ref, w2_ref, b2_ref, g_ref, b_ref, out_ref, hb_ref):
    f = pl.program_id(1)

    @pl.when(f == 0)
    def _():
        hb_ref[...] = h_ref[...].astype(BF16)

    a = jnp.maximum(jnp.dot(hb_ref[...], w1_ref[...], preferred_element_type=F32) + b1_ref[...], 0.0)
    part = _dot(a * a, w2_ref[...])

    @pl.when(f == 0)
    def _():
        out_ref[...] = part

    @pl.when(f > 0)
    def _():
        out_ref[...] += part

    @pl.when(f == pl.num_programs(1) - 1)
    def _():
        out_ref[...] = _layernorm(ALPHA * h_ref[...] + (out_ref[...] + b2_ref[...]), g_ref[...], b_ref[...])


def _mlp(h2d, p):
    m = h2d.shape[0]
    tm, tf = MLP_TM, MLP_TF
    vec = pl.BlockSpec((1, D_MODEL), lambda i, f: (0, 0))
    return pl.pallas_call(
        _mlp_kernel,
        grid=(m // tm, D_FF // tf),
        in_specs=[
            pl.BlockSpec((tm, D_MODEL), lambda i, f: (i, 0)),
            pl.BlockSpec((D_MODEL, tf), lambda i, f: (0, f)),
            pl.BlockSpec((1, tf), lambda i, f: (0, f)),
            pl.BlockSpec((tf, D_MODEL), lambda i, f: (f, 0)),
            vec, vec, vec,
        ],
        out_specs=pl.BlockSpec((tm, D_MODEL), lambda i, f: (i, 0)),
        out_shape=jax.ShapeDtypeStruct((m, D_MODEL), F32),
        scratch_shapes=[pltpu.VMEM((tm, D_MODEL), BF16)],
        compiler_params=_cparams(2),
        name="mlp",
    )(h2d, p["w_ff1"], p["b_ff1"], p["w_ff2"], p["b_ff2"], p["ln2_g"], p["ln2_b"])


def _trunk(x, ssd_hist, ssm0, conf_hist, mem_k, mem_v, p, *, ssd_cfg, conf_cfg, attn_cfg):
    nb, l_seq, _ = x.shape
    x2d = x.reshape(nb * l_seq, D_MODEL)
    proj2d, dt2d = _proj(x2d, p["ln_in_g"], p["ln_in_b"], p["w_in_t"])
    proj3 = proj2d.reshape(nb, l_seq, P_W)
    dt3 = dt2d.reshape(nb, l_seq, LANES)
    y, hx, hb, hc, ssm1 = _ssd(proj3, dt3, ssd_hist, ssm0, p, **ssd_cfg)
    c, conf_hist1 = _conf(proj3, conf_hist, p, **conf_cfg)
    o = _attn(proj3, mem_k, mem_v, **attn_cfg)
    h1 = _merge(x2d, y, c, o, proj2d, p)
    out = _mlp(h1, p).reshape(nb, l_seq, D_MODEL)
    return out, ssm1, jnp.concatenate([hx, hb, hc], axis=-1), conf_hist1


def kernel(x_prompt, x_sample, mem_prompt, state_ssm, state_ssd_conv, state_conf_conv, cache_mem_k, cache_mem_v, ln_in_g, ln_in_b, w_in, b_gate, ssd_conv_w, ssd_conv_b, ssd_dt_bias, ssd_a_log, ssd_d, ssd_norm_g, conf_dw_w, conf_dw_b, conf_ln_g, conf_ln_b, w_mem_k, w_mem_v, w_br_ssd, w_br_conf, w_br_mem, w_out, ln1_g, ln1_b, w_ff1, b_ff1, w_ff2, b_ff2, ln2_g, ln2_b):
    layer = 0
    row = lambda v: v.reshape(1, -1)
    lane_pad = lambda v: jnp.pad(v.reshape(1, -1), ((0, 0), (0, LANES - v.size)))
    p = {
        "ln_in_g": row(ln_in_g), "ln_in_b": row(ln_in_b),
        "w_in_t": jnp.swapaxes(w_in[layer], 0, 1),
        "b_gate": b_gate[layer],
        "conv_w": ssd_conv_w[layer], "conv_b": row(ssd_conv_b[layer]),
        "dt_bias": lane_pad(ssd_dt_bias[layer]), "a_log": lane_pad(ssd_a_log[layer]),
        "d_skip": row(jnp.repeat(ssd_d[layer], SSD_HEADDIM)), "norm_g": row(ssd_norm_g[layer]),
        "conf_w": conf_dw_w[layer], "conf_b": row(conf_dw_b[layer]),
        "conf_ln_g": row(conf_ln_g[layer]), "conf_ln_b": row(conf_ln_b[layer]),
        "w_br_ssd": w_br_ssd[layer].astype(BF16), "w_br_conf": w_br_conf[layer].astype(BF16),
        "w_br_mem": w_br_mem[layer].astype(BF16), "w_out": w_out[layer].astype(BF16),
        "ln1_g": row(ln1_g[layer]), "ln1_b": row(ln1_b[layer]),
        "w_ff1": w_ff1[layer].astype(BF16), "b_ff1": row(b_ff1[layer]),
        "w_ff2": w_ff2[layer].astype(BF16), "b_ff2": row(b_ff2[layer]),
        "ln2_g": row(ln2_g[layer]), "ln2_b": row(ln2_b[layer]),
    }
    n_p, l_p, _ = x_prompt.shape
    n_s, l_s, _ = x_sample.shape

    mem2d = mem_prompt.reshape(n_p * MEM_LEN, D_MODEL)
    p_mem_k = _matmul(mem2d, w_mem_k[layer]).reshape(n_p, MEM_LEN, ATT_W)
    p_mem_v = _matmul(mem2d, w_mem_v[layer]).reshape(n_p, MEM_LEN, ATT_W)
    y_prompt, p_ssm, p_ssd_conv, p_cc = _trunk(
        x_prompt,
        jnp.zeros((n_p, SSD_CONV - 1, CONV_DIM), F32),
        jnp.zeros((n_p, SSD_HEADS, SSD_HEADDIM, SSD_STATE), F32),
        jnp.zeros((n_p, CONF_HIST, D_CONV), F32),
        p_mem_k, p_mem_v, p,
        ssd_cfg=dict(n_seq=1, l_q=SSD_ROWS, g_step=SSD_GROUPS),
        conf_cfg=dict(n_seq=1, l_t=512),
        attn_cfg=dict(n_seq=1, l_t=512))

    y_sample, s_ssm, s_ssd_conv, s_cc = _trunk(
        x_sample, state_ssd_conv[layer], state_ssm[layer], state_conf_conv[layer],
        cache_mem_k[layer].reshape(n_s, MEM_LEN, ATT_W).astype(BF16),
        cache_mem_v[layer].reshape(n_s, MEM_LEN, ATT_W).astype(BF16), p,
        ssd_cfg=dict(n_seq=SSD_ROWS // l_s, l_q=l_s, g_step=1),
        conf_cfg=dict(n_seq=8, l_t=l_s),
        attn_cfg=dict(n_seq=8, l_t=l_s))

    kv_shape = (DEPTH, n_p, MEM_LEN, MEM_HEADS, MEM_HEAD_DIM)
    return (y_prompt, y_sample, p_ssm[None], p_ssd_conv[None], p_cc[None],
            p_mem_k.reshape(kv_shape), p_mem_v.reshape(kv_shape),
            s_ssm[None], s_ssd_conv[None], s_cc[None])
```

```python
import functools
import math

import jax
import jax.numpy as jnp
from jax import lax
from jax.experimental import pallas as pl
from jax.experimental.pallas import tpu as pltpu

F32 = jnp.float32
BF16 = jnp.bfloat16

D_MODEL = 2048
SSD_HEADDIM = 64
SSD_HEADS = 32
SSD_GROUPS = 8
SSD_HG = 4
SSD_STATE = 128
SSD_CONV = 4
CONV_DIM = 4096
D_CONV = 1024
CONF_KERNEL = 31
MEM_LEN = 256
MEM_HEADS = 4
MEM_HEAD_DIM = 256
ATT_W = 1024
D_FF = 8192
DEPTH = 1
ALPHA = (2.0 * DEPTH) ** 0.25
EPS = 1e-5

O_XBC = 2048
O_DT = 6144
O_GLU = 6176
P_GATE = 6144
P_GLU = 12288
P_Q = 14336
P_W = 15360

GROUP_W = SSD_HG * SSD_HEADDIM
SSD_ROWS = 128
LANES = 128
VMEM_LIMIT = 56 * 1024 * 1024


def _cparams(n_axes):
    return pltpu.CompilerParams(dimension_semantics=("arbitrary",) * n_axes, vmem_limit_bytes=VMEM_LIMIT)


def _layernorm(x, g, b):
    mu = jnp.mean(x, axis=-1, keepdims=True)
    xc = x - mu
    var = jnp.mean(xc * xc, axis=-1, keepdims=True)
    return xc * lax.rsqrt(var + EPS) * g + b


def _sigmoid(x):
    return 1.0 / (1.0 + jnp.exp(-x))


def _silu(x):
    return x * _sigmoid(x)


def _dot(a, b):
    return jnp.dot(a.astype(BF16), b.astype(BF16), preferred_element_type=F32)


def _dot_nt(a, b):
    return lax.dot_general(a.astype(BF16), b.astype(BF16), (((1,), (1,)), ((), ())), preferred_element_type=F32)


def _dot_exact(a, b):
    return jnp.dot(a, b, precision=lax.Precision.HIGHEST, preferred_element_type=F32)


def _iota_div(shape, axis, divisor):
    shift = divisor.bit_length() - 1
    assert 1 << shift == divisor
    return lax.shift_right_logical(lax.broadcasted_iota(jnp.int32, shape, axis), shift)


PROJ_TM = 1024
PROJ_TN = 1024
PROJ_LN_ROWS = 128
PROJ_NA = O_DT // PROJ_TN
PROJ_NB = (P_W - O_DT) // PROJ_TN


def _proj_kernel(x_ref, g_ref, b_ref, wt_ref, wdt_ref, out_ref, dt_ref, xn_ref):
    @pl.when(pl.program_id(1) == 0)
    def _():
        def ln_rows(r, carry):
            r0 = pl.multiple_of(r * PROJ_LN_ROWS, PROJ_LN_ROWS)
            xn = _layernorm(x_ref[pl.ds(r0, PROJ_LN_ROWS), :], g_ref[...], b_ref[...])
            xn_ref[pl.ds(r0, PROJ_LN_ROWS), :] = xn.astype(BF16)
            return carry
        lax.fori_loop(0, x_ref.shape[0] // PROJ_LN_ROWS, ln_rows, 0)
        dt_ref[...] = _dot_nt(xn_ref[...], wdt_ref[...])

    out_ref[...] = _dot_nt(xn_ref[...], wt_ref[...])


def _proj_w_row(j):
    assert O_GLU % 8 == 0 and PROJ_TN % 8 == 0
    return 8 * jnp.where(j < PROJ_NA, j * (PROJ_TN // 8), O_GLU // 8 + (j - PROJ_NA) * (PROJ_TN // 8))


def _proj_out_tile(j):
    jb = j - PROJ_NA
    n_glu_q = (P_W - P_GLU) // PROJ_TN
    tile_b = jnp.where(jb < n_glu_q, P_GLU // PROJ_TN + jb, P_GATE // PROJ_TN + jb - n_glu_q)
    return jnp.where(j < PROJ_NA, j, tile_b)


def _proj(x2d, ln_g, ln_b, w_in_t):
    m = x2d.shape[0]
    tm = min(PROJ_TM, m)
    grid = (m // tm, PROJ_NA + PROJ_NB)
    return pl.pallas_call(
        _proj_kernel,
        grid=grid,
        in_specs=[
            pl.BlockSpec((tm, D_MODEL), lambda i, j: (i, 0), pipeline_mode=pl.Buffered(1)),
            pl.BlockSpec((1, D_MODEL), lambda i, j: (0, 0)),
            pl.BlockSpec((1, D_MODEL), lambda i, j: (0, 0)),
            pl.BlockSpec((pl.Element(PROJ_TN), pl.Element(D_MODEL)), lambda i, j: (_proj_w_row(j), 0)),
            pl.BlockSpec((LANES, D_MODEL), lambda i, j: (O_DT // LANES, 0)),
        ],
        out_specs=[
            pl.BlockSpec((tm, PROJ_TN), lambda i, j: (i, _proj_out_tile(j))),
            pl.BlockSpec((tm, LANES), lambda i, j: (i, 0)),
        ],
        out_shape=[jax.ShapeDtypeStruct((m, P_W), F32), jax.ShapeDtypeStruct((m, LANES), F32)],
        scratch_shapes=[pltpu.VMEM((tm, D_MODEL), BF16)],
        compiler_params=_cparams(2),
        name="proj",
    )(x2d, ln_g, ln_b, w_in_t, w_in_t)


def _matmul_kernel(x_ref, w_ref, out_ref):
    out_ref[...] = _dot(x_ref[...], w_ref[...])


def _matmul(x2d, w, tn=512):
    m, k = x2d.shape
    n = w.shape[1]
    return pl.pallas_call(
        _matmul_kernel,
        grid=(n // tn,),
        in_specs=[pl.BlockSpec((m, k), lambda j: (0, 0)), pl.BlockSpec((k, tn), lambda j: (0, j))],
        out_specs=pl.BlockSpec((m, tn), lambda j: (0, j)),
        out_shape=jax.ShapeDtypeStruct((m, n), F32),
        compiler_params=_cparams(1),
        name="memkv",
    )(x2d, w)


def _split3(v):
    hi = v.astype(BF16)
    r1 = v - hi.astype(F32)
    mid = r1.astype(BF16)
    lo = (r1 - mid.astype(F32)).astype(BF16)
    return hi, mid, lo


def _expand_heads(pieces, onehot):
    out = jnp.dot(pieces[0], onehot, preferred_element_type=F32)
    out = out + jnp.dot(pieces[1], onehot, preferred_element_type=F32)
    return out + jnp.dot(pieces[2], onehot, preferred_element_type=F32)


def _conv4(pad_ref, u, hist_ref, hist_out_ref, w_ref, b_ref, first, l_q):
    @pl.when(first)
    def _():
        pad_ref[:, 5:8, :] = hist_ref[...]

    @pl.when(jnp.logical_not(first))
    def _():
        pad_ref[:, 5:8, :] = pad_ref[:, 5 + l_q:8 + l_q, :]

    pad_ref[:, 8:8 + l_q, :] = u
    acc = b_ref[...] + w_ref[0:1, :] * pad_ref[:, 5:5 + l_q, :]
    for j in range(1, SSD_CONV):
        acc = acc + w_ref[j:j + 1, :] * pad_ref[:, 5 + j:5 + j + l_q, :]
    hist_out_ref[...] = pad_ref[:, 5 + l_q:8 + l_q, :]
    return _silu(acc)


def _ssd_kernel(z_ref, x_ref, bm_ref, cm_ref, dt_ref, hx_ref, hb_ref, hc_ref, h0_ref,
                wx_ref, wb_ref, wc_ref, bx_ref, bb_ref, bc_ref, dtb_ref, alog_ref, dsk_ref, ng_ref,
                y_ref, ox_ref, ob_ref, oc_ref, h_ref,
                px_ref, pb_ref, pc_ref, yoff_ref, *, n_seq, l_q, g_step, single_chunk):
    rows = n_seq * l_q
    gs = pl.program_id(1)
    first = pl.program_id(2) == 0

    xs = _conv4(px_ref, x_ref[...], hx_ref, ox_ref, wx_ref, bx_ref, first, l_q).reshape(rows, g_step * GROUP_W)
    bm = _conv4(pb_ref, bm_ref[...], hb_ref, ob_ref, wb_ref, bb_ref, first, l_q).reshape(rows, g_step * SSD_STATE)
    cm = _conv4(pc_ref, cm_ref[...], hc_ref, oc_ref, wc_ref, bc_ref, first, l_q).reshape(rows, g_step * SSD_STATE)

    if single_chunk:
        h_in_ref = h0_ref
    else:
        h_in_ref = h_ref

        @pl.when(first)
        def _():
            h_ref[...] = h0_ref[...]

    lane = lax.broadcasted_iota(jnp.int32, (1, LANES), 1)
    dt_raw = dt_ref[...].reshape(rows, LANES) + dtb_ref[...]
    dt = jnp.maximum(dt_raw, 0.0) + jnp.log1p(jnp.exp(-jnp.abs(dt_raw)))
    dt = jnp.where(lane < SSD_HEADS, dt, 0.0)
    a = dt * (-jnp.exp(alog_ref[...]))
    if g_step != SSD_GROUPS:
        src = lax.broadcasted_iota(jnp.int32, (LANES, LANES), 0)
        dst = lax.broadcasted_iota(jnp.int32, (LANES, LANES), 1)
        pick = (src == dst + gs * (g_step * SSD_HG)).astype(F32)
        dt = _dot_exact(dt, pick)
        a = _dot_exact(a, pick)
    rq = lax.broadcasted_iota(jnp.int32, (rows, rows), 0)
    ck = lax.broadcasted_iota(jnp.int32, (rows, rows), 1)
    same = _iota_div((rows, rows), 0, l_q) == _iota_div((rows, rows), 1, l_q)
    causal = jnp.logical_and(same, ck <= rq)
    acs = _dot_exact(causal.astype(F32), a)
    tot = _dot_exact(same.astype(F32), a)
    acs_t = acs.T
    seq_of_col = _iota_div((1, rows), 1, l_q)

    dt_p = _split3(dt)
    to_end_p = _split3(jnp.exp(tot - acs))
    from_start_p = _split3(jnp.exp(acs))
    head_of_col = _iota_div((1, GROUP_W), 1, SSD_HEADDIM)

    for gi in range(g_step):
        lane0 = gi * SSD_HG
        xg = xs[:, gi * GROUP_W:(gi + 1) * GROUP_W]
        bg = bm[:, gi * SSD_STATE:(gi + 1) * SSD_STATE]
        cg = cm[:, gi * SSD_STATE:(gi + 1) * SSD_STATE]
        onehot = (lax.broadcasted_iota(jnp.int32, (LANES, GROUP_W), 0)
                  == _iota_div((LANES, GROUP_W), 1, SSD_HEADDIM) + lane0).astype(BF16)
        xdt = xg * _expand_heads(dt_p, onehot)
        xdt_b = xdt.astype(BF16)

        cbm = jnp.where(causal, _dot_nt(cg, bg), 0.0)
        m_heads, x_heads = [], []
        for h in range(SSD_HG):
            diff = acs[:, lane0 + h:lane0 + h + 1] - acs_t[lane0 + h:lane0 + h + 1, :]
            m_heads.append((cbm * jnp.exp(jnp.where(causal, diff, 0.0))).astype(BF16))
            x_heads.append(jnp.where(head_of_col == h, xdt_b, jnp.zeros_like(xdt_b)))
        y = jnp.dot(jnp.concatenate(m_heads, axis=1), jnp.concatenate(x_heads, axis=0),
                    preferred_element_type=F32)

        xd_t = (xdt * _expand_heads(to_end_p, onehot)).T
        for s in range(n_seq):
            h_old = h_in_ref[s, lane0:lane0 + SSD_HG].reshape(GROUP_W, SSD_STATE)
            yoff_ref[s * l_q:(s + 1) * l_q, :] = _dot_nt(cg[s * l_q:(s + 1) * l_q, :], h_old)
            xd_s = xd_t if n_seq == 1 else jnp.where(seq_of_col == s, xd_t, 0.0)
            upd = _dot(xd_s, bg)
            dec = jnp.concatenate(
                [jnp.broadcast_to(jnp.exp(tot[s * l_q:s * l_q + 1, lane0 + h:lane0 + h + 1]),
                                  (SSD_HEADDIM, SSD_STATE)) for h in range(SSD_HG)], axis=0)
            h_ref[s, lane0:lane0 + SSD_HG] = (h_old * dec + upd).reshape(SSD_HG, SSD_HEADDIM, SSD_STATE)
        y = y + yoff_ref[...] * _expand_heads(from_start_p, onehot) + dsk_ref[:, gi * GROUP_W:(gi + 1) * GROUP_W] * xg

        zg = z_ref[:, :, gi * GROUP_W:(gi + 1) * GROUP_W].reshape(rows, GROUP_W)
        v = y * _silu(zg)
        v = v * lax.rsqrt(jnp.mean(v * v, axis=-1, keepdims=True) + EPS)
        v = v * ng_ref[:, gi * GROUP_W:(gi + 1) * GROUP_W]
        y_ref[:, gi * GROUP_W:(gi + 1) * GROUP_W] = v.astype(BF16)


def _ssd(proj3, dt3, hist, h0, p, *, n_seq, l_q, g_step):
    nb, l_seq, _ = proj3.shape
    wx = g_step * GROUP_W
    wn = g_step * SSD_STATE
    grid = (nb // n_seq, SSD_GROUPS // g_step, l_seq // l_q)
    ob = (O_XBC + D_MODEL) // wn
    oc = ob + (SSD_GROUPS * SSD_STATE) // wn
    cb = D_MODEL // wn
    cc = cb + (SSD_GROUPS * SSD_STATE) // wn

    def rows_spec(width, first_block):
        return pl.BlockSpec((n_seq, l_q, width), lambda n, g, c: (n, c, first_block + g))

    def hist_spec(width, first_block):
        return pl.BlockSpec((n_seq, SSD_CONV - 1, width), lambda n, g, c: (n, 0, first_block + g))

    def par_spec(r, width, first_block):
        return pl.BlockSpec((r, width), lambda n, g, c: (0, first_block + g))

    state_spec = pl.BlockSpec((n_seq, g_step * SSD_HG, SSD_HEADDIM, SSD_STATE), lambda n, g, c: (n, g, 0, 0))
    full_lane = pl.BlockSpec((1, LANES), lambda n, g, c: (0, 0))
    kern = functools.partial(_ssd_kernel, n_seq=n_seq, l_q=l_q, g_step=g_step, single_chunk=grid[2] == 1)
    return pl.pallas_call(
        kern,
        grid=grid,
        in_specs=[
            rows_spec(wx, 0), rows_spec(wx, O_XBC // wx), rows_spec(wn, ob), rows_spec(wn, oc),
            pl.BlockSpec((n_seq, l_q, LANES), lambda n, g, c: (n, c, 0)),
            hist_spec(wx, 0), hist_spec(wn, cb), hist_spec(wn, cc),
            state_spec,
            par_spec(SSD_CONV, wx, 0), par_spec(SSD_CONV, wn, cb), par_spec(SSD_CONV, wn, cc),
            par_spec(1, wx, 0), par_spec(1, wn, cb), par_spec(1, wn, cc),
            full_lane, full_lane, par_spec(1, wx, 0), par_spec(1, wx, 0),
        ],
        out_specs=[
            pl.BlockSpec((SSD_ROWS, wx), lambda n, g, c: (n * grid[2] + c, g)),
            hist_spec(wx, 0), hist_spec(wn, 0), hist_spec(wn, 0),
            state_spec,
        ],
        out_shape=[
            jax.ShapeDtypeStruct((nb * l_seq, D_MODEL), BF16),
            jax.ShapeDtypeStruct((nb, SSD_CONV - 1, D_MODEL), F32),
            jax.ShapeDtypeStruct((nb, SSD_CONV - 1, SSD_GROUPS * SSD_STATE), F32),
            jax.ShapeDtypeStruct((nb, SSD_CONV - 1, SSD_GROUPS * SSD_STATE), F32),
            jax.ShapeDtypeStruct((nb, SSD_HEADS, SSD_HEADDIM, SSD_STATE), F32),
        ],
        scratch_shapes=[
            pltpu.VMEM((n_seq, 8 + l_q, wx), F32),
            pltpu.VMEM((n_seq, 8 + l_q, wn), F32),
            pltpu.VMEM((n_seq, 8 + l_q, wn), F32),
            pltpu.VMEM((SSD_ROWS, GROUP_W), F32),
        ],
        compiler_params=_cparams(3),
        name="ssd",
    )(proj3, proj3, proj3, proj3, dt3, hist, hist, hist, h0,
      p["conv_w"], p["conv_w"], p["conv_w"], p["conv_b"], p["conv_b"], p["conv_b"],
      p["dt_bias"], p["a_log"], p["d_skip"], p["norm_g"])


CONF_HIST = CONF_KERNEL - 1
CONF_PAD0 = 32


SUBLANES = 8
CONF_NORM_ROWS = 64


def _conf_kernel(a_ref, b_ref, hist_ref, w_ref, bias_ref, g_ref, beta_ref, c_ref, hist_out_ref,
                 pad_ref, ph_ref, cf_ref, wrep_ref, *, n_seq, l_t, rb):
    first = pl.program_id(1) == 0
    lo = CONF_PAD0 - CONF_HIST
    rows = CONF_PAD0 + l_t

    @pl.when(first)
    def _():
        pad_ref[:, lo:CONF_PAD0, :] = hist_ref[...]

    @pl.when(jnp.logical_not(first))
    def _():
        pad_ref[:, lo:CONF_PAD0, :] = pad_ref[:, lo + l_t:CONF_PAD0 + l_t, :]

    pad_ref[:, CONF_PAD0:CONF_PAD0 + l_t, :] = a_ref[...] * _sigmoid(b_ref[...])
    hist_out_ref[...] = pad_ref[:, lo + l_t:CONF_PAD0 + l_t, :]

    for p in range(1, SUBLANES):
        ph_ref[p - 1, :, 0:rows - SUBLANES, :] = pad_ref[:, p:p + rows - SUBLANES, :]

    for j in range(CONF_KERNEL):
        wrep_ref[j] = jnp.broadcast_to(w_ref[j:j + 1, :], (SUBLANES, D_CONV))

    n_rb = l_t // rb

    def conv_block(i, carry):
        s = i // n_rb
        r0 = (i % n_rb) * rb
        n_tiles = rb // SUBLANES
        acc = [jnp.broadcast_to(bias_ref[...], (SUBLANES, D_CONV))] * n_tiles
        for j in range(CONF_KERNEL):
            p = (lo + j) % SUBLANES
            src = pad_ref if p == 0 else ph_ref.at[p - 1]
            wj = wrep_ref[j]
            for k in range(n_tiles):
                start = pl.multiple_of(r0 + (lo + j - p) + k * SUBLANES, SUBLANES)
                acc[k] = acc[k] + wj * src[s, pl.ds(start, SUBLANES), :]
        out_row = pl.multiple_of(s * l_t + r0, SUBLANES)
        cf_ref[pl.ds(out_row, rb), :] = jnp.concatenate(acc, axis=0)
        return carry

    lax.fori_loop(0, n_seq * n_rb, conv_block, 0)

    nb_rows = min(CONF_NORM_ROWS, n_seq * l_t)

    def norm_block(i, carry):
        r0 = pl.multiple_of(i * nb_rows, nb_rows)
        v = _silu(_layernorm(cf_ref[pl.ds(r0, nb_rows), :], g_ref[...], beta_ref[...]))
        c_ref[pl.ds(r0, nb_rows), :] = v.astype(BF16)
        return carry

    lax.fori_loop(0, (n_seq * l_t) // nb_rows, norm_block, 0)


def _conf(proj3, hist, p, *, n_seq, l_t):
    nb, l_seq, _ = proj3.shape
    rb = min(32, l_t)
    grid = (nb // n_seq, l_seq // l_t)
    par = lambda r: pl.BlockSpec((r, D_CONV), lambda n, t: (0, 0))
    hist_spec = pl.BlockSpec((n_seq, CONF_HIST, D_CONV), lambda n, t: (n, 0, 0))
    kern = functools.partial(_conf_kernel, n_seq=n_seq, l_t=l_t, rb=rb)
    return pl.pallas_call(
        kern,
        grid=grid,
        in_specs=[
            pl.BlockSpec((n_seq, l_t, D_CONV), lambda n, t: (n, t, P_GLU // D_CONV)),
            pl.BlockSpec((n_seq, l_t, D_CONV), lambda n, t: (n, t, P_GLU // D_CONV + 1)),
            hist_spec, par(CONF_KERNEL), par(1), par(1), par(1),
        ],
        out_specs=[pl.BlockSpec((n_seq * l_t, D_CONV), lambda n, t: (n * grid[1] + t, 0)), hist_spec],
        out_shape=[jax.ShapeDtypeStruct((nb * l_seq, D_CONV), BF16),
                   jax.ShapeDtypeStruct((nb, CONF_HIST, D_CONV), F32)],
        scratch_shapes=[pltpu.VMEM((n_seq, CONF_PAD0 + l_t, D_CONV), F32),
                        pltpu.VMEM((SUBLANES - 1, n_seq, CONF_PAD0 + l_t, D_CONV), F32),
                        pltpu.VMEM((n_seq * l_t, D_CONV), F32),
                        pltpu.VMEM((CONF_KERNEL, SUBLANES, D_CONV), F32)],
        compiler_params=_cparams(2),
        name="conf",
    )(proj3, proj3, hist, p["conf_w"], p["conf_b"], p["conf_ln_g"], p["conf_ln_b"])


def _attn_kernel(q_ref, k_ref, v_ref, o_ref, *, n_seq):
    scale = MEM_HEAD_DIM ** -0.5
    few_rows = q_ref.shape[1] < LANES
    for h in range(MEM_HEADS):
        cols = slice(h * MEM_HEAD_DIM, (h + 1) * MEM_HEAD_DIM)
        if few_rows:
            scores = [_dot_nt(k_ref[s, :, cols], q_ref[s, :, cols]) * scale for s in range(n_seq)]
            axis = 0
        else:
            scores = [_dot_nt(q_ref[s, :, cols], k_ref[s, :, cols]) * scale for s in range(n_seq)]
            axis = 1
        probs = []
        for sc in scores:
            e = jnp.exp(sc - jnp.max(sc, axis=axis, keepdims=True))
            probs.append((e / jnp.sum(e, axis=axis, keepdims=True)).astype(BF16))
        if few_rows:
            outs = [lax.dot_general(pr, v_ref[s, :, cols].astype(BF16), (((0,), (0,)), ((), ())),
                                    preferred_element_type=F32) for s, pr in enumerate(probs)]
        else:
            outs = [_dot(pr, v_ref[s, :, cols]) for s, pr in enumerate(probs)]
        o_ref[:, cols] = jnp.concatenate(outs, axis=0).astype(BF16)


def _attn(proj3, mem_k, mem_v, *, n_seq, l_t):
    nb, l_seq, _ = proj3.shape
    grid = (nb // n_seq, l_seq // l_t)
    kv_spec = pl.BlockSpec((n_seq, MEM_LEN, ATT_W), lambda n, t: (n, 0, 0))
    return pl.pallas_call(
        functools.partial(_attn_kernel, n_seq=n_seq),
        grid=grid,
        in_specs=[pl.BlockSpec((n_seq, l_t, ATT_W), lambda n, t: (n, t, P_Q // ATT_W)), kv_spec, kv_spec],
        out_specs=pl.BlockSpec((n_seq * l_t, ATT_W), lambda n, t: (n * grid[1] + t, 0)),
        out_shape=jax.ShapeDtypeStruct((nb * l_seq, ATT_W), BF16),
        compiler_params=_cparams(2),
        name="attn",
    )(proj3, mem_k, mem_v)


KV_HALVES = MEM_HEAD_DIM // LANES
KV_ROWS = MEM_LEN * MEM_HEADS * KV_HALVES


def _attn_rows_kernel(q_ref, k_ref, v_ref, o_ref, *, n_seq, l_t):
    scale = MEM_HEAD_DIM ** -0.5
    nq = MEM_HEADS * l_t
    col = lax.broadcasted_iota(jnp.int32, (nq, KV_ROWS), 1)
    valid = jnp.bitwise_and(col, MEM_HEADS * KV_HALVES - 1) == _iota_div((nq, KV_ROWS), 0, l_t)
    parts = []
    for s in range(n_seq):
        q2 = jnp.concatenate([q_ref[s, :, (h * KV_HALVES + c) * LANES:(h * KV_HALVES + c + 1) * LANES]
                              for c in range(KV_HALVES) for h in range(MEM_HEADS)], axis=0)
        parts.append(_dot_nt(q2, k_ref[s]))
    probs = []
    for part in parts:
        sc = (part[0:nq] + pltpu.roll(part[nq:2 * nq], KV_ROWS - MEM_HEADS, axis=1)) * scale
        sc = jnp.where(valid, sc, -jnp.inf)
        e = jnp.exp(sc - jnp.max(sc, axis=1, keepdims=True))
        pr = e / jnp.sum(e, axis=1, keepdims=True)
        probs.append(jnp.concatenate([pr, pltpu.roll(pr, MEM_HEADS, axis=1)], axis=0).astype(BF16))
    outs = [_dot(p2, v_ref[s]) for s, p2 in enumerate(probs)]
    for c in range(KV_HALVES):
        for h in range(MEM_HEADS):
            r0 = (c * MEM_HEADS + h) * l_t
            piece = jnp.concatenate([o[r0:r0 + l_t] for o in outs], axis=0)
            o_ref[:, (h * KV_HALVES + c) * LANES:(h * KV_HALVES + c + 1) * LANES] = piece.astype(BF16)


def _attn_rows(proj3, k_rows, v_rows, *, n_seq):
    nb, l_t, _ = proj3.shape
    assert l_t % SUBLANES == 0 and KV_HALVES == 2
    kv_spec = pl.BlockSpec((n_seq, KV_ROWS, LANES), lambda n: (n, 0, 0))
    return pl.pallas_call(
        functools.partial(_attn_rows_kernel, n_seq=n_seq, l_t=l_t),
        grid=(nb // n_seq,),
        in_specs=[pl.BlockSpec((n_seq, l_t, ATT_W), lambda n: (n, 0, P_Q // ATT_W)), kv_spec, kv_spec],
        out_specs=pl.BlockSpec((n_seq * l_t, ATT_W), lambda n: (n, 0)),
        out_shape=jax.ShapeDtypeStruct((nb * l_t, ATT_W), BF16),
        compiler_params=_cparams(1),
        name="attn_rows",
    )(proj3, k_rows, v_rows)


def _interleave_kv(kv):
    b = kv.shape[0]
    kv = kv.reshape(b, MEM_LEN, MEM_HEADS, KV_HALVES, LANES)
    return jnp.transpose(kv, (0, 1, 3, 2, 4)).reshape(b, KV_ROWS, LANES)


MERGE_TM = 256


def _merge_kernel(x_ref, y_ref, c_ref, o_ref, g0_ref, g1_ref, g2_ref, bg_ref, wssd_ref, wconf_ref, wmem_ref,
                  wout_ref, lg_ref, lb_ref, l1g_ref, l1b_ref, h1_ref):
    merged = _sigmoid(g0_ref[...] + bg_ref[0:1, :]) * _dot(y_ref[...], wssd_ref[...])
    merged = merged + _sigmoid(g1_ref[...] + bg_ref[1:2, :]) * _dot(c_ref[...], wconf_ref[...])
    merged = merged + _sigmoid(g2_ref[...] + bg_ref[2:3, :]) * _dot(o_ref[...], wmem_ref[...])
    mix = _dot(merged, wout_ref[...])
    h = _layernorm(x_ref[...], lg_ref[...], lb_ref[...])
    h1_ref[...] = _layernorm(ALPHA * h + mix, l1g_ref[...], l1b_ref[...])


def _merge(x2d, y2d, c2d, o2d, proj2d, p):
    m = x2d.shape[0]
    tm = MERGE_TM
    rows = lambda w, blk: pl.BlockSpec((tm, w), lambda i: (i, blk))
    whole = lambda a: pl.BlockSpec(a.shape, lambda i: (0, 0), pipeline_mode=pl.Buffered(1))
    vec = pl.BlockSpec((1, D_MODEL), lambda i: (0, 0))
    g_blk = P_GATE // D_MODEL
    return pl.pallas_call(
        _merge_kernel,
        grid=(m // tm,),
        in_specs=[
            rows(D_MODEL, 0), rows(D_MODEL, 0), rows(D_CONV, 0), rows(ATT_W, 0),
            rows(D_MODEL, g_blk), rows(D_MODEL, g_blk + 1), rows(D_MODEL, g_blk + 2),
            pl.BlockSpec((3, D_MODEL), lambda i: (0, 0)),
            whole(p["w_br_ssd"]), whole(p["w_br_conf"]), whole(p["w_br_mem"]), whole(p["w_out"]),
            vec, vec, vec, vec,
        ],
        out_specs=rows(D_MODEL, 0),
        out_shape=jax.ShapeDtypeStruct((m, D_MODEL), F32),
        compiler_params=_cparams(1),
        name="merge",
    )(x2d, y2d, c2d, o2d, proj2d, proj2d, proj2d, p["b_gate"],
      p["w_br_ssd"], p["w_br_conf"], p["w_br_mem"], p["w_out"],
      p["ln_in_g"], p["ln_in_b"], p["ln1_g"], p["ln1_b"])


MLP_TM = 512
MLP_TF = 1024


def _mlp_kernel(h_ref, w1_ref, b1_ref, w2_ref, b2_ref, g_ref, b_ref, out_ref, hb_ref):
    f = pl.program_id(1)

    @pl.when(f == 0)
    def _():
        hb_ref[...] = h_ref[...].astype(BF16)

    a = jnp.maximum(jnp.dot(hb_ref[...], w1_ref[...], preferred_element_type=F32) + b1_ref[...], 0.0)
    part = _dot(a * a, w2_ref[...])

    @pl.when(f == 0)
    def _():
        out_ref[...] = part

    @pl.when(f > 0)
    def _():
        out_ref[...] += part

    @pl.when(f == pl.num_programs(1) - 1)
    def _():
        out_ref[...] = _layernorm(ALPHA * h_ref[...] + (out_ref[...] + b2_ref[...]), g_ref[...], b_ref[...])


def _mlp(h2d, p):
    m = h2d.shape[0]
    tm, tf = MLP_TM, MLP_TF
    vec = pl.BlockSpec((1, D_MODEL), lambda i, f: (0, 0))
    return pl.pallas_call(
        _mlp_kernel,
        grid=(m // tm, D_FF // tf),
        in_specs=[
            pl.BlockSpec((tm, D_MODEL), lambda i, f: (i, 0)),
            pl.BlockSpec((D_MODEL, tf), lambda i, f: (0, f)),
            pl.BlockSpec((1, tf), lambda i, f: (0, f)),
            pl.BlockSpec((tf, D_MODEL), lambda i, f: (f, 0)),
            vec, vec, vec,
        ],
        out_specs=pl.BlockSpec((tm, D_MODEL), lambda i, f: (i, 0)),
        out_shape=jax.ShapeDtypeStruct((m, D_MODEL), F32),
        scratch_shapes=[pltpu.VMEM((tm, D_MODEL), BF16)],
        compiler_params=_cparams(2),
        name="mlp",
    )(h2d, p["w_ff1"], p["b_ff1"], p["w_ff2"], p["b_ff2"], p["ln2_g"], p["ln2_b"])


def _trunk(x, ssd_hist, ssm0, conf_hist, mem_k, mem_v, p, *, ssd_cfg, conf_cfg, attn_fn):
    nb, l_seq, _ = x.shape
    x2d = x.reshape(nb * l_seq, D_MODEL)
    proj2d, dt2d = _proj(x2d, p["ln_in_g"], p["ln_in_b"], p["w_in_t"])
    proj3 = proj2d.reshape(nb, l_seq, P_W)
    dt3 = dt2d.reshape(nb, l_seq, LANES)
    y, hx, hb, hc, ssm1 = _ssd(proj3, dt3, ssd_hist, ssm0, p, **ssd_cfg)
    c, conf_hist1 = _conf(proj3, conf_hist, p, **conf_cfg)
    o = attn_fn(proj3, mem_k, mem_v)
    h1 = _merge(x2d, y, c, o, proj2d, p)
    out = _mlp(h1, p).reshape(nb, l_seq, D_MODEL)
    return out, ssm1, jnp.concatenate([hx, hb, hc], axis=-1), conf_hist1


def kernel(x_prompt, x_sample, mem_prompt, state_ssm, state_ssd_conv, state_conf_conv, cache_mem_k, cache_mem_v, ln_in_g, ln_in_b, w_in, b_gate, ssd_conv_w, ssd_conv_b, ssd_dt_bias, ssd_a_log, ssd_d, ssd_norm_g, conf_dw_w, conf_dw_b, conf_ln_g, conf_ln_b, w_mem_k, w_mem_v, w_br_ssd, w_br_conf, w_br_mem, w_out, ln1_g, ln1_b, w_ff1, b_ff1, w_ff2, b_ff2, ln2_g, ln2_b):
    layer = 0
    row = lambda v: v.reshape(1, -1)
    lane_pad = lambda v: jnp.pad(v.reshape(1, -1), ((0, 0), (0, LANES - v.size)))
    p = {
        "ln_in_g": row(ln_in_g), "ln_in_b": row(ln_in_b),
        "w_in_t": jnp.swapaxes(w_in[layer], 0, 1),
        "b_gate": b_gate[layer],
        "conv_w": ssd_conv_w[layer], "conv_b": row(ssd_conv_b[layer]),
        "dt_bias": lane_pad(ssd_dt_bias[layer]), "a_log": lane_pad(ssd_a_log[layer]),
        "d_skip": row(jnp.repeat(ssd_d[layer], SSD_HEADDIM)), "norm_g": row(ssd_norm_g[layer]),
        "conf_w": conf_dw_w[layer], "conf_b": row(conf_dw_b[layer]),
        "conf_ln_g": row(conf_ln_g[layer]), "conf_ln_b": row(conf_ln_b[layer]),
        "w_br_ssd": w_br_ssd[layer].astype(BF16), "w_br_conf": w_br_conf[layer].astype(BF16),
        "w_br_mem": w_br_mem[layer].astype(BF16), "w_out": w_out[layer].astype(BF16),
        "ln1_g": row(ln1_g[layer]), "ln1_b": row(ln1_b[layer]),
        "w_ff1": w_ff1[layer].astype(BF16), "b_ff1": row(b_ff1[layer]),
        "w_ff2": w_ff2[layer].astype(BF16), "b_ff2": row(b_ff2[layer]),
        "ln2_g": row(ln2_g[layer]), "ln2_b": row(ln2_b[layer]),
    }
    n_p, l_p, _ = x_prompt.shape
    n_s, l_s, _ = x_sample.shape

    mem2d = mem_prompt.reshape(n_p * MEM_LEN, D_MODEL)
    p_mem_k = _matmul(mem2d, w_mem_k[layer]).reshape(n_p, MEM_LEN, ATT_W)
    p_mem_v = _matmul(mem2d, w_mem_v[layer]).reshape(n_p, MEM_LEN, ATT_W)
    y_prompt, p_ssm, p_ssd_conv, p_cc = _trunk(
        x_prompt,
        jnp.zeros((n_p, SSD_CONV - 1, CONV_DIM), F32),
        jnp.zeros((n_p, SSD_HEADS, SSD_HEADDIM, SSD_STATE), F32),
        jnp.zeros((n_p, CONF_HIST, D_CONV), F32),
        p_mem_k, p_mem_v, p,
        ssd_cfg=dict(n_seq=1, l_q=SSD_ROWS, g_step=SSD_GROUPS),
        conf_cfg=dict(n_seq=1, l_t=512),
        attn_fn=functools.partial(_attn, n_seq=1, l_t=512))

    y_sample, s_ssm, s_ssd_conv, s_cc = _trunk(
        x_sample, state_ssd_conv[layer], state_ssm[layer], state_conf_conv[layer],
        _interleave_kv(cache_mem_k[layer]), _interleave_kv(cache_mem_v[layer]), p,
        ssd_cfg=dict(n_seq=SSD_ROWS // l_s, l_q=l_s, g_step=1),
        conf_cfg=dict(n_seq=8, l_t=l_s),
        attn_fn=functools.partial(_attn_rows, n_seq=8))

    kv_shape = (DEPTH, n_p, MEM_LEN, MEM_HEADS, MEM_HEAD_DIM)
    return (y_prompt, y_sample, p_ssm[None], p_ssd_conv[None], p_cc[None],
            p_mem_k.reshape(kv_shape), p_mem_v.reshape(kv_shape),
            s_ssm[None], s_ssd_conv[None], s_cc[None])
```

```python
import functools
import math

import jax
import jax.numpy as jnp
from jax import lax
from jax.experimental import pallas as pl
from jax.experimental.pallas import tpu as pltpu

F32 = jnp.float32
BF16 = jnp.bfloat16

D_MODEL = 2048
SSD_HEADDIM = 64
SSD_HEADS = 32
SSD_GROUPS = 8
SSD_HG = 4
SSD_STATE = 128
SSD_CONV = 4
CONV_DIM = 4096
D_CONV = 1024
CONF_KERNEL = 31
MEM_LEN = 256
MEM_HEADS = 4
MEM_HEAD_DIM = 256
ATT_W = 1024
D_FF = 8192
DEPTH = 1
ALPHA = (2.0 * DEPTH) ** 0.25
EPS = 1e-5

O_XBC = 2048
O_DT = 6144
O_GLU = 6176
P_GATE = 6144
P_GLU = 12288
P_Q = 14336
P_W = 15360

GROUP_W = SSD_HG * SSD_HEADDIM
SSD_ROWS = 128
LANES = 128
BF16_ROWS = 16
VMEM_LIMIT = 56 * 1024 * 1024


def _cparams(n_axes):
    return pltpu.CompilerParams(dimension_semantics=("arbitrary",) * n_axes, vmem_limit_bytes=VMEM_LIMIT)


def _layernorm(x, g, b):
    mu = jnp.mean(x, axis=-1, keepdims=True)
    xc = x - mu
    var = jnp.mean(xc * xc, axis=-1, keepdims=True)
    return xc * lax.rsqrt(var + EPS) * g + b


def _sigmoid(x):
    return 1.0 / (1.0 + jnp.exp(-x))


def _silu(x):
    return x * _sigmoid(x)


def _dot(a, b):
    return jnp.dot(a.astype(BF16), b.astype(BF16), preferred_element_type=F32)


def _dot_nt(a, b):
    return lax.dot_general(a.astype(BF16), b.astype(BF16), (((1,), (1,)), ((), ())), preferred_element_type=F32)


def _dot_exact(a, b):
    return jnp.dot(a, b, precision=lax.Precision.HIGHEST, preferred_element_type=F32)


def _iota_div(shape, axis, divisor):
    shift = divisor.bit_length() - 1
    assert 1 << shift == divisor
    return lax.shift_right_logical(lax.broadcasted_iota(jnp.int32, shape, axis), shift)


PROJ_TM = 1024
PROJ_TN = 1024
PROJ_LN_ROWS = 128
PROJ_NA = O_DT // PROJ_TN
PROJ_NB = (P_W - O_DT) // PROJ_TN


def _proj_kernel(x_ref, g_ref, b_ref, wt_ref, wdt_ref, out_ref, dt_ref, xn_ref):
    @pl.when(pl.program_id(1) == 0)
    def _():
        def ln_rows(r, carry):
            r0 = pl.multiple_of(r * PROJ_LN_ROWS, PROJ_LN_ROWS)
            xn = _layernorm(x_ref[pl.ds(r0, PROJ_LN_ROWS), :], g_ref[...], b_ref[...])
            xn_ref[pl.ds(r0, PROJ_LN_ROWS), :] = xn.astype(BF16)
            return carry
        lax.fori_loop(0, x_ref.shape[0] // PROJ_LN_ROWS, ln_rows, 0)
        dt_ref[...] = _dot_nt(xn_ref[...], wdt_ref[...])

    out_ref[...] = _dot_nt(xn_ref[...], wt_ref[...])


def _proj_w_row(j):
    t = BF16_ROWS
    assert O_GLU % t == 0 and PROJ_TN % t == 0
    return t * jnp.where(j < PROJ_NA, j * (PROJ_TN // t), O_GLU // t + (j - PROJ_NA) * (PROJ_TN // t))


def _proj_out_tile(j):
    jb = j - PROJ_NA
    n_glu_q = (P_W - P_GLU) // PROJ_TN
    tile_b = jnp.where(jb < n_glu_q, P_GLU // PROJ_TN + jb, P_GATE // PROJ_TN + jb - n_glu_q)
    return jnp.where(j < PROJ_NA, j, tile_b)


def _proj(x2d, ln_g, ln_b, w_in_t):
    m = x2d.shape[0]
    tm = min(PROJ_TM, m)
    grid = (m // tm, PROJ_NA + PROJ_NB)
    return pl.pallas_call(
        _proj_kernel,
        grid=grid,
        in_specs=[
            pl.BlockSpec((tm, D_MODEL), lambda i, j: (i, 0), pipeline_mode=pl.Buffered(1)),
            pl.BlockSpec((1, D_MODEL), lambda i, j: (0, 0)),
            pl.BlockSpec((1, D_MODEL), lambda i, j: (0, 0)),
            pl.BlockSpec((pl.Element(PROJ_TN), pl.Element(D_MODEL)), lambda i, j: (_proj_w_row(j), 0)),
            pl.BlockSpec((LANES, D_MODEL), lambda i, j: (O_DT // LANES, 0)),
        ],
        out_specs=[
            pl.BlockSpec((tm, PROJ_TN), lambda i, j: (i, _proj_out_tile(j))),
            pl.BlockSpec((tm, LANES), lambda i, j: (i, 0)),
        ],
        out_shape=[jax.ShapeDtypeStruct((m, P_W), F32), jax.ShapeDtypeStruct((m, LANES), F32)],
        scratch_shapes=[pltpu.VMEM((tm, D_MODEL), BF16)],
        compiler_params=_cparams(2),
        name="proj",
    )(x2d, ln_g, ln_b, w_in_t, w_in_t)


def _matmul_kernel(x_ref, w_ref, out_ref):
    out_ref[...] = _dot(x_ref[...], w_ref[...])


def _matmul(x2d, w, tn=512):
    m, k = x2d.shape
    n = w.shape[1]
    return pl.pallas_call(
        _matmul_kernel,
        grid=(n // tn,),
        in_specs=[pl.BlockSpec((m, k), lambda j: (0, 0)), pl.BlockSpec((k, tn), lambda j: (0, j))],
        out_specs=pl.BlockSpec((m, tn), lambda j: (0, j)),
        out_shape=jax.ShapeDtypeStruct((m, n), F32),
        compiler_params=_cparams(1),
        name="memkv",
    )(x2d, w)


def _split3(v):
    hi = v.astype(BF16)
    r1 = v - hi.astype(F32)
    mid = r1.astype(BF16)
    lo = (r1 - mid.astype(F32)).astype(BF16)
    return jnp.concatenate([hi, mid, lo], axis=1)


def _expand_heads(pieces, onehot3):
    return jnp.dot(pieces, onehot3, preferred_element_type=F32)


def _conv4(pad_ref, u, hist_ref, hist_out_ref, w_ref, b_ref, first, l_q):
    @pl.when(first)
    def _():
        pad_ref[:, 5:8, :] = hist_ref[...]

    @pl.when(jnp.logical_not(first))
    def _():
        pad_ref[:, 5:8, :] = pad_ref[:, 5 + l_q:8 + l_q, :]

    pad_ref[:, 8:8 + l_q, :] = u
    acc = b_ref[...] + w_ref[0:1, :] * pad_ref[:, 5:5 + l_q, :]
    for j in range(1, SSD_CONV):
        acc = acc + w_ref[j:j + 1, :] * pad_ref[:, 5 + j:5 + j + l_q, :]
    hist_out_ref[...] = pad_ref[:, 5 + l_q:8 + l_q, :]
    return _silu(acc)


def _ssd_kernel(z_ref, x_ref, bm_ref, cm_ref, dt_ref, hx_ref, hb_ref, hc_ref, h0_ref,
                wx_ref, wb_ref, wc_ref, bx_ref, bb_ref, bc_ref, dtb_ref, alog_ref, dsk_ref, ng_ref,
                y_ref, ox_ref, ob_ref, oc_ref, h_ref,
                px_ref, pb_ref, pc_ref, yoff_ref, *, n_seq, l_q, g_step, single_chunk):
    rows = n_seq * l_q
    gs = pl.program_id(1)
    first = pl.program_id(2) == 0

    xs = _conv4(px_ref, x_ref[...], hx_ref, ox_ref, wx_ref, bx_ref, first, l_q).reshape(rows, g_step * GROUP_W)
    bm = _conv4(pb_ref, bm_ref[...], hb_ref, ob_ref, wb_ref, bb_ref, first, l_q).reshape(rows, g_step * SSD_STATE)
    cm = _conv4(pc_ref, cm_ref[...], hc_ref, oc_ref, wc_ref, bc_ref, first, l_q).reshape(rows, g_step * SSD_STATE)

    if single_chunk:
        h_in_ref = h0_ref
    else:
        h_in_ref = h_ref

        @pl.when(first)
        def _():
            h_ref[...] = h0_ref[...]

    lane = lax.broadcasted_iota(jnp.int32, (1, LANES), 1)
    dt_raw = dt_ref[...].reshape(rows, LANES) + dtb_ref[...]
    dt = jnp.maximum(dt_raw, 0.0) + jnp.log1p(jnp.exp(-jnp.abs(dt_raw)))
    dt = jnp.where(lane < SSD_HEADS, dt, 0.0)
    a = dt * (-jnp.exp(alog_ref[...]))
    if g_step != SSD_GROUPS:
        src = lax.broadcasted_iota(jnp.int32, (LANES, LANES), 0)
        dst = lax.broadcasted_iota(jnp.int32, (LANES, LANES), 1)
        pick = (src == dst + gs * (g_step * SSD_HG)).astype(F32)
        dt = _dot_exact(dt, pick)
        a = _dot_exact(a, pick)
    rq = lax.broadcasted_iota(jnp.int32, (rows, rows), 0)
    ck = lax.broadcasted_iota(jnp.int32, (rows, rows), 1)
    same = _iota_div((rows, rows), 0, l_q) == _iota_div((rows, rows), 1, l_q)
    causal = jnp.logical_and(same, ck <= rq)
    acs = _dot_exact(causal.astype(F32), a)
    tot = _dot_exact(same.astype(F32), a)
    acs_t = acs.T
    seq_of_col = _iota_div((1, rows), 1, l_q)

    dt_p = _split3(dt)
    to_end_p = _split3(jnp.exp(tot - acs))
    from_start_p = _split3(jnp.exp(acs))
    head_of_col = _iota_div((1, GROUP_W), 1, SSD_HEADDIM)

    for gi in range(g_step):
        lane0 = gi * SSD_HG
        xg = xs[:, gi * GROUP_W:(gi + 1) * GROUP_W]
        bg = bm[:, gi * SSD_STATE:(gi + 1) * SSD_STATE]
        cg = cm[:, gi * SSD_STATE:(gi + 1) * SSD_STATE]
        piece_lane = jnp.bitwise_and(lax.broadcasted_iota(jnp.int32, (3 * LANES, GROUP_W), 0), LANES - 1)
        onehot = (piece_lane == _iota_div((3 * LANES, GROUP_W), 1, SSD_HEADDIM) + lane0).astype(BF16)
        xdt = xg * _expand_heads(dt_p, onehot)
        xdt_b = xdt.astype(BF16)

        cbm = jnp.where(causal, _dot_nt(cg, bg), 0.0)
        m_heads, x_heads = [], []
        for h in range(SSD_HG):
            diff = acs[:, lane0 + h:lane0 + h + 1] - acs_t[lane0 + h:lane0 + h + 1, :]
            m_heads.append((cbm * jnp.exp(jnp.where(causal, diff, 0.0))).astype(BF16))
            x_heads.append(jnp.where(head_of_col == h, xdt_b, jnp.zeros_like(xdt_b)))
        y = jnp.dot(jnp.concatenate(m_heads, axis=1), jnp.concatenate(x_heads, axis=0),
                    preferred_element_type=F32)

        xd_t = (xdt * _expand_heads(to_end_p, onehot)).T
        for s in range(n_seq):
            h_old = h_in_ref[s, lane0:lane0 + SSD_HG].reshape(GROUP_W, SSD_STATE)
            yoff_ref[s * l_q:(s + 1) * l_q, :] = _dot_nt(cg[s * l_q:(s + 1) * l_q, :], h_old)
            xd_s = xd_t if n_seq == 1 else jnp.where(seq_of_col == s, xd_t, 0.0)
            upd = _dot(xd_s, bg)
            dec = jnp.concatenate(
                [jnp.broadcast_to(jnp.exp(tot[s * l_q:s * l_q + 1, lane0 + h:lane0 + h + 1]),
                                  (SSD_HEADDIM, SSD_STATE)) for h in range(SSD_HG)], axis=0)
            h_ref[s, lane0:lane0 + SSD_HG] = (h_old * dec + upd).reshape(SSD_HG, SSD_HEADDIM, SSD_STATE)
        y = y + yoff_ref[...] * _expand_heads(from_start_p, onehot) + dsk_ref[:, gi * GROUP_W:(gi + 1) * GROUP_W] * xg

        zg = z_ref[:, :, gi * GROUP_W:(gi + 1) * GROUP_W].reshape(rows, GROUP_W)
        v = y * _silu(zg)
        v = v * lax.rsqrt(jnp.mean(v * v, axis=-1, keepdims=True) + EPS)
        v = v * ng_ref[:, gi * GROUP_W:(gi + 1) * GROUP_W]
        y_ref[:, gi * GROUP_W:(gi + 1) * GROUP_W] = v.astype(BF16)


def _ssd(proj3, dt3, hist, h0, p, *, n_seq, l_q, g_step):
    nb, l_seq, _ = proj3.shape
    wx = g_step * GROUP_W
    wn = g_step * SSD_STATE
    grid = (nb // n_seq, SSD_GROUPS // g_step, l_seq // l_q)
    ob = (O_XBC + D_MODEL) // wn
    oc = ob + (SSD_GROUPS * SSD_STATE) // wn
    cb = D_MODEL // wn
    cc = cb + (SSD_GROUPS * SSD_STATE) // wn

    def rows_spec(width, first_block):
        return pl.BlockSpec((n_seq, l_q, width), lambda n, g, c: (n, c, first_block + g))

    def hist_spec(width, first_block):
        return pl.BlockSpec((n_seq, SSD_CONV - 1, width), lambda n, g, c: (n, 0, first_block + g))

    def par_spec(r, width, first_block):
        return pl.BlockSpec((r, width), lambda n, g, c: (0, first_block + g))

    state_spec = pl.BlockSpec((n_seq, g_step * SSD_HG, SSD_HEADDIM, SSD_STATE), lambda n, g, c: (n, g, 0, 0))
    full_lane = pl.BlockSpec((1, LANES), lambda n, g, c: (0, 0))
    kern = functools.partial(_ssd_kernel, n_seq=n_seq, l_q=l_q, g_step=g_step, single_chunk=grid[2] == 1)
    return pl.pallas_call(
        kern,
        grid=grid,
        in_specs=[
            rows_spec(wx, 0), rows_spec(wx, O_XBC // wx), rows_spec(wn, ob), rows_spec(wn, oc),
            pl.BlockSpec((n_seq, l_q, LANES), lambda n, g, c: (n, c, 0)),
            hist_spec(wx, 0), hist_spec(wn, cb), hist_spec(wn, cc),
            state_spec,
            par_spec(SSD_CONV, wx, 0), par_spec(SSD_CONV, wn, cb), par_spec(SSD_CONV, wn, cc),
            par_spec(1, wx, 0), par_spec(1, wn, cb), par_spec(1, wn, cc),
            full_lane, full_lane, par_spec(1, wx, 0), par_spec(1, wx, 0),
        ],
        out_specs=[
            pl.BlockSpec((SSD_ROWS, wx), lambda n, g, c: (n * grid[2] + c, g)),
            hist_spec(wx, 0), hist_spec(wn, 0), hist_spec(wn, 0),
            state_spec,
        ],
        out_shape=[
            jax.ShapeDtypeStruct((nb * l_seq, D_MODEL), BF16),
            jax.ShapeDtypeStruct((nb, SSD_CONV - 1, D_MODEL), F32),
            jax.ShapeDtypeStruct((nb, SSD_CONV - 1, SSD_GROUPS * SSD_STATE), F32),
            jax.ShapeDtypeStruct((nb, SSD_CONV - 1, SSD_GROUPS * SSD_STATE), F32),
            jax.ShapeDtypeStruct((nb, SSD_HEADS, SSD_HEADDIM, SSD_STATE), F32),
        ],
        scratch_shapes=[
            pltpu.VMEM((n_seq, 8 + l_q, wx), F32),
            pltpu.VMEM((n_seq, 8 + l_q, wn), F32),
            pltpu.VMEM((n_seq, 8 + l_q, wn), F32),
            pltpu.VMEM((SSD_ROWS, GROUP_W), F32),
        ],
        compiler_params=_cparams(3),
        name="ssd",
    )(proj3, proj3, proj3, proj3, dt3, hist, hist, hist, h0,
      p["conv_w"], p["conv_w"], p["conv_w"], p["conv_b"], p["conv_b"], p["conv_b"],
      p["dt_bias"], p["a_log"], p["d_skip"], p["norm_g"])


CONF_HIST = CONF_KERNEL - 1
CONF_PAD0 = 32


SUBLANES = 8
CONF_NORM_ROWS = 128


def _conf_kernel(a_ref, b_ref, hist_ref, w_ref, bias_ref, g_ref, beta_ref, c_ref, hist_out_ref,
                 pad_ref, ph_ref, cf_ref, wrep_ref, *, n_seq, l_t, rb):
    first = pl.program_id(1) == 0
    lo = CONF_PAD0 - CONF_HIST
    rows = CONF_PAD0 + l_t

    @pl.when(first)
    def _():
        pad_ref[:, lo:CONF_PAD0, :] = hist_ref[...]

    @pl.when(jnp.logical_not(first))
    def _():
        pad_ref[:, lo:CONF_PAD0, :] = pad_ref[:, lo + l_t:CONF_PAD0 + l_t, :]

    pad_ref[:, CONF_PAD0:CONF_PAD0 + l_t, :] = a_ref[...] * _sigmoid(b_ref[...])
    hist_out_ref[...] = pad_ref[:, lo + l_t:CONF_PAD0 + l_t, :]

    for p in range(1, SUBLANES):
        ph_ref[p - 1, :, 0:rows - SUBLANES, :] = pad_ref[:, p:p + rows - SUBLANES, :]

    for j in range(CONF_KERNEL):
        wrep_ref[j] = jnp.broadcast_to(w_ref[j:j + 1, :], (SUBLANES, D_CONV))

    n_rb = l_t // rb

    def conv_block(i, carry):
        s = i // n_rb
        r0 = (i % n_rb) * rb
        n_tiles = rb // SUBLANES
        out_row = pl.multiple_of(s * l_t + r0, SUBLANES)
        for c0 in range(0, D_CONV, LANES):
            lanes = slice(c0, c0 + LANES)
            acc = [jnp.broadcast_to(bias_ref[:, lanes], (SUBLANES, LANES))] * n_tiles
            for j in range(CONF_KERNEL):
                p = (lo + j) % SUBLANES
                src = pad_ref if p == 0 else ph_ref.at[p - 1]
                wj = wrep_ref[j, :, lanes]
                for k in range(n_tiles):
                    start = pl.multiple_of(r0 + (lo + j - p) + k * SUBLANES, SUBLANES)
                    acc[k] = acc[k] + wj * src[s, pl.ds(start, SUBLANES), lanes]
            cf_ref[pl.ds(out_row, rb), lanes] = jnp.concatenate(acc, axis=0)
        return carry

    lax.fori_loop(0, n_seq * n_rb, conv_block, 0)

    nb_rows = min(CONF_NORM_ROWS, n_seq * l_t)

    def norm_block(i, carry):
        r0 = pl.multiple_of(i * nb_rows, nb_rows)
        v = _silu(_layernorm(cf_ref[pl.ds(r0, nb_rows), :], g_ref[...], beta_ref[...]))
        c_ref[pl.ds(r0, nb_rows), :] = v.astype(BF16)
        return carry

    lax.fori_loop(0, (n_seq * l_t) // nb_rows, norm_block, 0)


def _conf(proj3, hist, p, *, n_seq, l_t):
    nb, l_seq, _ = proj3.shape
    rb = min(32, l_t)
    grid = (nb // n_seq, l_seq // l_t)
    par = lambda r: pl.BlockSpec((r, D_CONV), lambda n, t: (0, 0))
    hist_spec = pl.BlockSpec((n_seq, CONF_HIST, D_CONV), lambda n, t: (n, 0, 0))
    kern = functools.partial(_conf_kernel, n_seq=n_seq, l_t=l_t, rb=rb)
    return pl.pallas_call(
        kern,
        grid=grid,
        in_specs=[
            pl.BlockSpec((n_seq, l_t, D_CONV), lambda n, t: (n, t, P_GLU // D_CONV)),
            pl.BlockSpec((n_seq, l_t, D_CONV), lambda n, t: (n, t, P_GLU // D_CONV + 1)),
            hist_spec, par(CONF_KERNEL), par(1), par(1), par(1),
        ],
        out_specs=[pl.BlockSpec((n_seq * l_t, D_CONV), lambda n, t: (n * grid[1] + t, 0)), hist_spec],
        out_shape=[jax.ShapeDtypeStruct((nb * l_seq, D_CONV), BF16),
                   jax.ShapeDtypeStruct((nb, CONF_HIST, D_CONV), F32)],
        scratch_shapes=[pltpu.VMEM((n_seq, CONF_PAD0 + l_t, D_CONV), F32),
                        pltpu.VMEM((SUBLANES - 1, n_seq, CONF_PAD0 + l_t, D_CONV), F32),
                        pltpu.VMEM((n_seq * l_t, D_CONV), F32),
                        pltpu.VMEM((CONF_KERNEL, SUBLANES, D_CONV), F32)],
        compiler_params=_cparams(2),
        name="conf",
    )(proj3, proj3, hist, p["conf_w"], p["conf_b"], p["conf_ln_g"], p["conf_ln_b"])


def _attn_kernel(q_ref, k_ref, v_ref, o_ref, *, n_seq):
    scale = MEM_HEAD_DIM ** -0.5
    few_rows = q_ref.shape[1] < LANES
    for h in range(MEM_HEADS):
        cols = slice(h * MEM_HEAD_DIM, (h + 1) * MEM_HEAD_DIM)
        if few_rows:
            scores = [_dot_nt(k_ref[s, :, cols], q_ref[s, :, cols]) * scale for s in range(n_seq)]
            axis = 0
        else:
            scores = [_dot_nt(q_ref[s, :, cols], k_ref[s, :, cols]) * scale for s in range(n_seq)]
            axis = 1
        probs = []
        for sc in scores:
            e = jnp.exp(sc - jnp.max(sc, axis=axis, keepdims=True))
            probs.append((e / jnp.sum(e, axis=axis, keepdims=True)).astype(BF16))
        if few_rows:
            outs = [lax.dot_general(pr, v_ref[s, :, cols].astype(BF16), (((0,), (0,)), ((), ())),
                                    preferred_element_type=F32) for s, pr in enumerate(probs)]
        else:
            outs = [_dot(pr, v_ref[s, :, cols]) for s, pr in enumerate(probs)]
        o_ref[:, cols] = jnp.concatenate(outs, axis=0).astype(BF16)


def _attn(proj3, mem_k, mem_v, *, n_seq, l_t):
    nb, l_seq, _ = proj3.shape
    grid = (nb // n_seq, l_seq // l_t)
    kv_spec = pl.BlockSpec((n_seq, MEM_LEN, ATT_W), lambda n, t: (n, 0, 0))
    return pl.pallas_call(
        functools.partial(_attn_kernel, n_seq=n_seq),
        grid=grid,
        in_specs=[pl.BlockSpec((n_seq, l_t, ATT_W), lambda n, t: (n, t, P_Q // ATT_W)), kv_spec, kv_spec],
        out_specs=pl.BlockSpec((n_seq * l_t, ATT_W), lambda n, t: (n * grid[1] + t, 0)),
        out_shape=jax.ShapeDtypeStruct((nb * l_seq, ATT_W), BF16),
        compiler_params=_cparams(2),
        name="attn",
    )(proj3, mem_k, mem_v)


KV_HALVES = MEM_HEAD_DIM // LANES
KV_ROWS = MEM_LEN * MEM_HEADS * KV_HALVES


def _attn_rows_kernel(q_ref, k_ref, v_ref, o_ref, *, n_seq, l_t):
    scale = MEM_HEAD_DIM ** -0.5
    nq = MEM_HEADS * l_t
    col = lax.broadcasted_iota(jnp.int32, (nq, KV_ROWS), 1)
    valid = jnp.bitwise_and(col, MEM_HEADS * KV_HALVES - 1) == _iota_div((nq, KV_ROWS), 0, l_t)
    parts = []
    for s in range(n_seq):
        q2 = jnp.concatenate([q_ref[s, :, (h * KV_HALVES + c) * LANES:(h * KV_HALVES + c + 1) * LANES]
                              for c in range(KV_HALVES) for h in range(MEM_HEADS)], axis=0)
        parts.append(_dot_nt(q2, k_ref[s]))
    probs = []
    for part in parts:
        sc = (part[0:nq] + pltpu.roll(part[nq:2 * nq], KV_ROWS - MEM_HEADS, axis=1)) * scale
        sc = jnp.where(valid, sc, -jnp.inf)
        e = jnp.exp(sc - jnp.max(sc, axis=1, keepdims=True))
        pr = e / jnp.sum(e, axis=1, keepdims=True)
        probs.append(jnp.concatenate([pr, pltpu.roll(pr, MEM_HEADS, axis=1)], axis=0).astype(BF16))
    outs = [_dot(p2, v_ref[s]) for s, p2 in enumerate(probs)]
    for c in range(KV_HALVES):
        for h in range(MEM_HEADS):
            r0 = (c * MEM_HEADS + h) * l_t
            piece = jnp.concatenate([o[r0:r0 + l_t] for o in outs], axis=0)
            o_ref[:, (h * KV_HALVES + c) * LANES:(h * KV_HALVES + c + 1) * LANES] = piece.astype(BF16)


def _attn_rows(proj3, k_rows, v_rows, *, n_seq):
    nb, l_t, _ = proj3.shape
    assert l_t % SUBLANES == 0 and KV_HALVES == 2
    kv_spec = pl.BlockSpec((n_seq, KV_ROWS, LANES), lambda n: (n, 0, 0))
    return pl.pallas_call(
        functools.partial(_attn_rows_kernel, n_seq=n_seq, l_t=l_t),
        grid=(nb // n_seq,),
        in_specs=[pl.BlockSpec((n_seq, l_t, ATT_W), lambda n: (n, 0, P_Q // ATT_W)), kv_spec, kv_spec],
        out_specs=pl.BlockSpec((n_seq * l_t, ATT_W), lambda n: (n, 0)),
        out_shape=jax.ShapeDtypeStruct((nb * l_t, ATT_W), BF16),
        compiler_params=_cparams(1),
        name="attn_rows",
    )(proj3, k_rows, v_rows)


def _interleave_kv(kv):
    b = kv.shape[0]
    kv = kv.reshape(b, MEM_LEN, MEM_HEADS, KV_HALVES, LANES)
    return jnp.transpose(kv, (0, 1, 3, 2, 4)).reshape(b, KV_ROWS, LANES)


MERGE_TM = 256


def _merge_kernel(x_ref, y_ref, c_ref, o_ref, g0_ref, g1_ref, g2_ref, bg_ref, wssd_ref, wconf_ref, wmem_ref,
                  wout_ref, lg_ref, lb_ref, l1g_ref, l1b_ref, h1_ref):
    merged = _sigmoid(g0_ref[...] + bg_ref[0:1, :]) * _dot(y_ref[...], wssd_ref[...])
    merged = merged + _sigmoid(g1_ref[...] + bg_ref[1:2, :]) * _dot(c_ref[...], wconf_ref[...])
    merged = merged + _sigmoid(g2_ref[...] + bg_ref[2:3, :]) * _dot(o_ref[...], wmem_ref[...])
    mix = _dot(merged, wout_ref[...])
    h = _layernorm(x_ref[...], lg_ref[...], lb_ref[...])
    h1_ref[...] = _layernorm(ALPHA * h + mix, l1g_ref[...], l1b_ref[...])


def _merge(x2d, y2d, c2d, o2d, proj2d, p):
    m = x2d.shape[0]
    tm = MERGE_TM
    rows = lambda w, blk: pl.BlockSpec((tm, w), lambda i: (i, blk))
    whole = lambda a: pl.BlockSpec(a.shape, lambda i: (0, 0), pipeline_mode=pl.Buffered(1))
    vec = pl.BlockSpec((1, D_MODEL), lambda i: (0, 0))
    g_blk = P_GATE // D_MODEL
    return pl.pallas_call(
        _merge_kernel,
        grid=(m // tm,),
        in_specs=[
            rows(D_MODEL, 0), rows(D_MODEL, 0), rows(D_CONV, 0), rows(ATT_W, 0),
            rows(D_MODEL, g_blk), rows(D_MODEL, g_blk + 1), rows(D_MODEL, g_blk + 2),
            pl.BlockSpec((3, D_MODEL), lambda i: (0, 0)),
            whole(p["w_br_ssd"]), whole(p["w_br_conf"]), whole(p["w_br_mem"]), whole(p["w_out"]),
            vec, vec, vec, vec,
        ],
        out_specs=rows(D_MODEL, 0),
        out_shape=jax.ShapeDtypeStruct((m, D_MODEL), F32),
        compiler_params=_cparams(1),
        name="merge",
    )(x2d, y2d, c2d, o2d, proj2d, proj2d, proj2d, p["b_gate"],
      p["w_br_ssd"], p["w_br_conf"], p["w_br_mem"], p["w_out"],
      p["ln_in_g"], p["ln_in_b"], p["ln1_g"], p["ln1_b"])


MLP_TM = 512
MLP_TF = 1024


def _mlp_kernel(h_ref, w1_ref, b1_ref, w2_ref, b2_ref, g_ref, b_ref, out_ref, hb_ref):
    f = pl.program_id(1)

    @pl.when(f == 0)
    def _():
        hb_ref[...] = h_ref[...].astype(BF16)

    a = jnp.maximum(jnp.dot(hb_ref[...], w1_ref[...], preferred_element_type=F32) + b1_ref[...], 0.0)
    part = _dot(a * a, w2_ref[...])

    @pl.when(f == 0)
    def _():
        out_ref[...] = part

    @pl.when(f > 0)
    def _():
        out_ref[...] += part

    @pl.when(f == pl.num_programs(1) - 1)
    def _():
        out_ref[...] = _layernorm(ALPHA * h_ref[...] + (out_ref[...] + b2_ref[...]), g_ref[...], b_ref[...])


def _mlp(h2d, p):
    m = h2d.shape[0]
    tm, tf = MLP_TM, MLP_TF
    vec = pl.BlockSpec((1, D_MODEL), lambda i, f: (0, 0))
    return pl.pallas_call(
        _mlp_kernel,
        grid=(m // tm, D_FF // tf),
        in_specs=[
            pl.BlockSpec((tm, D_MODEL), lambda i, f: (i, 0)),
            pl.BlockSpec((D_MODEL, tf), lambda i, f: (0, f)),
            pl.BlockSpec((1, tf), lambda i, f: (0, f)),
            pl.BlockSpec((tf, D_MODEL), lambda i, f: (f, 0)),
            vec, vec, vec,
        ],
        out_specs=pl.BlockSpec((tm, D_MODEL), lambda i, f: (i, 0)),
        out_shape=jax.ShapeDtypeStruct((m, D_MODEL), F32),
        scratch_shapes=[pltpu.VMEM((tm, D_MODEL), BF16)],
        compiler_params=_cparams(2),
        name="mlp",
    )(h2d, p["w_ff1"], p["b_ff1"], p["w_ff2"], p["b_ff2"], p["ln2_g"], p["ln2_b"])


def _trunk(x, ssd_hist, ssm0, conf_hist, mem_k, mem_v, p, *, ssd_cfg, conf_cfg, attn_fn):
    nb, l_seq, _ = x.shape
    x2d = x.reshape(nb * l_seq, D_MODEL)
    proj2d, dt2d = _proj(x2d, p["ln_in_g"], p["ln_in_b"], p["w_in_t"])
    proj3 = proj2d.reshape(nb, l_seq, P_W)
    dt3 = dt2d.reshape(nb, l_seq, LANES)
    y, hx, hb, hc, ssm1 = _ssd(proj3, dt3, ssd_hist, ssm0, p, **ssd_cfg)
    c, conf_hist1 = _conf(proj3, conf_hist, p, **conf_cfg)
    o = attn_fn(proj3, mem_k, mem_v)
    h1 = _merge(x2d, y, c, o, proj2d, p)
    out = _mlp(h1, p).reshape(nb, l_seq, D_MODEL)
    return out, ssm1, jnp.concatenate([hx, hb, hc], axis=-1), conf_hist1


def kernel(x_prompt, x_sample, mem_prompt, state_ssm, state_ssd_conv, state_conf_conv, cache_mem_k, cache_mem_v, ln_in_g, ln_in_b, w_in, b_gate, ssd_conv_w, ssd_conv_b, ssd_dt_bias, ssd_a_log, ssd_d, ssd_norm_g, conf_dw_w, conf_dw_b, conf_ln_g, conf_ln_b, w_mem_k, w_mem_v, w_br_ssd, w_br_conf, w_br_mem, w_out, ln1_g, ln1_b, w_ff1, b_ff1, w_ff2, b_ff2, ln2_g, ln2_b):
    layer = 0
    row = lambda v: v.reshape(1, -1)
    lane_pad = lambda v: jnp.pad(v.reshape(1, -1), ((0, 0), (0, LANES - v.size)))
    p = {
        "ln_in_g": row(ln_in_g), "ln_in_b": row(ln_in_b),
        "w_in_t": jnp.swapaxes(w_in[layer], 0, 1).astype(BF16),
        "b_gate": b_gate[layer],
        "conv_w": ssd_conv_w[layer], "conv_b": row(ssd_conv_b[layer]),
        "dt_bias": lane_pad(ssd_dt_bias[layer]), "a_log": lane_pad(ssd_a_log[layer]),
        "d_skip": row(jnp.repeat(ssd_d[layer], SSD_HEADDIM)), "norm_g": row(ssd_norm_g[layer]),
        "conf_w": conf_dw_w[layer], "conf_b": row(conf_dw_b[layer]),
        "conf_ln_g": row(conf_ln_g[layer]), "conf_ln_b": row(conf_ln_b[layer]),
        "w_br_ssd": w_br_ssd[layer].astype(BF16), "w_br_conf": w_br_conf[layer].astype(BF16),
        "w_br_mem": w_br_mem[layer].astype(BF16), "w_out": w_out[layer].astype(BF16),
        "ln1_g": row(ln1_g[layer]), "ln1_b": row(ln1_b[layer]),
        "w_ff1": w_ff1[layer].astype(BF16), "b_ff1": row(b_ff1[layer]),
        "w_ff2": w_ff2[layer].astype(BF16), "b_ff2": row(b_ff2[layer]),
        "ln2_g": row(ln2_g[layer]), "ln2_b": row(ln2_b[layer]),
    }
    n_p, l_p, _ = x_prompt.shape
    n_s, l_s, _ = x_sample.shape

    mem2d = mem_prompt.reshape(n_p * MEM_LEN, D_MODEL)
    p_mem_k = _matmul(mem2d, w_mem_k[layer]).reshape(n_p, MEM_LEN, ATT_W)
    p_mem_v = _matmul(mem2d, w_mem_v[layer]).reshape(n_p, MEM_LEN, ATT_W)
    y_prompt, p_ssm, p_ssd_conv, p_cc = _trunk(
        x_prompt,
        jnp.zeros((n_p, SSD_CONV - 1, CONV_DIM), F32),
        jnp.zeros((n_p, SSD_HEADS, SSD_HEADDIM, SSD_STATE), F32),
        jnp.zeros((n_p, CONF_HIST, D_CONV), F32),
        p_mem_k, p_mem_v, p,
        ssd_cfg=dict(n_seq=1, l_q=SSD_ROWS, g_step=SSD_GROUPS),
        conf_cfg=dict(n_seq=1, l_t=512),
        attn_fn=functools.partial(_attn, n_seq=1, l_t=512))

    y_sample, s_ssm, s_ssd_conv, s_cc = _trunk(
        x_sample, state_ssd_conv[layer], state_ssm[layer], state_conf_conv[layer],
        _interleave_kv(cache_mem_k[layer]), _interleave_kv(cache_mem_v[layer]), p,
        ssd_cfg=dict(n_seq=SSD_ROWS // l_s, l_q=l_s, g_step=2),
        conf_cfg=dict(n_seq=8, l_t=l_s),
        attn_fn=functools.partial(_attn_rows, n_seq=8))

    kv_shape = (DEPTH, n_p, MEM_LEN, MEM_HEADS, MEM_HEAD_DIM)
    return (y_prompt, y_sample, p_ssm[None], p_ssd_conv[None], p_cc[None],
            p_mem_k.reshape(kv_shape), p_mem_v.reshape(kv_shape),
            s_ssm[None], s_ssd_conv[None], s_cc[None])
```

```python
import functools
import math

import jax
import jax.numpy as jnp
from jax import lax
from jax.experimental import pallas as pl
from jax.experimental.pallas import tpu as pltpu

F32 = jnp.float32
BF16 = jnp.bfloat16

D_MODEL = 2048
SSD_HEADDIM = 64
SSD_HEADS = 32
SSD_GROUPS = 8
SSD_HG = 4
SSD_STATE = 128
SSD_CONV = 4
CONV_DIM = 4096
D_CONV = 1024
CONF_KERNEL = 31
MEM_LEN = 256
MEM_HEADS = 4
MEM_HEAD_DIM = 256
ATT_W = 1024
D_FF = 8192
DEPTH = 1
ALPHA = (2.0 * DEPTH) ** 0.25
EPS = 1e-5

O_XBC = 2048
O_DT = 6144
O_GLU = 6176
P_GATE = 6144
P_GLU = 12288
P_Q = 14336
P_W = 15360

GROUP_W = SSD_HG * SSD_HEADDIM
SSD_ROWS = 128
LANES = 128
BF16_ROWS = 16
VMEM_LIMIT = 56 * 1024 * 1024


def _cparams(n_axes):
    return pltpu.CompilerParams(dimension_semantics=("arbitrary",) * n_axes, vmem_limit_bytes=VMEM_LIMIT)


def _layernorm(x, g, b):
    mu = jnp.mean(x, axis=-1, keepdims=True)
    xc = x - mu
    var = jnp.mean(xc * xc, axis=-1, keepdims=True)
    return xc * lax.rsqrt(var + EPS) * g + b


def _sigmoid(x):
    return 1.0 / (1.0 + jnp.exp(-x))


def _silu(x):
    return x * _sigmoid(x)


def _dot(a, b):
    return jnp.dot(a.astype(BF16), b.astype(BF16), preferred_element_type=F32)


def _dot_nt(a, b):
    return lax.dot_general(a.astype(BF16), b.astype(BF16), (((1,), (1,)), ((), ())), preferred_element_type=F32)


def _dot_exact(a, b):
    return jnp.dot(a, b, precision=lax.Precision.HIGHEST, preferred_element_type=F32)


def _iota_div(shape, axis, divisor):
    shift = divisor.bit_length() - 1
    assert 1 << shift == divisor
    return lax.shift_right_logical(lax.broadcasted_iota(jnp.int32, shape, axis), shift)


PROJ_TM = 1024
PROJ_TN = 1024
PROJ_LN_ROWS = 128
PROJ_NA = O_DT // PROJ_TN
PROJ_NB = (P_W - O_DT) // PROJ_TN


def _proj_kernel(x_ref, g_ref, b_ref, wt_ref, wdt_ref, out_ref, dt_ref, xn_ref):
    @pl.when(pl.program_id(1) == 0)
    def _():
        def ln_rows(r, carry):
            r0 = pl.multiple_of(r * PROJ_LN_ROWS, PROJ_LN_ROWS)
            xn = _layernorm(x_ref[pl.ds(r0, PROJ_LN_ROWS), :], g_ref[...], b_ref[...])
            xn_ref[pl.ds(r0, PROJ_LN_ROWS), :] = xn.astype(BF16)
            return carry
        lax.fori_loop(0, x_ref.shape[0] // PROJ_LN_ROWS, ln_rows, 0)
        dt_ref[...] = _dot_nt(xn_ref[...], wdt_ref[...])

    out_ref[...] = _dot_nt(xn_ref[...], wt_ref[...])


def _proj_w_row(j):
    t = BF16_ROWS
    assert O_GLU % t == 0 and PROJ_TN % t == 0
    return t * jnp.where(j < PROJ_NA, j * (PROJ_TN // t), O_GLU // t + (j - PROJ_NA) * (PROJ_TN // t))


def _proj_out_tile(j):
    jb = j - PROJ_NA
    n_glu_q = (P_W - P_GLU) // PROJ_TN
    tile_b = jnp.where(jb < n_glu_q, P_GLU // PROJ_TN + jb, P_GATE // PROJ_TN + jb - n_glu_q)
    return jnp.where(j < PROJ_NA, j, tile_b)


def _proj(x2d, ln_g, ln_b, w_in_t):
    m = x2d.shape[0]
    tm = min(PROJ_TM, m)
    grid = (m // tm, PROJ_NA + PROJ_NB)
    return pl.pallas_call(
        _proj_kernel,
        grid=grid,
        in_specs=[
            pl.BlockSpec((tm, D_MODEL), lambda i, j: (i, 0), pipeline_mode=pl.Buffered(1)),
            pl.BlockSpec((1, D_MODEL), lambda i, j: (0, 0)),
            pl.BlockSpec((1, D_MODEL), lambda i, j: (0, 0)),
            pl.BlockSpec((pl.Element(PROJ_TN), pl.Element(D_MODEL)), lambda i, j: (_proj_w_row(j), 0)),
            pl.BlockSpec((LANES, D_MODEL), lambda i, j: (O_DT // LANES, 0)),
        ],
        out_specs=[
            pl.BlockSpec((tm, PROJ_TN), lambda i, j: (i, _proj_out_tile(j))),
            pl.BlockSpec((tm, LANES), lambda i, j: (i, 0)),
        ],
        out_shape=[jax.ShapeDtypeStruct((m, P_W), F32), jax.ShapeDtypeStruct((m, LANES), F32)],
        scratch_shapes=[pltpu.VMEM((tm, D_MODEL), BF16)],
        compiler_params=_cparams(2),
        name="proj",
    )(x2d, ln_g, ln_b, w_in_t, w_in_t)


def _matmul_kernel(x_ref, w_ref, out_ref):
    out_ref[...] = _dot(x_ref[...], w_ref[...])


def _matmul(x2d, w, tn=512):
    m, k = x2d.shape
    n = w.shape[1]
    return pl.pallas_call(
        _matmul_kernel,
        grid=(n // tn,),
        in_specs=[pl.BlockSpec((m, k), lambda j: (0, 0)), pl.BlockSpec((k, tn), lambda j: (0, j))],
        out_specs=pl.BlockSpec((m, tn), lambda j: (0, j)),
        out_shape=jax.ShapeDtypeStruct((m, n), F32),
        compiler_params=_cparams(1),
        name="memkv",
    )(x2d, w)


def _split3(v):
    hi = v.astype(BF16)
    r1 = v - hi.astype(F32)
    mid = r1.astype(BF16)
    lo = (r1 - mid.astype(F32)).astype(BF16)
    return jnp.concatenate([hi, mid, lo], axis=1)


def _expand_heads(pieces, onehot3):
    return jnp.dot(pieces, onehot3, preferred_element_type=F32)


def _conv4(pad_ref, u, hist_ref, hist_out_ref, w_ref, b_ref, first, l_q):
    @pl.when(first)
    def _():
        pad_ref[:, 5:8, :] = hist_ref[...]

    @pl.when(jnp.logical_not(first))
    def _():
        pad_ref[:, 5:8, :] = pad_ref[:, 5 + l_q:8 + l_q, :]

    pad_ref[:, 8:8 + l_q, :] = u
    acc = b_ref[...] + w_ref[0:1, :] * pad_ref[:, 5:5 + l_q, :]
    for j in range(1, SSD_CONV):
        acc = acc + w_ref[j:j + 1, :] * pad_ref[:, 5 + j:5 + j + l_q, :]
    hist_out_ref[...] = pad_ref[:, 5 + l_q:8 + l_q, :]
    return _silu(acc)


def _ssd_kernel(z_ref, x_ref, bm_ref, cm_ref, dt_ref, hx_ref, hb_ref, hc_ref, h0_ref,
                wx_ref, wb_ref, wc_ref, bx_ref, bb_ref, bc_ref, dtb_ref, alog_ref, dsk_ref, ng_ref,
                y_ref, ox_ref, ob_ref, oc_ref, h_ref,
                px_ref, pb_ref, pc_ref, yoff_ref, *, n_seq, l_q, g_step, single_chunk):
    rows = n_seq * l_q
    gs = pl.program_id(1)
    first = pl.program_id(2) == 0

    xs = _conv4(px_ref, x_ref[...], hx_ref, ox_ref, wx_ref, bx_ref, first, l_q).reshape(rows, g_step * GROUP_W)
    bm = _conv4(pb_ref, bm_ref[...], hb_ref, ob_ref, wb_ref, bb_ref, first, l_q).reshape(rows, g_step * SSD_STATE)
    cm = _conv4(pc_ref, cm_ref[...], hc_ref, oc_ref, wc_ref, bc_ref, first, l_q).reshape(rows, g_step * SSD_STATE)

    if single_chunk:
        h_in_ref = h0_ref
    else:
        h_in_ref = h_ref

        @pl.when(first)
        def _():
            h_ref[...] = h0_ref[...]

    lane = lax.broadcasted_iota(jnp.int32, (1, LANES), 1)
    dt_raw = dt_ref[...].reshape(rows, LANES) + dtb_ref[...]
    dt = jnp.maximum(dt_raw, 0.0) + jnp.log1p(jnp.exp(-jnp.abs(dt_raw)))
    dt = jnp.where(lane < SSD_HEADS, dt, 0.0)
    a = dt * (-jnp.exp(alog_ref[...]))
    if g_step != SSD_GROUPS:
        src = lax.broadcasted_iota(jnp.int32, (LANES, LANES), 0)
        dst = lax.broadcasted_iota(jnp.int32, (LANES, LANES), 1)
        pick = (src == dst + gs * (g_step * SSD_HG)).astype(F32)
        dt = _dot_exact(dt, pick)
        a = _dot_exact(a, pick)
    rq = lax.broadcasted_iota(jnp.int32, (rows, rows), 0)
    ck = lax.broadcasted_iota(jnp.int32, (rows, rows), 1)
    same = _iota_div((rows, rows), 0, l_q) == _iota_div((rows, rows), 1, l_q)
    causal = jnp.logical_and(same, ck <= rq)
    acs = _dot_exact(causal.astype(F32), a)
    tot = _dot_exact(same.astype(F32), a)
    acs_t = acs.T
    seq_of_col = _iota_div((1, rows), 1, l_q)

    dt_p = _split3(dt)
    to_end_p = _split3(jnp.exp(tot - acs))
    from_start_p = _split3(jnp.exp(acs))
    head_of_col = _iota_div((1, GROUP_W), 1, SSD_HEADDIM)

    for gi in range(g_step):
        lane0 = gi * SSD_HG
        xg = xs[:, gi * GROUP_W:(gi + 1) * GROUP_W]
        bg = bm[:, gi * SSD_STATE:(gi + 1) * SSD_STATE]
        cg = cm[:, gi * SSD_STATE:(gi + 1) * SSD_STATE]
        piece_lane = jnp.bitwise_and(lax.broadcasted_iota(jnp.int32, (3 * LANES, GROUP_W), 0), LANES - 1)
        onehot = (piece_lane == _iota_div((3 * LANES, GROUP_W), 1, SSD_HEADDIM) + lane0).astype(BF16)
        xdt = xg * _expand_heads(dt_p, onehot)
        xdt_b = xdt.astype(BF16)

        cbm = jnp.where(causal, _dot_nt(cg, bg), 0.0)
        m_heads, x_heads = [], []
        for h in range(SSD_HG):
            diff = acs[:, lane0 + h:lane0 + h + 1] - acs_t[lane0 + h:lane0 + h + 1, :]
            m_heads.append((cbm * jnp.exp(jnp.where(causal, diff, 0.0))).astype(BF16))
            x_heads.append(jnp.where(head_of_col == h, xdt_b, jnp.zeros_like(xdt_b)))
        y = jnp.dot(jnp.concatenate(m_heads, axis=1), jnp.concatenate(x_heads, axis=0),
                    preferred_element_type=F32)

        xd_t = (xdt * _expand_heads(to_end_p, onehot)).T
        for s in range(n_seq):
            h_old = h_in_ref[s, lane0:lane0 + SSD_HG].reshape(GROUP_W, SSD_STATE)
            yoff_ref[s * l_q:(s + 1) * l_q, :] = _dot_nt(cg[s * l_q:(s + 1) * l_q, :], h_old)
            xd_s = xd_t if n_seq == 1 else jnp.where(seq_of_col == s, xd_t, 0.0)
            upd = _dot(xd_s, bg)
            dec = jnp.concatenate(
                [jnp.broadcast_to(jnp.exp(tot[s * l_q:s * l_q + 1, lane0 + h:lane0 + h + 1]),
                                  (SSD_HEADDIM, SSD_STATE)) for h in range(SSD_HG)], axis=0)
            h_ref[s, lane0:lane0 + SSD_HG] = (h_old * dec + upd).reshape(SSD_HG, SSD_HEADDIM, SSD_STATE)
        y = y + yoff_ref[...] * _expand_heads(from_start_p, onehot) + dsk_ref[:, gi * GROUP_W:(gi + 1) * GROUP_W] * xg

        zg = z_ref[:, :, gi * GROUP_W:(gi + 1) * GROUP_W].reshape(rows, GROUP_W)
        v = y * _silu(zg)
        v = v * lax.rsqrt(jnp.mean(v * v, axis=-1, keepdims=True) + EPS)
        v = v * ng_ref[:, gi * GROUP_W:(gi + 1) * GROUP_W]
        y_ref[:, gi * GROUP_W:(gi + 1) * GROUP_W] = v.astype(BF16)


def _ssd(proj3, dt3, hist, h0, p, *, n_seq, l_q, g_step):
    nb, l_seq, _ = proj3.shape
    wx = g_step * GROUP_W
    wn = g_step * SSD_STATE
    grid = (nb // n_seq, SSD_GROUPS // g_step, l_seq // l_q)
    ob = (O_XBC + D_MODEL) // wn
    oc = ob + (SSD_GROUPS * SSD_STATE) // wn
    cb = D_MODEL // wn
    cc = cb + (SSD_GROUPS * SSD_STATE) // wn

    def rows_spec(width, first_block):
        return pl.BlockSpec((n_seq, l_q, width), lambda n, g, c: (n, c, first_block + g))

    def hist_spec(width, first_block):
        return pl.BlockSpec((n_seq, SSD_CONV - 1, width), lambda n, g, c: (n, 0, first_block + g))

    def par_spec(r, width, first_block):
        return pl.BlockSpec((r, width), lambda n, g, c: (0, first_block + g))

    state_spec = pl.BlockSpec((n_seq, g_step * SSD_HG, SSD_HEADDIM, SSD_STATE), lambda n, g, c: (n, g, 0, 0))
    full_lane = pl.BlockSpec((1, LANES), lambda n, g, c: (0, 0))
    kern = functools.partial(_ssd_kernel, n_seq=n_seq, l_q=l_q, g_step=g_step, single_chunk=grid[2] == 1)
    return pl.pallas_call(
        kern,
        grid=grid,
        in_specs=[
            rows_spec(wx, 0), rows_spec(wx, O_XBC // wx), rows_spec(wn, ob), rows_spec(wn, oc),
            pl.BlockSpec((n_seq, l_q, LANES), lambda n, g, c: (n, c, 0)),
            hist_spec(wx, 0), hist_spec(wn, cb), hist_spec(wn, cc),
            state_spec,
            par_spec(SSD_CONV, wx, 0), par_spec(SSD_CONV, wn, cb), par_spec(SSD_CONV, wn, cc),
            par_spec(1, wx, 0), par_spec(1, wn, cb), par_spec(1, wn, cc),
            full_lane, full_lane, par_spec(1, wx, 0), par_spec(1, wx, 0),
        ],
        out_specs=[
            pl.BlockSpec((SSD_ROWS, wx), lambda n, g, c: (n * grid[2] + c, g)),
            hist_spec(wx, 0), hist_spec(wn, 0), hist_spec(wn, 0),
            state_spec,
        ],
        out_shape=[
            jax.ShapeDtypeStruct((nb * l_seq, D_MODEL), BF16),
            jax.ShapeDtypeStruct((nb, SSD_CONV - 1, D_MODEL), F32),
            jax.ShapeDtypeStruct((nb, SSD_CONV - 1, SSD_GROUPS * SSD_STATE), F32),
            jax.ShapeDtypeStruct((nb, SSD_CONV - 1, SSD_GROUPS * SSD_STATE), F32),
            jax.ShapeDtypeStruct((nb, SSD_HEADS, SSD_HEADDIM, SSD_STATE), F32),
        ],
        scratch_shapes=[
            pltpu.VMEM((n_seq, 8 + l_q, wx), F32),
            pltpu.VMEM((n_seq, 8 + l_q, wn), F32),
            pltpu.VMEM((n_seq, 8 + l_q, wn), F32),
            pltpu.VMEM((SSD_ROWS, GROUP_W), F32),
        ],
        compiler_params=_cparams(3),
        name="ssd",
    )(proj3, proj3, proj3, proj3, dt3, hist, hist, hist, h0,
      p["conv_w"], p["conv_w"], p["conv_w"], p["conv_b"], p["conv_b"], p["conv_b"],
      p["dt_bias"], p["a_log"], p["d_skip"], p["norm_g"])


CONF_HIST = CONF_KERNEL - 1
CONF_PAD0 = 32


SUBLANES = 8
CONF_NORM_ROWS = 128


def _conf_kernel(a_ref, b_ref, hist_ref, w_ref, bias_ref, g_ref, beta_ref, c_ref, hist_out_ref,
                 pad_ref, ph_ref, cf_ref, wrep_ref, *, n_seq, l_t, rb):
    first = pl.program_id(1) == 0
    lo = CONF_PAD0 - CONF_HIST
    rows = CONF_PAD0 + l_t

    @pl.when(first)
    def _():
        pad_ref[:, lo:CONF_PAD0, :] = hist_ref[...]

    @pl.when(jnp.logical_not(first))
    def _():
        pad_ref[:, lo:CONF_PAD0, :] = pad_ref[:, lo + l_t:CONF_PAD0 + l_t, :]

    pad_ref[:, CONF_PAD0:CONF_PAD0 + l_t, :] = a_ref[...] * _sigmoid(b_ref[...])
    hist_out_ref[...] = pad_ref[:, lo + l_t:CONF_PAD0 + l_t, :]

    for p in range(1, SUBLANES):
        ph_ref[p - 1, :, 0:rows - SUBLANES, :] = pad_ref[:, p:p + rows - SUBLANES, :]

    for j in range(CONF_KERNEL):
        wrep_ref[j] = jnp.broadcast_to(w_ref[j:j + 1, :], (SUBLANES, D_CONV))

    n_rb = l_t // rb

    def conv_block(i, carry):
        s = i // n_rb
        r0 = (i % n_rb) * rb
        n_tiles = rb // SUBLANES
        out_row = pl.multiple_of(s * l_t + r0, SUBLANES)
        for c0 in range(0, D_CONV, LANES):
            lanes = slice(c0, c0 + LANES)
            acc = [jnp.broadcast_to(bias_ref[:, lanes], (SUBLANES, LANES))] * n_tiles
            for j in range(CONF_KERNEL):
                p = (lo + j) % SUBLANES
                src = pad_ref if p == 0 else ph_ref.at[p - 1]
                wj = wrep_ref[j, :, lanes]
                for k in range(n_tiles):
                    start = pl.multiple_of(r0 + (lo + j - p) + k * SUBLANES, SUBLANES)
                    acc[k] = acc[k] + wj * src[s, pl.ds(start, SUBLANES), lanes]
            cf_ref[pl.ds(out_row, rb), lanes] = jnp.concatenate(acc, axis=0)
        return carry

    lax.fori_loop(0, n_seq * n_rb, conv_block, 0)

    nb_rows = min(CONF_NORM_ROWS, n_seq * l_t)

    def norm_block(i, carry):
        r0 = pl.multiple_of(i * nb_rows, nb_rows)
        v = _silu(_layernorm(cf_ref[pl.ds(r0, nb_rows), :], g_ref[...], beta_ref[...]))
        c_ref[pl.ds(r0, nb_rows), :] = v.astype(BF16)
        return carry

    lax.fori_loop(0, (n_seq * l_t) // nb_rows, norm_block, 0)


def _conf(proj3, hist, p, *, n_seq, l_t):
    nb, l_seq, _ = proj3.shape
    rb = min(32, l_t)
    grid = (nb // n_seq, l_seq // l_t)
    par = lambda r: pl.BlockSpec((r, D_CONV), lambda n, t: (0, 0))
    hist_spec = pl.BlockSpec((n_seq, CONF_HIST, D_CONV), lambda n, t: (n, 0, 0))
    kern = functools.partial(_conf_kernel, n_seq=n_seq, l_t=l_t, rb=rb)
    return pl.pallas_call(
        kern,
        grid=grid,
        in_specs=[
            pl.BlockSpec((n_seq, l_t, D_CONV), lambda n, t: (n, t, P_GLU // D_CONV)),
            pl.BlockSpec((n_seq, l_t, D_CONV), lambda n, t: (n, t, P_GLU // D_CONV + 1)),
            hist_spec, par(CONF_KERNEL), par(1), par(1), par(1),
        ],
        out_specs=[pl.BlockSpec((n_seq * l_t, D_CONV), lambda n, t: (n * grid[1] + t, 0)), hist_spec],
        out_shape=[jax.ShapeDtypeStruct((nb * l_seq, D_CONV), BF16),
                   jax.ShapeDtypeStruct((nb, CONF_HIST, D_CONV), F32)],
        scratch_shapes=[pltpu.VMEM((n_seq, CONF_PAD0 + l_t, D_CONV), F32),
                        pltpu.VMEM((SUBLANES - 1, n_seq, CONF_PAD0 + l_t, D_CONV), F32),
                        pltpu.VMEM((n_seq * l_t, D_CONV), F32),
                        pltpu.VMEM((CONF_KERNEL, SUBLANES, D_CONV), F32)],
        compiler_params=_cparams(2),
        name="conf",
    )(proj3, proj3, hist, p["conf_w"], p["conf_b"], p["conf_ln_g"], p["conf_ln_b"])


def _attn_kernel(q_ref, k_ref, v_ref, o_ref, *, n_seq):
    scale = MEM_HEAD_DIM ** -0.5
    few_rows = q_ref.shape[1] < LANES
    for h in range(MEM_HEADS):
        cols = slice(h * MEM_HEAD_DIM, (h + 1) * MEM_HEAD_DIM)
        if few_rows:
            scores = [_dot_nt(k_ref[s, :, cols], q_ref[s, :, cols]) * scale for s in range(n_seq)]
            axis = 0
        else:
            scores = [_dot_nt(q_ref[s, :, cols], k_ref[s, :, cols]) * scale for s in range(n_seq)]
            axis = 1
        probs = []
        for sc in scores:
            e = jnp.exp(sc - jnp.max(sc, axis=axis, keepdims=True))
            probs.append((e / jnp.sum(e, axis=axis, keepdims=True)).astype(BF16))
        if few_rows:
            outs = [lax.dot_general(pr, v_ref[s, :, cols].astype(BF16), (((0,), (0,)), ((), ())),
                                    preferred_element_type=F32) for s, pr in enumerate(probs)]
        else:
            outs = [_dot(pr, v_ref[s, :, cols]) for s, pr in enumerate(probs)]
        o_ref[:, cols] = jnp.concatenate(outs, axis=0).astype(BF16)


def _attn(proj3, mem_k, mem_v, *, n_seq, l_t):
    nb, l_seq, _ = proj3.shape
    grid = (nb // n_seq, l_seq // l_t)
    kv_spec = pl.BlockSpec((n_seq, MEM_LEN, ATT_W), lambda n, t: (n, 0, 0))
    return pl.pallas_call(
        functools.partial(_attn_kernel, n_seq=n_seq),
        grid=grid,
        in_specs=[pl.BlockSpec((n_seq, l_t, ATT_W), lambda n, t: (n, t, P_Q // ATT_W)), kv_spec, kv_spec],
        out_specs=pl.BlockSpec((n_seq * l_t, ATT_W), lambda n, t: (n * grid[1] + t, 0)),
        out_shape=jax.ShapeDtypeStruct((nb * l_seq, ATT_W), BF16),
        compiler_params=_cparams(2),
        name="attn",
    )(proj3, mem_k, mem_v)


KV_HALVES = MEM_HEAD_DIM // LANES
KV_ROWS = MEM_LEN * MEM_HEADS * KV_HALVES


def _attn_rows_kernel(q_ref, k_ref, v_ref, o_ref, *, n_seq, l_t):
    scale = MEM_HEAD_DIM ** -0.5
    nq = MEM_HEADS * l_t
    col = lax.broadcasted_iota(jnp.int32, (nq, KV_ROWS), 1)
    valid = jnp.bitwise_and(col, MEM_HEADS * KV_HALVES - 1) == _iota_div((nq, KV_ROWS), 0, l_t)
    parts = []
    for s in range(n_seq):
        q2 = jnp.concatenate([q_ref[s, :, (h * KV_HALVES + c) * LANES:(h * KV_HALVES + c + 1) * LANES]
                              for c in range(KV_HALVES) for h in range(MEM_HEADS)], axis=0)
        parts.append(_dot_nt(q2, k_ref[s]))
    probs = []
    for part in parts:
        sc = (part[0:nq] + pltpu.roll(part[nq:2 * nq], KV_ROWS - MEM_HEADS, axis=1)) * scale
        sc = jnp.where(valid, sc, -jnp.inf)
        e = jnp.exp(sc - jnp.max(sc, axis=1, keepdims=True))
        pr = e / jnp.sum(e, axis=1, keepdims=True)
        probs.append(jnp.concatenate([pr, pltpu.roll(pr, MEM_HEADS, axis=1)], axis=0).astype(BF16))
    outs = [_dot(p2, v_ref[s]) for s, p2 in enumerate(probs)]
    for c in range(KV_HALVES):
        for h in range(MEM_HEADS):
            r0 = (c * MEM_HEADS + h) * l_t
            piece = jnp.concatenate([o[r0:r0 + l_t] for o in outs], axis=0)
            o_ref[:, (h * KV_HALVES + c) * LANES:(h * KV_HALVES + c + 1) * LANES] = piece.astype(BF16)


def _attn_rows(proj3, k_rows, v_rows, *, n_seq):
    nb, l_t, _ = proj3.shape
    assert l_t % SUBLANES == 0 and KV_HALVES == 2
    kv_spec = pl.BlockSpec((n_seq, KV_ROWS, LANES), lambda n: (n, 0, 0))
    return pl.pallas_call(
        functools.partial(_attn_rows_kernel, n_seq=n_seq, l_t=l_t),
        grid=(nb // n_seq,),
        in_specs=[pl.BlockSpec((n_seq, l_t, ATT_W), lambda n: (n, 0, P_Q // ATT_W)), kv_spec, kv_spec],
        out_specs=pl.BlockSpec((n_seq * l_t, ATT_W), lambda n: (n, 0)),
        out_shape=jax.ShapeDtypeStruct((nb * l_t, ATT_W), BF16),
        compiler_params=_cparams(1),
        name="attn_rows",
    )(proj3, k_rows, v_rows)


def _interleave_kv(kv):
    b = kv.shape[0]
    kv = kv.reshape(b, MEM_LEN, MEM_HEADS, KV_HALVES, LANES)
    return jnp.transpose(kv, (0, 1, 3, 2, 4)).reshape(b, KV_ROWS, LANES)


MERGE_TM = 256


def _merge_kernel(x_ref, y_ref, c_ref, o_ref, g0_ref, g1_ref, g2_ref, bg_ref, wssd_ref, wconf_ref, wmem_ref,
                  wout_ref, lg_ref, lb_ref, l1g_ref, l1b_ref, h1_ref):
    merged = _sigmoid(g0_ref[...] + bg_ref[0:1, :]) * _dot(y_ref[...], wssd_ref[...])
    merged = merged + _sigmoid(g1_ref[...] + bg_ref[1:2, :]) * _dot(c_ref[...], wconf_ref[...])
    merged = merged + _sigmoid(g2_ref[...] + bg_ref[2:3, :]) * _dot(o_ref[...], wmem_ref[...])
    mix = _dot(merged, wout_ref[...])
    h = _layernorm(x_ref[...], lg_ref[...], lb_ref[...])
    h1_ref[...] = _layernorm(ALPHA * h + mix, l1g_ref[...], l1b_ref[...])


def _merge(x2d, y2d, c2d, o2d, proj2d, p):
    m = x2d.shape[0]
    tm = MERGE_TM
    rows = lambda w, blk: pl.BlockSpec((tm, w), lambda i: (i, blk))
    whole = lambda a: pl.BlockSpec(a.shape, lambda i: (0, 0), pipeline_mode=pl.Buffered(1))
    vec = pl.BlockSpec((1, D_MODEL), lambda i: (0, 0))
    g_blk = P_GATE // D_MODEL
    return pl.pallas_call(
        _merge_kernel,
        grid=(m // tm,),
        in_specs=[
            rows(D_MODEL, 0), rows(D_MODEL, 0), rows(D_CONV, 0), rows(ATT_W, 0),
            rows(D_MODEL, g_blk), rows(D_MODEL, g_blk + 1), rows(D_MODEL, g_blk + 2),
            pl.BlockSpec((3, D_MODEL), lambda i: (0, 0)),
            whole(p["w_br_ssd"]), whole(p["w_br_conf"]), whole(p["w_br_mem"]), whole(p["w_out"]),
            vec, vec, vec, vec,
        ],
        out_specs=rows(D_MODEL, 0),
        out_shape=jax.ShapeDtypeStruct((m, D_MODEL), F32),
        compiler_params=_cparams(1),
        name="merge",
    )(x2d, y2d, c2d, o2d, proj2d, proj2d, proj2d, p["b_gate"],
      p["w_br_ssd"], p["w_br_conf"], p["w_br_mem"], p["w_out"],
      p["ln_in_g"], p["ln_in_b"], p["ln1_g"], p["ln1_b"])


MLP_TM = 512
MLP_TF = 1024


def _mlp_kernel(h_ref, w1_ref, b1_ref, w2_ref, b2_ref, g_ref, b_ref, out_ref, hb_ref):
    f = pl.program_id(1)

    @pl.when(f == 0)
    def _():
        hb_ref[...] = h_ref[...].astype(BF16)
        out_ref[...] = jnp.zeros_like(out_ref)

    a = jnp.maximum(jnp.dot(hb_ref[...], w1_ref[...], preferred_element_type=F32) + b1_ref[...], 0.0)
    out_ref[...] += _dot(a * a, w2_ref[...])

    @pl.when(f == pl.num_programs(1) - 1)
    def _():
        out_ref[...] = _layernorm(ALPHA * h_ref[...] + (out_ref[...] + b2_ref[...]), g_ref[...], b_ref[...])


def _mlp(h2d, p):
    m = h2d.shape[0]
    tm, tf = MLP_TM, MLP_TF
    vec = pl.BlockSpec((1, D_MODEL), lambda i, f: (0, 0))
    return pl.pallas_call(
        _mlp_kernel,
        grid=(m // tm, D_FF // tf),
        in_specs=[
            pl.BlockSpec((tm, D_MODEL), lambda i, f: (i, 0)),
            pl.BlockSpec((D_MODEL, tf), lambda i, f: (0, f)),
            pl.BlockSpec((1, tf), lambda i, f: (0, f)),
            pl.BlockSpec((tf, D_MODEL), lambda i, f: (f, 0)),
            vec, vec, vec,
        ],
        out_specs=pl.BlockSpec((tm, D_MODEL), lambda i, f: (i, 0)),
        out_shape=jax.ShapeDtypeStruct((m, D_MODEL), F32),
        scratch_shapes=[pltpu.VMEM((tm, D_MODEL), BF16)],
        compiler_params=_cparams(2),
        name="mlp",
    )(h2d, p["w_ff1"], p["b_ff1"], p["w_ff2"], p["b_ff2"], p["ln2_g"], p["ln2_b"])


def _trunk(x, ssd_hist, ssm0, conf_hist, mem_k, mem_v, p, *, ssd_cfg, conf_cfg, attn_fn):
    nb, l_seq, _ = x.shape
    x2d = x.reshape(nb * l_seq, D_MODEL)
    proj2d, dt2d = _proj(x2d, p["ln_in_g"], p["ln_in_b"], p["w_in_t"])
    proj3 = proj2d.reshape(nb, l_seq, P_W)
    dt3 = dt2d.reshape(nb, l_seq, LANES)
    y, hx, hb, hc, ssm1 = _ssd(proj3, dt3, ssd_hist, ssm0, p, **ssd_cfg)
    c, conf_hist1 = _conf(proj3, conf_hist, p, **conf_cfg)
    o = attn_fn(proj3, mem_k, mem_v)
    h1 = _merge(x2d, y, c, o, proj2d, p)
    out = _mlp(h1, p).reshape(nb, l_seq, D_MODEL)
    return out, ssm1, jnp.concatenate([hx, hb, hc], axis=-1), conf_hist1


def kernel(x_prompt, x_sample, mem_prompt, state_ssm, state_ssd_conv, state_conf_conv, cache_mem_k, cache_mem_v, ln_in_g, ln_in_b, w_in, b_gate, ssd_conv_w, ssd_conv_b, ssd_dt_bias, ssd_a_log, ssd_d, ssd_norm_g, conf_dw_w, conf_dw_b, conf_ln_g, conf_ln_b, w_mem_k, w_mem_v, w_br_ssd, w_br_conf, w_br_mem, w_out, ln1_g, ln1_b, w_ff1, b_ff1, w_ff2, b_ff2, ln2_g, ln2_b):
    layer = 0
    row = lambda v: v.reshape(1, -1)
    lane_pad = lambda v: jnp.pad(v.reshape(1, -1), ((0, 0), (0, LANES - v.size)))
    p = {
        "ln_in_g": row(ln_in_g), "ln_in_b": row(ln_in_b),
        "w_in_t": jnp.swapaxes(w_in[layer], 0, 1),
        "b_gate": b_gate[layer],
        "conv_w": ssd_conv_w[layer], "conv_b": row(ssd_conv_b[layer]),
        "dt_bias": lane_pad(ssd_dt_bias[layer]), "a_log": lane_pad(ssd_a_log[layer]),
        "d_skip": row(jnp.repeat(ssd_d[layer], SSD_HEADDIM)), "norm_g": row(ssd_norm_g[layer]),
        "conf_w": conf_dw_w[layer], "conf_b": row(conf_dw_b[layer]),
        "conf_ln_g": row(conf_ln_g[layer]), "conf_ln_b": row(conf_ln_b[layer]),
        "w_br_ssd": w_br_ssd[layer].astype(BF16), "w_br_conf": w_br_conf[layer].astype(BF16),
        "w_br_mem": w_br_mem[layer].astype(BF16), "w_out": w_out[layer].astype(BF16),
        "ln1_g": row(ln1_g[layer]), "ln1_b": row(ln1_b[layer]),
        "w_ff1": w_ff1[layer].astype(BF16), "b_ff1": row(b_ff1[layer]),
        "w_ff2": w_ff2[layer].astype(BF16), "b_ff2": row(b_ff2[layer]),
        "ln2_g": row(ln2_g[layer]), "ln2_b": row(ln2_b[layer]),
    }
    n_p, l_p, _ = x_prompt.shape
    n_s, l_s, _ = x_sample.shape

    mem2d = mem_prompt.reshape(n_p * MEM_LEN, D_MODEL)
    p_mem_k = _matmul(mem2d, w_mem_k[layer]).reshape(n_p, MEM_LEN, ATT_W)
    p_mem_v = _matmul(mem2d, w_mem_v[layer]).reshape(n_p, MEM_LEN, ATT_W)
    y_prompt, p_ssm, p_ssd_conv, p_cc = _trunk(
        x_prompt,
        jnp.zeros((n_p, SSD_CONV - 1, CONV_DIM), F32),
        jnp.zeros((n_p, SSD_HEADS, SSD_HEADDIM, SSD_STATE), F32),
        jnp.zeros((n_p, CONF_HIST, D_CONV), F32),
        p_mem_k, p_mem_v, p,
        ssd_cfg=dict(n_seq=1, l_q=SSD_ROWS, g_step=SSD_GROUPS),
        conf_cfg=dict(n_seq=1, l_t=512),
        attn_fn=functools.partial(_attn, n_seq=1, l_t=512))

    y_sample, s_ssm, s_ssd_conv, s_cc = _trunk(
        x_sample, state_ssd_conv[layer], state_ssm[layer], state_conf_conv[layer],
        _interleave_kv(cache_mem_k[layer]), _interleave_kv(cache_mem_v[layer]), p,
        ssd_cfg=dict(n_seq=SSD_ROWS // l_s, l_q=l_s, g_step=2),
        conf_cfg=dict(n_seq=8, l_t=l_s),
        attn_fn=functools.partial(_attn_rows, n_seq=8))

    kv_shape = (DEPTH, n_p, MEM_LEN, MEM_HEADS, MEM_HEAD_DIM)
    return (y_prompt, y_sample, p_ssm[None], p_ssd_conv[None], p_cc[None],
            p_mem_k.reshape(kv_shape), p_mem_v.reshape(kv_shape),
            s_ssm[None], s_ssd_conv[None], s_cc[None])
```

```python
import functools
import math

import jax
import jax.numpy as jnp
from jax import lax
from jax.experimental import pallas as pl
from jax.experimental.pallas import tpu as pltpu

F32 = jnp.float32
BF16 = jnp.bfloat16

D_MODEL = 2048
SSD_HEADDIM = 64
SSD_HEADS = 32
SSD_GROUPS = 8
SSD_HG = 4
SSD_STATE = 128
SSD_CONV = 4
CONV_DIM = 4096
D_CONV = 1024
CONF_KERNEL = 31
MEM_LEN = 256
MEM_HEADS = 4
MEM_HEAD_DIM = 256
ATT_W = 1024
D_FF = 8192
DEPTH = 1
ALPHA = (2.0 * DEPTH) ** 0.25
EPS = 1e-5

O_XBC = 2048
O_DT = 6144
O_GLU = 6176
P_GATE = 6144
P_GLU = 12288
P_Q = 14336
P_W = 15360

GROUP_W = SSD_HG * SSD_HEADDIM
SSD_ROWS = 128
LANES = 128
BF16_ROWS = 16
VMEM_LIMIT = 56 * 1024 * 1024


def _cparams(n_axes):
    return pltpu.CompilerParams(dimension_semantics=("arbitrary",) * n_axes, vmem_limit_bytes=VMEM_LIMIT)


def _layernorm(x, g, b):
    mu = jnp.mean(x, axis=-1, keepdims=True)
    xc = x - mu
    var = jnp.mean(xc * xc, axis=-1, keepdims=True)
    return xc * lax.rsqrt(var + EPS) * g + b


def _sigmoid(x):
    return 0.5 * jnp.tanh(0.5 * x) + 0.5


def _silu(x):
    return x * _sigmoid(x)


def _dot(a, b):
    return jnp.dot(a.astype(BF16), b.astype(BF16), preferred_element_type=F32)


def _dot_nt(a, b):
    return lax.dot_general(a.astype(BF16), b.astype(BF16), (((1,), (1,)), ((), ())), preferred_element_type=F32)


def _iota_div(shape, axis, divisor):
    shift = divisor.bit_length() - 1
    assert 1 << shift == divisor
    return lax.shift_right_logical(lax.broadcasted_iota(jnp.int32, shape, axis), shift)


PROJ_TM = 1024
PROJ_TN = 1024
PROJ_LN_ROWS = 128
PROJ_NA = O_DT // PROJ_TN
PROJ_NB = (P_W - O_DT) // PROJ_TN


def _proj_kernel(x_ref, g_ref, b_ref, wt_ref, wdt_ref, out_ref, dt_ref, xn_ref):
    @pl.when(pl.program_id(1) == 0)
    def _():
        def ln_rows(r, carry):
            r0 = pl.multiple_of(r * PROJ_LN_ROWS, PROJ_LN_ROWS)
            xn = _layernorm(x_ref[pl.ds(r0, PROJ_LN_ROWS), :], g_ref[...], b_ref[...])
            xn_ref[pl.ds(r0, PROJ_LN_ROWS), :] = xn.astype(BF16)
            return carry
        lax.fori_loop(0, x_ref.shape[0] // PROJ_LN_ROWS, ln_rows, 0)
        dt_ref[...] = _dot_nt(xn_ref[...], wdt_ref[...])

    out_ref[...] = _dot_nt(xn_ref[...], wt_ref[...])


def _proj_w_row(j):
    t = BF16_ROWS
    assert O_GLU % t == 0 and PROJ_TN % t == 0
    return t * jnp.where(j < PROJ_NA, j * (PROJ_TN // t), O_GLU // t + (j - PROJ_NA) * (PROJ_TN // t))


def _proj_out_tile(j):
    jb = j - PROJ_NA
    n_glu_q = (P_W - P_GLU) // PROJ_TN
    tile_b = jnp.where(jb < n_glu_q, P_GLU // PROJ_TN + jb, P_GATE // PROJ_TN + jb - n_glu_q)
    return jnp.where(j < PROJ_NA, j, tile_b)


def _proj(x2d, ln_g, ln_b, w_in_t):
    m = x2d.shape[0]
    tm = min(PROJ_TM, m)
    grid = (m // tm, PROJ_NA + PROJ_NB)
    return pl.pallas_call(
        _proj_kernel,
        grid=grid,
        in_specs=[
            pl.BlockSpec((tm, D_MODEL), lambda i, j: (i, 0), pipeline_mode=pl.Buffered(1)),
            pl.BlockSpec((1, D_MODEL), lambda i, j: (0, 0)),
            pl.BlockSpec((1, D_MODEL), lambda i, j: (0, 0)),
            pl.BlockSpec((pl.Element(PROJ_TN), pl.Element(D_MODEL)), lambda i, j: (_proj_w_row(j), 0)),
            pl.BlockSpec((LANES, D_MODEL), lambda i, j: (O_DT // LANES, 0)),
        ],
        out_specs=[
            pl.BlockSpec((tm, PROJ_TN), lambda i, j: (i, _proj_out_tile(j))),
            pl.BlockSpec((tm, LANES), lambda i, j: (i, 0)),
        ],
        out_shape=[jax.ShapeDtypeStruct((m, P_W), F32), jax.ShapeDtypeStruct((m, LANES), F32)],
        scratch_shapes=[pltpu.VMEM((tm, D_MODEL), BF16)],
        compiler_params=_cparams(2),
        name="proj",
    )(x2d, ln_g, ln_b, w_in_t, w_in_t)


def _matmul_kernel(x_ref, w_ref, out_ref):
    out_ref[...] = _dot(x_ref[...], w_ref[...])


def _matmul(x2d, w, tn=512):
    m, k = x2d.shape
    n = w.shape[1]
    return pl.pallas_call(
        _matmul_kernel,
        grid=(n // tn,),
        in_specs=[pl.BlockSpec((m, k), lambda j: (0, 0)), pl.BlockSpec((k, tn), lambda j: (0, j))],
        out_specs=pl.BlockSpec((m, tn), lambda j: (0, j)),
        out_shape=jax.ShapeDtypeStruct((m, n), F32),
        compiler_params=_cparams(1),
        name="memkv",
    )(x2d, w)


def _split3(v):
    hi = v.astype(BF16)
    r1 = v - hi.astype(F32)
    mid = r1.astype(BF16)
    lo = (r1 - mid.astype(F32)).astype(BF16)
    return jnp.concatenate([hi, mid, lo], axis=1)


def _expand_heads(pieces, onehot3):
    return jnp.dot(pieces, onehot3, preferred_element_type=F32)


def _conv4(pad_ref, u, hist_ref, hist_out_ref, w_ref, b_ref, first, l_q):
    @pl.when(first)
    def _():
        pad_ref[:, 5:8, :] = hist_ref[...]

    @pl.when(jnp.logical_not(first))
    def _():
        pad_ref[:, 5:8, :] = pad_ref[:, 5 + l_q:8 + l_q, :]

    pad_ref[:, 8:8 + l_q, :] = u
    full = pad_ref[...]
    acc = b_ref[...] + w_ref[SSD_CONV - 1:SSD_CONV, :] * u
    for d in range(1, SSD_CONV):
        shifted = pltpu.roll(full, d, axis=1)[:, 8:8 + l_q, :]
        acc = acc + w_ref[SSD_CONV - 1 - d:SSD_CONV - d, :] * shifted
    hist_out_ref[...] = pad_ref[:, 5 + l_q:8 + l_q, :]
    return _silu(acc)


def _ssd_kernel(z_ref, x_ref, bm_ref, cm_ref, dt_ref, hx_ref, hb_ref, hc_ref, h0_ref,
                wx_ref, wb_ref, wc_ref, bx_ref, bb_ref, bc_ref, dtb_ref, alog_ref, dsk_ref, ng_ref, oh_ref,
                y_ref, ox_ref, ob_ref, oc_ref, h_ref,
                px_ref, pb_ref, pc_ref, yoff_ref, *, n_seq, l_q, g_step, single_chunk):
    rows = n_seq * l_q
    gs = pl.program_id(1)
    first = pl.program_id(2) == 0

    xs = _conv4(px_ref, x_ref[...], hx_ref, ox_ref, wx_ref, bx_ref, first, l_q).reshape(rows, g_step * GROUP_W)
    bm = _conv4(pb_ref, bm_ref[...], hb_ref, ob_ref, wb_ref, bb_ref, first, l_q).reshape(rows, g_step * SSD_STATE)
    cm = _conv4(pc_ref, cm_ref[...], hc_ref, oc_ref, wc_ref, bc_ref, first, l_q).reshape(rows, g_step * SSD_STATE)

    if single_chunk:
        h_in_ref = h0_ref
    else:
        h_in_ref = h_ref

        @pl.when(first)
        def _():
            h_ref[...] = h0_ref[...]

    lane = lax.broadcasted_iota(jnp.int32, (1, LANES), 1)
    dt_raw = dt_ref[...].reshape(rows, LANES) + dtb_ref[...]
    dt = jnp.maximum(dt_raw, 0.0) + jnp.log1p(jnp.exp(-jnp.abs(dt_raw)))
    dt = jnp.where(lane < SSD_HEADS, dt, 0.0)
    a = dt * (-jnp.exp(alog_ref[...]))
    if g_step != SSD_GROUPS:
        src = jnp.bitwise_and(lax.broadcasted_iota(jnp.int32, (3 * LANES, LANES), 0), LANES - 1)
        dst = lax.broadcasted_iota(jnp.int32, (3 * LANES, LANES), 1)
        pick = (src == dst + gs * (g_step * SSD_HG)).astype(BF16)
        dt = _expand_heads(_split3(dt), pick)
        a = _expand_heads(_split3(a), pick)
    rq = lax.broadcasted_iota(jnp.int32, (rows, rows), 0)
    ck = lax.broadcasted_iota(jnp.int32, (rows, rows), 1)
    same = _iota_div((rows, rows), 0, l_q) == _iota_div((rows, rows), 1, l_q)
    causal = jnp.logical_and(same, ck <= rq)
    sums = jnp.dot(jnp.concatenate([causal, same], axis=0).astype(BF16), _split3(a), preferred_element_type=F32)
    sums = sums[:, 0:LANES] + sums[:, LANES:2 * LANES] + sums[:, 2 * LANES:3 * LANES]
    acs = sums[0:rows]
    tot = sums[rows:2 * rows]
    acs_t = acs.T
    seq_of_col = _iota_div((1, rows), 1, l_q)

    dt_p = _split3(dt)
    to_end_p = _split3(jnp.exp(tot - acs))
    from_start_p = _split3(jnp.exp(acs))
    head_of_col = _iota_div((1, GROUP_W), 1, SSD_HEADDIM)
    groups = range(g_step)
    cols_x = [slice(gi * GROUP_W, (gi + 1) * GROUP_W) for gi in groups]
    cols_n = [slice(gi * SSD_STATE, (gi + 1) * SSD_STATE) for gi in groups]

    dt_x = [_expand_heads(dt_p, oh_ref[:, c]) for c in cols_x]
    to_end_x = [_expand_heads(to_end_p, oh_ref[:, c]) for c in cols_x]
    from_start_x = [_expand_heads(from_start_p, oh_ref[:, c]) for c in cols_x]
    cbm = [jnp.where(causal, _dot_nt(cm[:, c], bm[:, c]), 0.0) for c in cols_n]

    m_cat, x_cat, xd_t = [], [], []
    for gi in groups:
        lane0 = gi * SSD_HG
        xdt = xs[:, cols_x[gi]] * dt_x[gi]
        xdt_b = xdt.astype(BF16)
        m_heads, x_heads = [], []
        for h in range(SSD_HG):
            diff = acs[:, lane0 + h:lane0 + h + 1] - acs_t[lane0 + h:lane0 + h + 1, :]
            m_heads.append((cbm[gi] * jnp.exp(jnp.where(causal, diff, 0.0))).astype(BF16))
            x_heads.append(jnp.where(head_of_col == h, xdt_b, jnp.zeros_like(xdt_b)))
        m_cat.append(jnp.concatenate(m_heads, axis=1))
        x_cat.append(jnp.concatenate(x_heads, axis=0))
        xd_t.append((xdt * to_end_x[gi]).T)

    y_in = [jnp.dot(m_cat[gi], x_cat[gi], preferred_element_type=F32) for gi in groups]

    for gi in groups:
        lane0 = gi * SSD_HG
        bg = bm[:, cols_n[gi]]
        cg = cm[:, cols_n[gi]]
        for s in range(n_seq):
            h_old = h_in_ref[s, lane0:lane0 + SSD_HG].reshape(GROUP_W, SSD_STATE)
            yoff_ref[s * l_q:(s + 1) * l_q, cols_x[gi]] = _dot_nt(cg[s * l_q:(s + 1) * l_q, :], h_old)
            xd_s = xd_t[gi] if n_seq == 1 else jnp.where(seq_of_col == s, xd_t[gi], 0.0)
            upd = _dot(xd_s, bg)
            dec = jnp.concatenate(
                [jnp.broadcast_to(jnp.exp(tot[s * l_q:s * l_q + 1, lane0 + h:lane0 + h + 1]),
                                  (SSD_HEADDIM, SSD_STATE)) for h in range(SSD_HG)], axis=0)
            h_ref[s, lane0:lane0 + SSD_HG] = (h_old * dec + upd).reshape(SSD_HG, SSD_HEADDIM, SSD_STATE)

    for gi in groups:
        c = cols_x[gi]
        y = y_in[gi] + yoff_ref[:, c] * from_start_x[gi] + dsk_ref[:, c] * xs[:, c]
        v = y * _silu(z_ref[:, :, c].reshape(rows, GROUP_W))
        v = v * lax.rsqrt(jnp.mean(v * v, axis=-1, keepdims=True) + EPS)
        y_ref[:, c] = (v * ng_ref[:, c]).astype(BF16)


def _head_onehot():
    lane = jnp.arange(3 * LANES)[:, None] % LANES
    head = jnp.arange(D_MODEL)[None, :] // SSD_HEADDIM
    return (lane == head).astype(BF16)


def _ssd(proj3, dt3, hist, h0, p, *, n_seq, l_q, g_step):
    nb, l_seq, _ = proj3.shape
    wx = g_step * GROUP_W
    wn = g_step * SSD_STATE
    grid = (nb // n_seq, SSD_GROUPS // g_step, l_seq // l_q)
    ob = (O_XBC + D_MODEL) // wn
    oc = ob + (SSD_GROUPS * SSD_STATE) // wn
    cb = D_MODEL // wn
    cc = cb + (SSD_GROUPS * SSD_STATE) // wn

    def rows_spec(width, first_block):
        return pl.BlockSpec((n_seq, l_q, width), lambda n, g, c: (n, c, first_block + g))

    def hist_spec(width, first_block):
        return pl.BlockSpec((n_seq, SSD_CONV - 1, width), lambda n, g, c: (n, 0, first_block + g))

    def par_spec(r, width, first_block):
        return pl.BlockSpec((r, width), lambda n, g, c: (0, first_block + g))

    state_spec = pl.BlockSpec((n_seq, g_step * SSD_HG, SSD_HEADDIM, SSD_STATE), lambda n, g, c: (n, g, 0, 0))
    full_lane = pl.BlockSpec((1, LANES), lambda n, g, c: (0, 0))
    kern = functools.partial(_ssd_kernel, n_seq=n_seq, l_q=l_q, g_step=g_step, single_chunk=grid[2] == 1)
    return pl.pallas_call(
        kern,
        grid=grid,
        in_specs=[
            rows_spec(wx, 0), rows_spec(wx, O_XBC // wx), rows_spec(wn, ob), rows_spec(wn, oc),
            pl.BlockSpec((n_seq, l_q, LANES), lambda n, g, c: (n, c, 0)),
            hist_spec(wx, 0), hist_spec(wn, cb), hist_spec(wn, cc),
            state_spec,
            par_spec(SSD_CONV, wx, 0), par_spec(SSD_CONV, wn, cb), par_spec(SSD_CONV, wn, cc),
            par_spec(1, wx, 0), par_spec(1, wn, cb), par_spec(1, wn, cc),
            full_lane, full_lane, par_spec(1, wx, 0), par_spec(1, wx, 0),
            pl.BlockSpec((3 * LANES, wx), lambda n, g, c: (0, 0)),
        ],
        out_specs=[
            pl.BlockSpec((SSD_ROWS, wx), lambda n, g, c: (n * grid[2] + c, g)),
            hist_spec(wx, 0), hist_spec(wn, 0), hist_spec(wn, 0),
            state_spec,
        ],
        out_shape=[
            jax.ShapeDtypeStruct((nb * l_seq, D_MODEL), BF16),
            jax.ShapeDtypeStruct((nb, SSD_CONV - 1, D_MODEL), F32),
            jax.ShapeDtypeStruct((nb, SSD_CONV - 1, SSD_GROUPS * SSD_STATE), F32),
            jax.ShapeDtypeStruct((nb, SSD_CONV - 1, SSD_GROUPS * SSD_STATE), F32),
            jax.ShapeDtypeStruct((nb, SSD_HEADS, SSD_HEADDIM, SSD_STATE), F32),
        ],
        scratch_shapes=[
            pltpu.VMEM((n_seq, 8 + l_q, wx), F32),
            pltpu.VMEM((n_seq, 8 + l_q, wn), F32),
            pltpu.VMEM((n_seq, 8 + l_q, wn), F32),
            pltpu.VMEM((SSD_ROWS, wx), F32),
        ],
        compiler_params=_cparams(3),
        name="ssd",
    )(proj3, proj3, proj3, proj3, dt3, hist, hist, hist, h0,
      p["conv_w"], p["conv_w"], p["conv_w"], p["conv_b"], p["conv_b"], p["conv_b"],
      p["dt_bias"], p["a_log"], p["d_skip"], p["norm_g"], p["head_onehot"])


CONF_HIST = CONF_KERNEL - 1
CONF_PAD0 = 32


SUBLANES = 8
CONF_NORM_ROWS = 128


def _conf_kernel(a_ref, b_ref, hist_ref, w_ref, bias_ref, g_ref, beta_ref, c_ref, hist_out_ref,
                 pad_ref, ph_ref, cf_ref, wrep_ref, *, n_seq, l_t, rb):
    first = pl.program_id(1) == 0
    lo = CONF_PAD0 - CONF_HIST
    rows = CONF_PAD0 + l_t

    @pl.when(first)
    def _():
        pad_ref[:, lo:CONF_PAD0, :] = hist_ref[...]

    @pl.when(jnp.logical_not(first))
    def _():
        pad_ref[:, lo:CONF_PAD0, :] = pad_ref[:, lo + l_t:CONF_PAD0 + l_t, :]

    pad_ref[:, CONF_PAD0:CONF_PAD0 + l_t, :] = a_ref[...] * _sigmoid(b_ref[...])
    hist_out_ref[...] = pad_ref[:, lo + l_t:CONF_PAD0 + l_t, :]

    for p in range(1, SUBLANES):
        ph_ref[p - 1, :, 0:rows - SUBLANES, :] = pad_ref[:, p:p + rows - SUBLANES, :]

    for j in range(CONF_KERNEL):
        wrep_ref[j] = jnp.broadcast_to(w_ref[j:j + 1, :], (SUBLANES, D_CONV))

    n_rb = l_t // rb

    def conv_block(i, carry):
        s = i // n_rb
        r0 = (i % n_rb) * rb
        n_tiles = rb // SUBLANES
        out_row = pl.multiple_of(s * l_t + r0, SUBLANES)
        for c0 in range(0, D_CONV, LANES):
            lanes = slice(c0, c0 + LANES)
            acc = [jnp.broadcast_to(bias_ref[:, lanes], (SUBLANES, LANES))] * n_tiles
            for j in range(CONF_KERNEL):
                p = (lo + j) % SUBLANES
                src = pad_ref if p == 0 else ph_ref.at[p - 1]
                wj = wrep_ref[j, :, lanes]
                for k in range(n_tiles):
                    start = pl.multiple_of(r0 + (lo + j - p) + k * SUBLANES, SUBLANES)
                    acc[k] = acc[k] + wj * src[s, pl.ds(start, SUBLANES), lanes]
            cf_ref[pl.ds(out_row, rb), lanes] = jnp.concatenate(acc, axis=0)
        return carry

    lax.fori_loop(0, n_seq * n_rb, conv_block, 0)

    nb_rows = min(CONF_NORM_ROWS, n_seq * l_t)

    def norm_block(i, carry):
        r0 = pl.multiple_of(i * nb_rows, nb_rows)
        v = _silu(_layernorm(cf_ref[pl.ds(r0, nb_rows), :], g_ref[...], beta_ref[...]))
        c_ref[pl.ds(r0, nb_rows), :] = v.astype(BF16)
        return carry

    lax.fori_loop(0, (n_seq * l_t) // nb_rows, norm_block, 0)


def _conf(proj3, hist, p, *, n_seq, l_t):
    nb, l_seq, _ = proj3.shape
    rb = min(32, l_t)
    grid = (nb // n_seq, l_seq // l_t)
    par = lambda r: pl.BlockSpec((r, D_CONV), lambda n, t: (0, 0))
    hist_spec = pl.BlockSpec((n_seq, CONF_HIST, D_CONV), lambda n, t: (n, 0, 0))
    kern = functools.partial(_conf_kernel, n_seq=n_seq, l_t=l_t, rb=rb)
    return pl.pallas_call(
        kern,
        grid=grid,
        in_specs=[
            pl.BlockSpec((n_seq, l_t, D_CONV), lambda n, t: (n, t, P_GLU // D_CONV)),
            pl.BlockSpec((n_seq, l_t, D_CONV), lambda n, t: (n, t, P_GLU // D_CONV + 1)),
            hist_spec, par(CONF_KERNEL), par(1), par(1), par(1),
        ],
        out_specs=[pl.BlockSpec((n_seq * l_t, D_CONV), lambda n, t: (n * grid[1] + t, 0)), hist_spec],
        out_shape=[jax.ShapeDtypeStruct((nb * l_seq, D_CONV), BF16),
                   jax.ShapeDtypeStruct((nb, CONF_HIST, D_CONV), F32)],
        scratch_shapes=[pltpu.VMEM((n_seq, CONF_PAD0 + l_t, D_CONV), F32),
                        pltpu.VMEM((SUBLANES - 1, n_seq, CONF_PAD0 + l_t, D_CONV), F32),
                        pltpu.VMEM((n_seq * l_t, D_CONV), F32),
                        pltpu.VMEM((CONF_KERNEL, SUBLANES, D_CONV), F32)],
        compiler_params=_cparams(2),
        name="conf",
    )(proj3, proj3, hist, p["conf_w"], p["conf_b"], p["conf_ln_g"], p["conf_ln_b"])


def _attn_kernel(q_ref, k_ref, v_ref, o_ref, *, n_seq):
    scale = MEM_HEAD_DIM ** -0.5
    few_rows = q_ref.shape[1] < LANES
    for h in range(MEM_HEADS):
        cols = slice(h * MEM_HEAD_DIM, (h + 1) * MEM_HEAD_DIM)
        if few_rows:
            scores = [_dot_nt(k_ref[s, :, cols], q_ref[s, :, cols]) * scale for s in range(n_seq)]
            axis = 0
        else:
            scores = [_dot_nt(q_ref[s, :, cols], k_ref[s, :, cols]) * scale for s in range(n_seq)]
            axis = 1
        probs = []
        for sc in scores:
            e = jnp.exp(sc - jnp.max(sc, axis=axis, keepdims=True))
            probs.append((e / jnp.sum(e, axis=axis, keepdims=True)).astype(BF16))
        if few_rows:
            outs = [lax.dot_general(pr, v_ref[s, :, cols].astype(BF16), (((0,), (0,)), ((), ())),
                                    preferred_element_type=F32) for s, pr in enumerate(probs)]
        else:
            outs = [_dot(pr, v_ref[s, :, cols]) for s, pr in enumerate(probs)]
        o_ref[:, cols] = jnp.concatenate(outs, axis=0).astype(BF16)


def _attn(proj3, mem_k, mem_v, *, n_seq, l_t):
    nb, l_seq, _ = proj3.shape
    grid = (nb // n_seq, l_seq // l_t)
    kv_spec = pl.BlockSpec((n_seq, MEM_LEN, ATT_W), lambda n, t: (n, 0, 0))
    return pl.pallas_call(
        functools.partial(_attn_kernel, n_seq=n_seq),
        grid=grid,
        in_specs=[pl.BlockSpec((n_seq, l_t, ATT_W), lambda n, t: (n, t, P_Q // ATT_W)), kv_spec, kv_spec],
        out_specs=pl.BlockSpec((n_seq * l_t, ATT_W), lambda n, t: (n * grid[1] + t, 0)),
        out_shape=jax.ShapeDtypeStruct((nb * l_seq, ATT_W), BF16),
        compiler_params=_cparams(2),
        name="attn",
    )(proj3, mem_k, mem_v)


KV_HALVES = MEM_HEAD_DIM // LANES
KV_ROWS = MEM_LEN * MEM_HEADS * KV_HALVES


def _attn_rows_kernel(q_ref, k_ref, v_ref, o_ref, *, n_seq, l_t):
    scale = MEM_HEAD_DIM ** -0.5
    nq = MEM_HEADS * l_t
    col = lax.broadcasted_iota(jnp.int32, (nq, KV_ROWS), 1)
    valid = jnp.bitwise_and(col, MEM_HEADS * KV_HALVES - 1) == _iota_div((nq, KV_ROWS), 0, l_t)
    parts = []
    for s in range(n_seq):
        q2 = jnp.concatenate([q_ref[s, :, (h * KV_HALVES + c) * LANES:(h * KV_HALVES + c + 1) * LANES]
                              for c in range(KV_HALVES) for h in range(MEM_HEADS)], axis=0)
        parts.append(_dot_nt(q2, k_ref[s]))
    probs = []
    for part in parts:
        sc = (part[0:nq] + pltpu.roll(part[nq:2 * nq], KV_ROWS - MEM_HEADS, axis=1)) * scale
        sc = jnp.where(valid, sc, -jnp.inf)
        e = jnp.exp(sc - jnp.max(sc, axis=1, keepdims=True))
        pr = e / jnp.sum(e, axis=1, keepdims=True)
        probs.append(jnp.concatenate([pr, pltpu.roll(pr, MEM_HEADS, axis=1)], axis=0).astype(BF16))
    outs = [_dot(p2, v_ref[s]) for s, p2 in enumerate(probs)]
    for c in range(KV_HALVES):
        for h in range(MEM_HEADS):
            r0 = (c * MEM_HEADS + h) * l_t
            piece = jnp.concatenate([o[r0:r0 + l_t] for o in outs], axis=0)
            o_ref[:, (h * KV_HALVES + c) * LANES:(h * KV_HALVES + c + 1) * LANES] = piece.astype(BF16)


def _attn_rows(proj3, k_rows, v_rows, *, n_seq):
    nb, l_t, _ = proj3.shape
    assert l_t % SUBLANES == 0 and KV_HALVES == 2
    kv_spec = pl.BlockSpec((n_seq, KV_ROWS, LANES), lambda n: (n, 0, 0))
    return pl.pallas_call(
        functools.partial(_attn_rows_kernel, n_seq=n_seq, l_t=l_t),
        grid=(nb // n_seq,),
        in_specs=[pl.BlockSpec((n_seq, l_t, ATT_W), lambda n: (n, 0, P_Q // ATT_W)), kv_spec, kv_spec],
        out_specs=pl.BlockSpec((n_seq * l_t, ATT_W), lambda n: (n, 0)),
        out_shape=jax.ShapeDtypeStruct((nb * l_t, ATT_W), BF16),
        compiler_params=_cparams(1),
        name="attn_rows",
    )(proj3, k_rows, v_rows)


def _interleave_kv(kv):
    b = kv.shape[0]
    kv = kv.reshape(b, MEM_LEN, MEM_HEADS, KV_HALVES, LANES)
    return jnp.transpose(kv, (0, 1, 3, 2, 4)).reshape(b, KV_ROWS, LANES)


MERGE_TM = 256


def _merge_kernel(x_ref, y_ref, c_ref, o_ref, g0_ref, g1_ref, g2_ref, bg_ref, wssd_ref, wconf_ref, wmem_ref,
                  wout_ref, lg_ref, lb_ref, l1g_ref, l1b_ref, h1_ref):
    merged = _sigmoid(g0_ref[...] + bg_ref[0:1, :]) * _dot(y_ref[...], wssd_ref[...])
    merged = merged + _sigmoid(g1_ref[...] + bg_ref[1:2, :]) * _dot(c_ref[...], wconf_ref[...])
    merged = merged + _sigmoid(g2_ref[...] + bg_ref[2:3, :]) * _dot(o_ref[...], wmem_ref[...])
    mix = _dot(merged, wout_ref[...])
    h = _layernorm(x_ref[...], lg_ref[...], lb_ref[...])
    h1_ref[...] = _layernorm(ALPHA * h + mix, l1g_ref[...], l1b_ref[...])


def _merge(x2d, y2d, c2d, o2d, proj2d, p):
    m = x2d.shape[0]
    tm = MERGE_TM
    rows = lambda w, blk: pl.BlockSpec((tm, w), lambda i: (i, blk))
    whole = lambda a: pl.BlockSpec(a.shape, lambda i: (0, 0), pipeline_mode=pl.Buffered(1))
    vec = pl.BlockSpec((1, D_MODEL), lambda i: (0, 0))
    g_blk = P_GATE // D_MODEL
    return pl.pallas_call(
        _merge_kernel,
        grid=(m // tm,),
        in_specs=[
            rows(D_MODEL, 0), rows(D_MODEL, 0), rows(D_CONV, 0), rows(ATT_W, 0),
            rows(D_MODEL, g_blk), rows(D_MODEL, g_blk + 1), rows(D_MODEL, g_blk + 2),
            pl.BlockSpec((3, D_MODEL), lambda i: (0, 0)),
            whole(p["w_br_ssd"]), whole(p["w_br_conf"]), whole(p["w_br_mem"]), whole(p["w_out"]),
            vec, vec, vec, vec,
        ],
        out_specs=rows(D_MODEL, 0),
        out_shape=jax.ShapeDtypeStruct((m, D_MODEL), F32),
        compiler_params=_cparams(1),
        name="merge",
    )(x2d, y2d, c2d, o2d, proj2d, proj2d, proj2d, p["b_gate"],
      p["w_br_ssd"], p["w_br_conf"], p["w_br_mem"], p["w_out"],
      p["ln_in_g"], p["ln_in_b"], p["ln1_g"], p["ln1_b"])


MLP_TM = 512
MLP_TF = 1024


def _mlp_kernel(h_ref, w1_ref, b1_ref, w2_ref, b2_ref, g_ref, b_ref, out_ref, hb_ref):
    f = pl.program_id(1)

    @pl.when(f == 0)
    def _():
        hb_ref[...] = h_ref[...].astype(BF16)
        out_ref[...] = jnp.zeros_like(out_ref)

    a = jnp.maximum(jnp.dot(hb_ref[...], w1_ref[...], preferred_element_type=F32) + b1_ref[...], 0.0)
    out_ref[...] += _dot(a * a, w2_ref[...])

    @pl.when(f == pl.num_programs(1) - 1)
    def _():
        out_ref[...] = _layernorm(ALPHA * h_ref[...] + (out_ref[...] + b2_ref[...]), g_ref[...], b_ref[...])


def _mlp(h2d, p):
    m = h2d.shape[0]
    tm, tf = MLP_TM, MLP_TF
    vec = pl.BlockSpec((1, D_MODEL), lambda i, f: (0, 0))
    return pl.pallas_call(
        _mlp_kernel,
        grid=(m // tm, D_FF // tf),
        in_specs=[
            pl.BlockSpec((tm, D_MODEL), lambda i, f: (i, 0)),
            pl.BlockSpec((D_MODEL, tf), lambda i, f: (0, f)),
            pl.BlockSpec((1, tf), lambda i, f: (0, f)),
            pl.BlockSpec((tf, D_MODEL), lambda i, f: (f, 0)),
            vec, vec, vec,
        ],
        out_specs=pl.BlockSpec((tm, D_MODEL), lambda i, f: (i, 0)),
        out_shape=jax.ShapeDtypeStruct((m, D_MODEL), F32),
        scratch_shapes=[pltpu.VMEM((tm, D_MODEL), BF16)],
        compiler_params=_cparams(2),
        name="mlp",
    )(h2d, p["w_ff1"], p["b_ff1"], p["w_ff2"], p["b_ff2"], p["ln2_g"], p["ln2_b"])


def _trunk(x, ssd_hist, ssm0, conf_hist, mem_k, mem_v, p, *, ssd_cfg, conf_cfg, attn_fn):
    nb, l_seq, _ = x.shape
    x2d = x.reshape(nb * l_seq, D_MODEL)
    proj2d, dt2d = _proj(x2d, p["ln_in_g"], p["ln_in_b"], p["w_in_t"])
    proj3 = proj2d.reshape(nb, l_seq, P_W)
    dt3 = dt2d.reshape(nb, l_seq, LANES)
    y, hx, hb, hc, ssm1 = _ssd(proj3, dt3, ssd_hist, ssm0, p, **ssd_cfg)
    c, conf_hist1 = _conf(proj3, conf_hist, p, **conf_cfg)
    o = attn_fn(proj3, mem_k, mem_v)
    h1 = _merge(x2d, y, c, o, proj2d, p)
    out = _mlp(h1, p).reshape(nb, l_seq, D_MODEL)
    return out, ssm1, jnp.concatenate([hx, hb, hc], axis=-1), conf_hist1


def kernel(x_prompt, x_sample, mem_prompt, state_ssm, state_ssd_conv, state_conf_conv, cache_mem_k, cache_mem_v, ln_in_g, ln_in_b, w_in, b_gate, ssd_conv_w, ssd_conv_b, ssd_dt_bias, ssd_a_log, ssd_d, ssd_norm_g, conf_dw_w, conf_dw_b, conf_ln_g, conf_ln_b, w_mem_k, w_mem_v, w_br_ssd, w_br_conf, w_br_mem, w_out, ln1_g, ln1_b, w_ff1, b_ff1, w_ff2, b_ff2, ln2_g, ln2_b):
    layer = 0
    row = lambda v: v.reshape(1, -1)
    lane_pad = lambda v: jnp.pad(v.reshape(1, -1), ((0, 0), (0, LANES - v.size)))
    p = {
        "ln_in_g": row(ln_in_g), "ln_in_b": row(ln_in_b),
        "w_in_t": jnp.swapaxes(w_in[layer], 0, 1),
        "b_gate": b_gate[layer],
        "conv_w": ssd_conv_w[layer], "conv_b": row(ssd_conv_b[layer]),
        "dt_bias": lane_pad(ssd_dt_bias[layer]), "a_log": lane_pad(ssd_a_log[layer]),
        "d_skip": row(jnp.repeat(ssd_d[layer], SSD_HEADDIM)), "norm_g": row(ssd_norm_g[layer]),
        "head_onehot": _head_onehot(),
        "conf_w": conf_dw_w[layer], "conf_b": row(conf_dw_b[layer]),
        "conf_ln_g": row(conf_ln_g[layer]), "conf_ln_b": row(conf_ln_b[layer]),
        "w_br_ssd": w_br_ssd[layer].astype(BF16), "w_br_conf": w_br_conf[layer].astype(BF16),
        "w_br_mem": w_br_mem[layer].astype(BF16), "w_out": w_out[layer].astype(BF16),
        "ln1_g": row(ln1_g[layer]), "ln1_b": row(ln1_b[layer]),
        "w_ff1": w_ff1[layer].astype(BF16), "b_ff1": row(b_ff1[layer]),
        "w_ff2": w_ff2[layer].astype(BF16), "b_ff2": row(b_ff2[layer]),
        "ln2_g": row(ln2_g[layer]), "ln2_b": row(ln2_b[layer]),
    }
    n_p, l_p, _ = x_prompt.shape
    n_s, l_s, _ = x_sample.shape

    mem2d = mem_prompt.reshape(n_p * MEM_LEN, D_MODEL)
    p_mem_k = _matmul(mem2d, w_mem_k[layer]).reshape(n_p, MEM_LEN, ATT_W)
    p_mem_v = _matmul(mem2d, w_mem_v[layer]).reshape(n_p, MEM_LEN, ATT_W)
    y_prompt, p_ssm, p_ssd_conv, p_cc = _trunk(
        x_prompt,
        jnp.zeros((n_p, SSD_CONV - 1, CONV_DIM), F32),
        jnp.zeros((n_p, SSD_HEADS, SSD_HEADDIM, SSD_STATE), F32),
        jnp.zeros((n_p, CONF_HIST, D_CONV), F32),
        p_mem_k, p_mem_v, p,
        ssd_cfg=dict(n_seq=1, l_q=SSD_ROWS, g_step=SSD_GROUPS),
        conf_cfg=dict(n_seq=1, l_t=512),
        attn_fn=functools.partial(_attn, n_seq=1, l_t=512))

    y_sample, s_ssm, s_ssd_conv, s_cc = _trunk(
        x_sample, state_ssd_conv[layer], state_ssm[layer], state_conf_conv[layer],
        _interleave_kv(cache_mem_k[layer]), _interleave_kv(cache_mem_v[layer]), p,
        ssd_cfg=dict(n_seq=SSD_ROWS // l_s, l_q=l_s, g_step=4),
        conf_cfg=dict(n_seq=8, l_t=l_s),
        attn_fn=functools.partial(_attn_rows, n_seq=8))

    kv_shape = (DEPTH, n_p, MEM_LEN, MEM_HEADS, MEM_HEAD_DIM)
    return (y_prompt, y_sample, p_ssm[None], p_ssd_conv[None], p_cc[None],
            p_mem_k.reshape(kv_shape), p_mem_v.reshape(kv_shape),
            s_ssm[None], s_ssd_conv[None], s_cc[None])
```

```python
import functools
import math

import jax
import jax.numpy as jnp
from jax import lax
from jax.experimental import pallas as pl
from jax.experimental.pallas import tpu as pltpu

F32 = jnp.float32
BF16 = jnp.bfloat16

D_MODEL = 2048
SSD_HEADDIM = 64
SSD_HEADS = 32
SSD_GROUPS = 8
SSD_HG = 4
SSD_STATE = 128
SSD_CONV = 4
CONV_DIM = 4096
D_CONV = 1024
CONF_KERNEL = 31
MEM_LEN = 256
MEM_HEADS = 4
MEM_HEAD_DIM = 256
ATT_W = 1024
D_FF = 8192
DEPTH = 1
ALPHA = (2.0 * DEPTH) ** 0.25
EPS = 1e-5

O_XBC = 2048
O_DT = 6144
O_GLU = 6176
P_GATE = 6144
P_GLU = 12288
P_Q = 14336
P_W = 15360

GROUP_W = SSD_HG * SSD_HEADDIM
SSD_ROWS = 128
LANES = 128
BF16_ROWS = 16
VMEM_LIMIT = 56 * 1024 * 1024


def _cparams(n_axes):
    return pltpu.CompilerParams(dimension_semantics=("arbitrary",) * n_axes, vmem_limit_bytes=VMEM_LIMIT)


def _layernorm(x, g, b):
    mu = jnp.mean(x, axis=-1, keepdims=True)
    xc = x - mu
    var = jnp.mean(xc * xc, axis=-1, keepdims=True)
    return xc * lax.rsqrt(var + EPS) * g + b


def _sigmoid(x):
    return 0.5 * jnp.tanh(0.5 * x) + 0.5


def _silu(x):
    return x * _sigmoid(x)


def _dot(a, b):
    return jnp.dot(a.astype(BF16), b.astype(BF16), preferred_element_type=F32)


def _dot_nt(a, b):
    return lax.dot_general(a.astype(BF16), b.astype(BF16), (((1,), (1,)), ((), ())), preferred_element_type=F32)


def _iota_div(shape, axis, divisor):
    shift = divisor.bit_length() - 1
    assert 1 << shift == divisor
    return lax.shift_right_logical(lax.broadcasted_iota(jnp.int32, shape, axis), shift)


PROJ_TM = 1024
PROJ_TN = 1024
PROJ_LN_ROWS = 128
PROJ_NA = O_DT // PROJ_TN
PROJ_NB = (P_W - O_DT) // PROJ_TN


def _proj_kernel(x_ref, g_ref, b_ref, wt_ref, wdt_ref, out_ref, dt_ref, xn_ref):
    @pl.when(pl.program_id(1) == 0)
    def _():
        def ln_rows(r, carry):
            r0 = pl.multiple_of(r * PROJ_LN_ROWS, PROJ_LN_ROWS)
            xn = _layernorm(x_ref[pl.ds(r0, PROJ_LN_ROWS), :], g_ref[...], b_ref[...])
            xn_ref[pl.ds(r0, PROJ_LN_ROWS), :] = xn.astype(BF16)
            return carry
        lax.fori_loop(0, x_ref.shape[0] // PROJ_LN_ROWS, ln_rows, 0)
        dt_ref[...] = _dot_nt(xn_ref[...], wdt_ref[...])

    out_ref[...] = _dot_nt(xn_ref[...], wt_ref[...])


def _proj_w_row(j):
    t = BF16_ROWS
    assert O_GLU % t == 0 and PROJ_TN % t == 0
    return t * jnp.where(j < PROJ_NA, j * (PROJ_TN // t), O_GLU // t + (j - PROJ_NA) * (PROJ_TN // t))


def _proj_out_tile(j):
    jb = j - PROJ_NA
    n_glu_q = (P_W - P_GLU) // PROJ_TN
    tile_b = jnp.where(jb < n_glu_q, P_GLU // PROJ_TN + jb, P_GATE // PROJ_TN + jb - n_glu_q)
    return jnp.where(j < PROJ_NA, j, tile_b)


def _proj(x2d, ln_g, ln_b, w_in_t):
    m = x2d.shape[0]
    tm = min(PROJ_TM, m)
    grid = (m // tm, PROJ_NA + PROJ_NB)
    return pl.pallas_call(
        _proj_kernel,
        grid=grid,
        in_specs=[
            pl.BlockSpec((tm, D_MODEL), lambda i, j: (i, 0), pipeline_mode=pl.Buffered(1)),
            pl.BlockSpec((1, D_MODEL), lambda i, j: (0, 0)),
            pl.BlockSpec((1, D_MODEL), lambda i, j: (0, 0)),
            pl.BlockSpec((pl.Element(PROJ_TN), pl.Element(D_MODEL)), lambda i, j: (_proj_w_row(j), 0)),
            pl.BlockSpec((LANES, D_MODEL), lambda i, j: (O_DT // LANES, 0)),
        ],
        out_specs=[
            pl.BlockSpec((tm, PROJ_TN), lambda i, j: (i, _proj_out_tile(j))),
            pl.BlockSpec((tm, LANES), lambda i, j: (i, 0)),
        ],
        out_shape=[jax.ShapeDtypeStruct((m, P_W), F32), jax.ShapeDtypeStruct((m, LANES), F32)],
        scratch_shapes=[pltpu.VMEM((tm, D_MODEL), BF16)],
        compiler_params=_cparams(2),
        name="proj",
    )(x2d, ln_g, ln_b, w_in_t, w_in_t)


def _matmul_kernel(x_ref, w_ref, out_ref):
    out_ref[...] = _dot(x_ref[...], w_ref[...])


def _matmul(x2d, w, tn=512):
    m, k = x2d.shape
    n = w.shape[1]
    return pl.pallas_call(
        _matmul_kernel,
        grid=(n // tn,),
        in_specs=[pl.BlockSpec((m, k), lambda j: (0, 0)), pl.BlockSpec((k, tn), lambda j: (0, j))],
        out_specs=pl.BlockSpec((m, tn), lambda j: (0, j)),
        out_shape=jax.ShapeDtypeStruct((m, n), F32),
        compiler_params=_cparams(1),
        name="memkv",
    )(x2d, w)


def _split3(v):
    hi = v.astype(BF16)
    r1 = v - hi.astype(F32)
    mid = r1.astype(BF16)
    lo = (r1 - mid.astype(F32)).astype(BF16)
    return jnp.concatenate([hi, mid, lo], axis=1)


def _expand_heads(pieces, onehot3):
    return jnp.dot(pieces, onehot3, preferred_element_type=F32)


def _conv4(pad_ref, u, hist_ref, hist_out_ref, w_ref, b_ref, first, l_q):
    @pl.when(first)
    def _():
        pad_ref[:, 5:8, :] = hist_ref[...]

    @pl.when(jnp.logical_not(first))
    def _():
        pad_ref[:, 5:8, :] = pad_ref[:, 5 + l_q:8 + l_q, :]

    pad_ref[:, 8:8 + l_q, :] = u
    full = pad_ref[...]
    acc = b_ref[...] + w_ref[SSD_CONV - 1:SSD_CONV, :] * u
    for d in range(1, SSD_CONV):
        shifted = pltpu.roll(full, d, axis=1)[:, 8:8 + l_q, :]
        acc = acc + w_ref[SSD_CONV - 1 - d:SSD_CONV - d, :] * shifted
    hist_out_ref[...] = pad_ref[:, 5 + l_q:8 + l_q, :]
    return _silu(acc)


def _ssd_kernel(z_ref, x_ref, bm_ref, cm_ref, dt_ref, hx_ref, hb_ref, hc_ref, h0_ref,
                wx_ref, wb_ref, wc_ref, bx_ref, bb_ref, bc_ref, dtb_ref, alog_ref, dsk_ref, ng_ref, oh_ref,
                y_ref, ox_ref, ob_ref, oc_ref, h_ref,
                px_ref, pb_ref, pc_ref, yoff_ref, *, n_seq, l_q, g_step, single_chunk):
    rows = n_seq * l_q
    gs = pl.program_id(1)
    first = pl.program_id(2) == 0

    xs = _conv4(px_ref, x_ref[...], hx_ref, ox_ref, wx_ref, bx_ref, first, l_q).reshape(rows, g_step * GROUP_W)
    bm = _conv4(pb_ref, bm_ref[...], hb_ref, ob_ref, wb_ref, bb_ref, first, l_q).reshape(rows, g_step * SSD_STATE)
    cm = _conv4(pc_ref, cm_ref[...], hc_ref, oc_ref, wc_ref, bc_ref, first, l_q).reshape(rows, g_step * SSD_STATE)

    if single_chunk:
        h_in_ref = h0_ref
    else:
        h_in_ref = h_ref

        @pl.when(first)
        def _():
            h_ref[...] = h0_ref[...]

    lane = lax.broadcasted_iota(jnp.int32, (1, LANES), 1)
    dt_raw = dt_ref[...].reshape(rows, LANES) + dtb_ref[...]
    dt = jnp.maximum(dt_raw, 0.0) + jnp.log1p(jnp.exp(-jnp.abs(dt_raw)))
    dt = jnp.where(lane < SSD_HEADS, dt, 0.0)
    a = dt * (-jnp.exp(alog_ref[...]))
    if g_step != SSD_GROUPS:
        src = jnp.bitwise_and(lax.broadcasted_iota(jnp.int32, (3 * LANES, LANES), 0), LANES - 1)
        dst = lax.broadcasted_iota(jnp.int32, (3 * LANES, LANES), 1)
        pick = (src == dst + gs * (g_step * SSD_HG)).astype(BF16)
        dt = _expand_heads(_split3(dt), pick)
        a = _expand_heads(_split3(a), pick)
    rq = lax.broadcasted_iota(jnp.int32, (rows, rows), 0)
    ck = lax.broadcasted_iota(jnp.int32, (rows, rows), 1)
    same = _iota_div((rows, rows), 0, l_q) == _iota_div((rows, rows), 1, l_q)
    causal = jnp.logical_and(same, ck <= rq)
    sums = jnp.dot(jnp.concatenate([causal, same], axis=0).astype(BF16), _split3(a), preferred_element_type=F32)
    sums = sums[:, 0:LANES] + sums[:, LANES:2 * LANES] + sums[:, 2 * LANES:3 * LANES]
    acs = sums[0:rows]
    tot = sums[rows:2 * rows]
    acs_t = acs.T
    seq_of_col = _iota_div((1, rows), 1, l_q)

    dt_p = _split3(dt)
    to_end_p = _split3(jnp.exp(tot - acs))
    from_start_p = _split3(jnp.exp(acs))
    head_of_col = _iota_div((1, GROUP_W), 1, SSD_HEADDIM)
    groups = range(g_step)
    cols_x = [slice(gi * GROUP_W, (gi + 1) * GROUP_W) for gi in groups]
    cols_n = [slice(gi * SSD_STATE, (gi + 1) * SSD_STATE) for gi in groups]

    dt_x = [_expand_heads(dt_p, oh_ref[:, c]) for c in cols_x]
    to_end_x = [_expand_heads(to_end_p, oh_ref[:, c]) for c in cols_x]
    from_start_x = [_expand_heads(from_start_p, oh_ref[:, c]) for c in cols_x]
    cbm = [jnp.where(causal, _dot_nt(cm[:, c], bm[:, c]), 0.0) for c in cols_n]

    m_cat, x_cat, xd_t = [], [], []
    for gi in groups:
        lane0 = gi * SSD_HG
        xdt = xs[:, cols_x[gi]] * dt_x[gi]
        xdt_b = xdt.astype(BF16)
        m_heads, x_heads = [], []
        for h in range(SSD_HG):
            diff = acs[:, lane0 + h:lane0 + h + 1] - acs_t[lane0 + h:lane0 + h + 1, :]
            m_heads.append((cbm[gi] * jnp.exp(jnp.where(causal, diff, 0.0))).astype(BF16))
            x_heads.append(jnp.where(head_of_col == h, xdt_b, jnp.zeros_like(xdt_b)))
        m_cat.append(jnp.concatenate(m_heads, axis=1))
        x_cat.append(jnp.concatenate(x_heads, axis=0))
        xd_t.append((xdt * to_end_x[gi]).T)

    y_in = [jnp.dot(m_cat[gi], x_cat[gi], preferred_element_type=F32) for gi in groups]

    for gi in groups:
        lane0 = gi * SSD_HG
        bg = bm[:, cols_n[gi]]
        cg = cm[:, cols_n[gi]]
        for s in range(n_seq):
            h_old = h_in_ref[s, lane0:lane0 + SSD_HG].reshape(GROUP_W, SSD_STATE)
            yoff_ref[s * l_q:(s + 1) * l_q, cols_x[gi]] = _dot_nt(cg[s * l_q:(s + 1) * l_q, :], h_old)
            xd_s = xd_t[gi] if n_seq == 1 else jnp.where(seq_of_col == s, xd_t[gi], 0.0)
            upd = _dot(xd_s, bg)
            dec = jnp.concatenate(
                [jnp.broadcast_to(jnp.exp(tot[s * l_q:s * l_q + 1, lane0 + h:lane0 + h + 1]),
                                  (SSD_HEADDIM, SSD_STATE)) for h in range(SSD_HG)], axis=0)
            h_ref[s, lane0:lane0 + SSD_HG] = (h_old * dec + upd).reshape(SSD_HG, SSD_HEADDIM, SSD_STATE)

    for gi in groups:
        c = cols_x[gi]
        y = y_in[gi] + yoff_ref[:, c] * from_start_x[gi] + dsk_ref[:, c] * xs[:, c]
        v = y * _silu(z_ref[:, :, c].reshape(rows, GROUP_W))
        v = v * lax.rsqrt(jnp.mean(v * v, axis=-1, keepdims=True) + EPS)
        y_ref[:, c] = (v * ng_ref[:, c]).astype(BF16)


def _head_onehot():
    lane = jnp.arange(3 * LANES)[:, None] % LANES
    head = jnp.arange(D_MODEL)[None, :] // SSD_HEADDIM
    return (lane == head).astype(BF16)


SSD_N_IN = 20
SSD_N_OUT = 5


def _ssd_cast_kernel(*refs, n_cast, **kw):
    ssd_in = refs[:SSD_N_IN]
    cast_in = refs[SSD_N_IN:SSD_N_IN + n_cast]
    ssd_out = refs[SSD_N_IN + n_cast:SSD_N_IN + n_cast + SSD_N_OUT]
    cast_out = refs[SSD_N_IN + n_cast + SSD_N_OUT:SSD_N_IN + 2 * n_cast + SSD_N_OUT]
    scratch = refs[SSD_N_IN + 2 * n_cast + SSD_N_OUT:]
    for src, dst in zip(cast_in, cast_out):
        dst[...] = src[...].astype(BF16)
    _ssd_kernel(*ssd_in, *ssd_out, *scratch, **kw)


def _ssd(proj3, dt3, hist, h0, p, *, n_seq, l_q, g_step, cast_weights=()):
    nb, l_seq, _ = proj3.shape
    wx = g_step * GROUP_W
    wn = g_step * SSD_STATE
    grid = (nb // n_seq, SSD_GROUPS // g_step, l_seq // l_q)
    ob = (O_XBC + D_MODEL) // wn
    oc = ob + (SSD_GROUPS * SSD_STATE) // wn
    cb = D_MODEL // wn
    cc = cb + (SSD_GROUPS * SSD_STATE) // wn

    def rows_spec(width, first_block):
        return pl.BlockSpec((n_seq, l_q, width), lambda n, g, c: (n, c, first_block + g))

    def hist_spec(width, first_block):
        return pl.BlockSpec((n_seq, SSD_CONV - 1, width), lambda n, g, c: (n, 0, first_block + g))

    def par_spec(r, width, first_block):
        return pl.BlockSpec((r, width), lambda n, g, c: (0, first_block + g))

    state_spec = pl.BlockSpec((n_seq, g_step * SSD_HG, SSD_HEADDIM, SSD_STATE), lambda n, g, c: (n, g, 0, 0))
    full_lane = pl.BlockSpec((1, LANES), lambda n, g, c: (0, 0))
    n_steps = grid[0] * grid[1] * grid[2]
    def cast_specs():
        specs = []
        for w in cast_weights:
            assert w.shape[0] % (n_steps * BF16_ROWS) == 0, w.shape
            specs.append(pl.BlockSpec((w.shape[0] // n_steps, w.shape[1]),
                                      lambda n, g, c: ((n * grid[1] + g) * grid[2] + c, 0)))
        return specs
    kern = functools.partial(_ssd_cast_kernel, n_cast=len(cast_weights), n_seq=n_seq, l_q=l_q, g_step=g_step,
                             single_chunk=grid[2] == 1)
    return pl.pallas_call(
        kern,
        grid=grid,
        in_specs=[
            rows_spec(wx, 0), rows_spec(wx, O_XBC // wx), rows_spec(wn, ob), rows_spec(wn, oc),
            pl.BlockSpec((n_seq, l_q, LANES), lambda n, g, c: (n, c, 0)),
            hist_spec(wx, 0), hist_spec(wn, cb), hist_spec(wn, cc),
            state_spec,
            par_spec(SSD_CONV, wx, 0), par_spec(SSD_CONV, wn, cb), par_spec(SSD_CONV, wn, cc),
            par_spec(1, wx, 0), par_spec(1, wn, cb), par_spec(1, wn, cc),
            full_lane, full_lane, par_spec(1, wx, 0), par_spec(1, wx, 0),
            pl.BlockSpec((3 * LANES, wx), lambda n, g, c: (0, 0)),
        ] + cast_specs(),
        out_specs=[
            pl.BlockSpec((SSD_ROWS, wx), lambda n, g, c: (n * grid[2] + c, g)),
            hist_spec(wx, 0), hist_spec(wn, 0), hist_spec(wn, 0),
            state_spec,
        ] + cast_specs(),
        out_shape=[
            jax.ShapeDtypeStruct((nb * l_seq, D_MODEL), BF16),
            jax.ShapeDtypeStruct((nb, SSD_CONV - 1, D_MODEL), F32),
            jax.ShapeDtypeStruct((nb, SSD_CONV - 1, SSD_GROUPS * SSD_STATE), F32),
            jax.ShapeDtypeStruct((nb, SSD_CONV - 1, SSD_GROUPS * SSD_STATE), F32),
            jax.ShapeDtypeStruct((nb, SSD_HEADS, SSD_HEADDIM, SSD_STATE), F32),
        ] + [jax.ShapeDtypeStruct(w.shape, BF16) for w in cast_weights],
        scratch_shapes=[
            pltpu.VMEM((n_seq, 8 + l_q, wx), F32),
            pltpu.VMEM((n_seq, 8 + l_q, wn), F32),
            pltpu.VMEM((n_seq, 8 + l_q, wn), F32),
            pltpu.VMEM((SSD_ROWS, wx), F32),
        ],
        compiler_params=_cparams(3),
        name="ssd",
    )(proj3, proj3, proj3, proj3, dt3, hist, hist, hist, h0,
      p["conv_w"], p["conv_w"], p["conv_w"], p["conv_b"], p["conv_b"], p["conv_b"],
      p["dt_bias"], p["a_log"], p["d_skip"], p["norm_g"], p["head_onehot"], *cast_weights)


CONF_HIST = CONF_KERNEL - 1
CONF_PAD0 = 32


SUBLANES = 8
CONF_NORM_ROWS = 128


def _conf_kernel(a_ref, b_ref, hist_ref, w_ref, bias_ref, g_ref, beta_ref, c_ref, hist_out_ref,
                 pad_ref, ph_ref, cf_ref, wrep_ref, *, n_seq, l_t, rb):
    first = pl.program_id(1) == 0
    lo = CONF_PAD0 - CONF_HIST
    rows = CONF_PAD0 + l_t

    @pl.when(first)
    def _():
        pad_ref[:, lo:CONF_PAD0, :] = hist_ref[...]

    @pl.when(jnp.logical_not(first))
    def _():
        pad_ref[:, lo:CONF_PAD0, :] = pad_ref[:, lo + l_t:CONF_PAD0 + l_t, :]

    pad_ref[:, CONF_PAD0:CONF_PAD0 + l_t, :] = a_ref[...] * _sigmoid(b_ref[...])
    hist_out_ref[...] = pad_ref[:, lo + l_t:CONF_PAD0 + l_t, :]

    full = pad_ref[...]
    for p in range(1, SUBLANES):
        ph_ref[p - 1] = pltpu.roll(full, rows - p, axis=1)

    for j in range(CONF_KERNEL):
        wrep_ref[j] = jnp.broadcast_to(w_ref[j:j + 1, :], (SUBLANES, D_CONV))

    n_rb = l_t // rb

    def conv_block(i, carry):
        s = i // n_rb
        r0 = (i % n_rb) * rb
        n_tiles = rb // SUBLANES
        out_row = pl.multiple_of(s * l_t + r0, SUBLANES)
        for c0 in range(0, D_CONV, LANES):
            lanes = slice(c0, c0 + LANES)
            acc = [jnp.broadcast_to(bias_ref[:, lanes], (SUBLANES, LANES))] * n_tiles
            for j in range(CONF_KERNEL):
                p = (lo + j) % SUBLANES
                src = pad_ref if p == 0 else ph_ref.at[p - 1]
                wj = wrep_ref[j, :, lanes]
                for k in range(n_tiles):
                    start = pl.multiple_of(r0 + (lo + j - p) + k * SUBLANES, SUBLANES)
                    acc[k] = acc[k] + wj * src[s, pl.ds(start, SUBLANES), lanes]
            cf_ref[pl.ds(out_row, rb), lanes] = jnp.concatenate(acc, axis=0)
        return carry

    lax.fori_loop(0, n_seq * n_rb, conv_block, 0)

    nb_rows = min(CONF_NORM_ROWS, n_seq * l_t)

    def norm_block(i, carry):
        r0 = pl.multiple_of(i * nb_rows, nb_rows)
        v = _silu(_layernorm(cf_ref[pl.ds(r0, nb_rows), :], g_ref[...], beta_ref[...]))
        c_ref[pl.ds(r0, nb_rows), :] = v.astype(BF16)
        return carry

    lax.fori_loop(0, (n_seq * l_t) // nb_rows, norm_block, 0)


def _conf(proj3, hist, p, *, n_seq, l_t):
    nb, l_seq, _ = proj3.shape
    rb = min(32, l_t)
    grid = (nb // n_seq, l_seq // l_t)
    par = lambda r: pl.BlockSpec((r, D_CONV), lambda n, t: (0, 0))
    hist_spec = pl.BlockSpec((n_seq, CONF_HIST, D_CONV), lambda n, t: (n, 0, 0))
    kern = functools.partial(_conf_kernel, n_seq=n_seq, l_t=l_t, rb=rb)
    return pl.pallas_call(
        kern,
        grid=grid,
        in_specs=[
            pl.BlockSpec((n_seq, l_t, D_CONV), lambda n, t: (n, t, P_GLU // D_CONV)),
            pl.BlockSpec((n_seq, l_t, D_CONV), lambda n, t: (n, t, P_GLU // D_CONV + 1)),
            hist_spec, par(CONF_KERNEL), par(1), par(1), par(1),
        ],
        out_specs=[pl.BlockSpec((n_seq * l_t, D_CONV), lambda n, t: (n * grid[1] + t, 0)), hist_spec],
        out_shape=[jax.ShapeDtypeStruct((nb * l_seq, D_CONV), BF16),
                   jax.ShapeDtypeStruct((nb, CONF_HIST, D_CONV), F32)],
        scratch_shapes=[pltpu.VMEM((n_seq, CONF_PAD0 + l_t, D_CONV), F32),
                        pltpu.VMEM((SUBLANES - 1, n_seq, CONF_PAD0 + l_t, D_CONV), F32),
                        pltpu.VMEM((n_seq * l_t, D_CONV), F32),
                        pltpu.VMEM((CONF_KERNEL, SUBLANES, D_CONV), F32)],
        compiler_params=_cparams(2),
        name="conf",
    )(proj3, proj3, hist, p["conf_w"], p["conf_b"], p["conf_ln_g"], p["conf_ln_b"])


def _attn_kernel(q_ref, k_ref, v_ref, o_ref, *, n_seq):
    scale = MEM_HEAD_DIM ** -0.5
    few_rows = q_ref.shape[1] < LANES
    for h in range(MEM_HEADS):
        cols = slice(h * MEM_HEAD_DIM, (h + 1) * MEM_HEAD_DIM)
        if few_rows:
            scores = [_dot_nt(k_ref[s, :, cols], q_ref[s, :, cols]) * scale for s in range(n_seq)]
            axis = 0
        else:
            scores = [_dot_nt(q_ref[s, :, cols], k_ref[s, :, cols]) * scale for s in range(n_seq)]
            axis = 1
        probs = []
        for sc in scores:
            e = jnp.exp(sc - jnp.max(sc, axis=axis, keepdims=True))
            probs.append((e / jnp.sum(e, axis=axis, keepdims=True)).astype(BF16))
        if few_rows:
            outs = [lax.dot_general(pr, v_ref[s, :, cols].astype(BF16), (((0,), (0,)), ((), ())),
                                    preferred_element_type=F32) for s, pr in enumerate(probs)]
        else:
            outs = [_dot(pr, v_ref[s, :, cols]) for s, pr in enumerate(probs)]
        o_ref[:, cols] = jnp.concatenate(outs, axis=0).astype(BF16)


def _attn(proj3, mem_k, mem_v, *, n_seq, l_t):
    nb, l_seq, _ = proj3.shape
    grid = (nb // n_seq, l_seq // l_t)
    kv_spec = pl.BlockSpec((n_seq, MEM_LEN, ATT_W), lambda n, t: (n, 0, 0))
    return pl.pallas_call(
        functools.partial(_attn_kernel, n_seq=n_seq),
        grid=grid,
        in_specs=[pl.BlockSpec((n_seq, l_t, ATT_W), lambda n, t: (n, t, P_Q // ATT_W)), kv_spec, kv_spec],
        out_specs=pl.BlockSpec((n_seq * l_t, ATT_W), lambda n, t: (n * grid[1] + t, 0)),
        out_shape=jax.ShapeDtypeStruct((nb * l_seq, ATT_W), BF16),
        compiler_params=_cparams(2),
        name="attn",
    )(proj3, mem_k, mem_v)


KV_HALVES = MEM_HEAD_DIM // LANES
KV_ROWS = MEM_LEN * MEM_HEADS * KV_HALVES


def _attn_rows_kernel(q_ref, k_ref, v_ref, o_ref, *, n_seq, l_t):
    scale = MEM_HEAD_DIM ** -0.5
    nq = MEM_HEADS * l_t
    col = lax.broadcasted_iota(jnp.int32, (nq, KV_ROWS), 1)
    valid = jnp.bitwise_and(col, MEM_HEADS * KV_HALVES - 1) == _iota_div((nq, KV_ROWS), 0, l_t)
    parts = []
    for s in range(n_seq):
        q2 = jnp.concatenate([q_ref[s, :, (h * KV_HALVES + c) * LANES:(h * KV_HALVES + c + 1) * LANES]
                              for c in range(KV_HALVES) for h in range(MEM_HEADS)], axis=0)
        parts.append(_dot_nt(q2, k_ref[s]))
    probs = []
    for part in parts:
        sc = (part[0:nq] + pltpu.roll(part[nq:2 * nq], KV_ROWS - MEM_HEADS, axis=1)) * scale
        sc = jnp.where(valid, sc, -jnp.inf)
        e = jnp.exp(sc - jnp.max(sc, axis=1, keepdims=True))
        pr = e / jnp.sum(e, axis=1, keepdims=True)
        probs.append(jnp.concatenate([pr, pltpu.roll(pr, MEM_HEADS, axis=1)], axis=0).astype(BF16))
    outs = [_dot(p2, v_ref[s]) for s, p2 in enumerate(probs)]
    for c in range(KV_HALVES):
        for h in range(MEM_HEADS):
            r0 = (c * MEM_HEADS + h) * l_t
            piece = jnp.concatenate([o[r0:r0 + l_t] for o in outs], axis=0)
            o_ref[:, (h * KV_HALVES + c) * LANES:(h * KV_HALVES + c + 1) * LANES] = piece.astype(BF16)


def _attn_rows(proj3, k_rows, v_rows, *, n_seq):
    nb, l_t, _ = proj3.shape
    assert l_t % SUBLANES == 0 and KV_HALVES == 2
    kv_spec = pl.BlockSpec((n_seq, KV_ROWS, LANES), lambda n: (n, 0, 0))
    return pl.pallas_call(
        functools.partial(_attn_rows_kernel, n_seq=n_seq, l_t=l_t),
        grid=(nb // n_seq,),
        in_specs=[pl.BlockSpec((n_seq, l_t, ATT_W), lambda n: (n, 0, P_Q // ATT_W)), kv_spec, kv_spec],
        out_specs=pl.BlockSpec((n_seq * l_t, ATT_W), lambda n: (n, 0)),
        out_shape=jax.ShapeDtypeStruct((nb * l_t, ATT_W), BF16),
        compiler_params=_cparams(1),
        name="attn_rows",
    )(proj3, k_rows, v_rows)


def _interleave_kv(kv):
    b = kv.shape[0]
    kv = kv.reshape(b, MEM_LEN, MEM_HEADS, KV_HALVES, LANES)
    return jnp.transpose(kv, (0, 1, 3, 2, 4)).reshape(b, KV_ROWS, LANES)


MERGE_TM = 256


def _merge_kernel(x_ref, y_ref, c_ref, o_ref, g0_ref, g1_ref, g2_ref, bg_ref, wssd_ref, wconf_ref, wmem_ref,
                  wout_ref, lg_ref, lb_ref, l1g_ref, l1b_ref, h1_ref):
    merged = _sigmoid(g0_ref[...] + bg_ref[0:1, :]) * _dot(y_ref[...], wssd_ref[...])
    merged = merged + _sigmoid(g1_ref[...] + bg_ref[1:2, :]) * _dot(c_ref[...], wconf_ref[...])
    merged = merged + _sigmoid(g2_ref[...] + bg_ref[2:3, :]) * _dot(o_ref[...], wmem_ref[...])
    mix = _dot(merged, wout_ref[...])
    h = _layernorm(x_ref[...], lg_ref[...], lb_ref[...])
    h1_ref[...] = _layernorm(ALPHA * h + mix, l1g_ref[...], l1b_ref[...])


def _merge(x2d, y2d, c2d, o2d, proj2d, p, wb):
    m = x2d.shape[0]
    tm = MERGE_TM
    rows = lambda w, blk: pl.BlockSpec((tm, w), lambda i: (i, blk))
    whole = lambda a: pl.BlockSpec(a.shape, lambda i: (0, 0), pipeline_mode=pl.Buffered(1))
    vec = pl.BlockSpec((1, D_MODEL), lambda i: (0, 0))
    g_blk = P_GATE // D_MODEL
    return pl.pallas_call(
        _merge_kernel,
        grid=(m // tm,),
        in_specs=[
            rows(D_MODEL, 0), rows(D_MODEL, 0), rows(D_CONV, 0), rows(ATT_W, 0),
            rows(D_MODEL, g_blk), rows(D_MODEL, g_blk + 1), rows(D_MODEL, g_blk + 2),
            pl.BlockSpec((3, D_MODEL), lambda i: (0, 0)),
            whole(wb["w_br_ssd"]), whole(wb["w_br_conf"]), whole(wb["w_br_mem"]), whole(wb["w_out"]),
            vec, vec, vec, vec,
        ],
        out_specs=rows(D_MODEL, 0),
        out_shape=jax.ShapeDtypeStruct((m, D_MODEL), F32),
        compiler_params=_cparams(1),
        name="merge",
    )(x2d, y2d, c2d, o2d, proj2d, proj2d, proj2d, p["b_gate"],
      wb["w_br_ssd"], wb["w_br_conf"], wb["w_br_mem"], wb["w_out"],
      p["ln_in_g"], p["ln_in_b"], p["ln1_g"], p["ln1_b"])


MLP_TM = 512
MLP_TF = 1024


def _mlp_kernel(h_ref, w1_ref, b1_ref, w2_ref, b2_ref, g_ref, b_ref, out_ref, hb_ref):
    f = pl.program_id(1)

    @pl.when(f == 0)
    def _():
        hb_ref[...] = h_ref[...].astype(BF16)
        out_ref[...] = jnp.zeros_like(out_ref)

    a = jnp.maximum(jnp.dot(hb_ref[...], w1_ref[...], preferred_element_type=F32) + b1_ref[...], 0.0)
    out_ref[...] += _dot(a * a, w2_ref[...])

    @pl.when(f == pl.num_programs(1) - 1)
    def _():
        out_ref[...] = _layernorm(ALPHA * h_ref[...] + (out_ref[...] + b2_ref[...]), g_ref[...], b_ref[...])


def _mlp(h2d, p, wb):
    m = h2d.shape[0]
    tm, tf = MLP_TM, MLP_TF
    vec = pl.BlockSpec((1, D_MODEL), lambda i, f: (0, 0))
    return pl.pallas_call(
        _mlp_kernel,
        grid=(m // tm, D_FF // tf),
        in_specs=[
            pl.BlockSpec((tm, D_MODEL), lambda i, f: (i, 0)),
            pl.BlockSpec((D_MODEL, tf), lambda i, f: (0, f)),
            pl.BlockSpec((1, tf), lambda i, f: (0, f)),
            pl.BlockSpec((tf, D_MODEL), lambda i, f: (f, 0)),
            vec, vec, vec,
        ],
        out_specs=pl.BlockSpec((tm, D_MODEL), lambda i, f: (i, 0)),
        out_shape=jax.ShapeDtypeStruct((m, D_MODEL), F32),
        scratch_shapes=[pltpu.VMEM((tm, D_MODEL), BF16)],
        compiler_params=_cparams(2),
        name="mlp",
    )(h2d, wb["w_ff1"], p["b_ff1"], wb["w_ff2"], p["b_ff2"], p["ln2_g"], p["ln2_b"])


CAST_WEIGHTS = ("w_ff1", "w_ff2", "w_br_ssd", "w_out", "w_br_conf", "w_br_mem")


def _trunk(x, ssd_hist, ssm0, conf_hist, mem_k, mem_v, p, wb, *, ssd_cfg, conf_cfg, attn_fn):
    nb, l_seq, _ = x.shape
    x2d = x.reshape(nb * l_seq, D_MODEL)
    proj2d, dt2d = _proj(x2d, p["ln_in_g"], p["ln_in_b"], p["w_in_t"])
    proj3 = proj2d.reshape(nb, l_seq, P_W)
    dt3 = dt2d.reshape(nb, l_seq, LANES)
    if wb is None:
        y, hx, hb, hc, ssm1, *cast = _ssd(proj3, dt3, ssd_hist, ssm0, p, cast_weights=[p[n] for n in CAST_WEIGHTS],
                                          **ssd_cfg)
        wb = dict(zip(CAST_WEIGHTS, cast))
    else:
        y, hx, hb, hc, ssm1 = _ssd(proj3, dt3, ssd_hist, ssm0, p, **ssd_cfg)
    c, conf_hist1 = _conf(proj3, conf_hist, p, **conf_cfg)
    o = attn_fn(proj3, mem_k, mem_v)
    h1 = _merge(x2d, y, c, o, proj2d, p, wb)
    out = _mlp(h1, p, wb).reshape(nb, l_seq, D_MODEL)
    return out, ssm1, jnp.concatenate([hx, hb, hc], axis=-1), conf_hist1, wb


def kernel(x_prompt, x_sample, mem_prompt, state_ssm, state_ssd_conv, state_conf_conv, cache_mem_k, cache_mem_v, ln_in_g, ln_in_b, w_in, b_gate, ssd_conv_w, ssd_conv_b, ssd_dt_bias, ssd_a_log, ssd_d, ssd_norm_g, conf_dw_w, conf_dw_b, conf_ln_g, conf_ln_b, w_mem_k, w_mem_v, w_br_ssd, w_br_conf, w_br_mem, w_out, ln1_g, ln1_b, w_ff1, b_ff1, w_ff2, b_ff2, ln2_g, ln2_b):
    layer = 0
    row = lambda v: v.reshape(1, -1)
    lane_pad = lambda v: jnp.pad(v.reshape(1, -1), ((0, 0), (0, LANES - v.size)))
    p = {
        "ln_in_g": row(ln_in_g), "ln_in_b": row(ln_in_b),
        "w_in_t": jnp.swapaxes(w_in[layer], 0, 1),
        "b_gate": b_gate[layer],
        "conv_w": ssd_conv_w[layer], "conv_b": row(ssd_conv_b[layer]),
        "dt_bias": lane_pad(ssd_dt_bias[layer]), "a_log": lane_pad(ssd_a_log[layer]),
        "d_skip": row(jnp.repeat(ssd_d[layer], SSD_HEADDIM)), "norm_g": row(ssd_norm_g[layer]),
        "head_onehot": _head_onehot(),
        "conf_w": conf_dw_w[layer], "conf_b": row(conf_dw_b[layer]),
        "conf_ln_g": row(conf_ln_g[layer]), "conf_ln_b": row(conf_ln_b[layer]),
        "w_br_ssd": w_br_ssd[layer], "w_br_conf": w_br_conf[layer],
        "w_br_mem": w_br_mem[layer], "w_out": w_out[layer],
        "ln1_g": row(ln1_g[layer]), "ln1_b": row(ln1_b[layer]),
        "w_ff1": w_ff1[layer], "b_ff1": row(b_ff1[layer]),
        "w_ff2": w_ff2[layer], "b_ff2": row(b_ff2[layer]),
        "ln2_g": row(ln2_g[layer]), "ln2_b": row(ln2_b[layer]),
    }
    n_p, l_p, _ = x_prompt.shape
    n_s, l_s, _ = x_sample.shape

    mem2d = mem_prompt.reshape(n_p * MEM_LEN, D_MODEL)
    p_mem_k = _matmul(mem2d, w_mem_k[layer]).reshape(n_p, MEM_LEN, ATT_W)
    p_mem_v = _matmul(mem2d, w_mem_v[layer]).reshape(n_p, MEM_LEN, ATT_W)
    y_prompt, p_ssm, p_ssd_conv, p_cc, wb = _trunk(
        x_prompt,
        jnp.zeros((n_p, SSD_CONV - 1, CONV_DIM), F32),
        jnp.zeros((n_p, SSD_HEADS, SSD_HEADDIM, SSD_STATE), F32),
        jnp.zeros((n_p, CONF_HIST, D_CONV), F32),
        p_mem_k, p_mem_v, p, None,
        ssd_cfg=dict(n_seq=1, l_q=SSD_ROWS, g_step=SSD_GROUPS),
        conf_cfg=dict(n_seq=1, l_t=512),
        attn_fn=functools.partial(_attn, n_seq=1, l_t=512))

    y_sample, s_ssm, s_ssd_conv, s_cc, _ = _trunk(
        x_sample, state_ssd_conv[layer], state_ssm[layer], state_conf_conv[layer],
        _interleave_kv(cache_mem_k[layer]), _interleave_kv(cache_mem_v[layer]), p, wb,
        ssd_cfg=dict(n_seq=SSD_ROWS // l_s, l_q=l_s, g_step=4),
        conf_cfg=dict(n_seq=8, l_t=l_s),
        attn_fn=functools.partial(_attn_rows, n_seq=8))

    kv_shape = (DEPTH, n_p, MEM_LEN, MEM_HEADS, MEM_HEAD_DIM)
    return (y_prompt, y_sample, p_ssm[None], p_ssd_conv[None], p_cc[None],
            p_mem_k.reshape(kv_shape), p_mem_v.reshape(kv_shape),
            s_ssm[None], s_ssd_conv[None], s_cc[None])
```

```python
import functools
import math

import jax
import jax.numpy as jnp
from jax import lax
from jax.experimental import pallas as pl
from jax.experimental.pallas import tpu as pltpu

F32 = jnp.float32
BF16 = jnp.bfloat16

D_MODEL = 2048
SSD_HEADDIM = 64
SSD_HEADS = 32
SSD_GROUPS = 8
SSD_HG = 4
SSD_STATE = 128
SSD_CONV = 4
CONV_DIM = 4096
D_CONV = 1024
CONF_KERNEL = 31
MEM_LEN = 256
MEM_HEADS = 4
MEM_HEAD_DIM = 256
ATT_W = 1024
D_FF = 8192
DEPTH = 1
ALPHA = (2.0 * DEPTH) ** 0.25
EPS = 1e-5

O_XBC = 2048
O_DT = 6144
O_GLU = 6176
P_GATE = 6144
P_GLU = 12288
P_Q = 14336
P_W = 15360

GROUP_W = SSD_HG * SSD_HEADDIM
SSD_ROWS = 128
LANES = 128
BF16_ROWS = 16
VMEM_LIMIT = 56 * 1024 * 1024


def _cparams(n_axes):
    return pltpu.CompilerParams(dimension_semantics=("arbitrary",) * n_axes, vmem_limit_bytes=VMEM_LIMIT)


def _layernorm(x, g, b):
    mu = jnp.mean(x, axis=-1, keepdims=True)
    xc = x - mu
    var = jnp.mean(xc * xc, axis=-1, keepdims=True)
    return xc * lax.rsqrt(var + EPS) * g + b


def _sigmoid(x):
    return 0.5 * jnp.tanh(0.5 * x) + 0.5


def _silu(x):
    return x * _sigmoid(x)


def _dot(a, b):
    return jnp.dot(a.astype(BF16), b.astype(BF16), preferred_element_type=F32)


def _dot_nt(a, b):
    return lax.dot_general(a.astype(BF16), b.astype(BF16), (((1,), (1,)), ((), ())), preferred_element_type=F32)


def _iota_div(shape, axis, divisor):
    shift = divisor.bit_length() - 1
    assert 1 << shift == divisor
    return lax.shift_right_logical(lax.broadcasted_iota(jnp.int32, shape, axis), shift)


PROJ_TM = 1024
PROJ_TN = 1024
PROJ_LN_ROWS = 128
PROJ_NA = O_DT // PROJ_TN
PROJ_NB = (P_W - O_DT) // PROJ_TN


def _proj_kernel(x_ref, g_ref, b_ref, wt_ref, wdt_ref, out_ref, dt_ref, xn_ref):
    @pl.when(pl.program_id(1) == 0)
    def _():
        def ln_rows(r, carry):
            r0 = pl.multiple_of(r * PROJ_LN_ROWS, PROJ_LN_ROWS)
            xn = _layernorm(x_ref[pl.ds(r0, PROJ_LN_ROWS), :], g_ref[...], b_ref[...])
            xn_ref[pl.ds(r0, PROJ_LN_ROWS), :] = xn.astype(BF16)
            return carry
        lax.fori_loop(0, x_ref.shape[0] // PROJ_LN_ROWS, ln_rows, 0)
        dt_ref[...] = _dot_nt(xn_ref[...], wdt_ref[...])

    out_ref[...] = _dot_nt(xn_ref[...], wt_ref[...])


def _proj_w_row(j):
    t = BF16_ROWS
    assert O_GLU % t == 0 and PROJ_TN % t == 0
    return t * jnp.where(j < PROJ_NA, j * (PROJ_TN // t), O_GLU // t + (j - PROJ_NA) * (PROJ_TN // t))


def _proj_out_tile(j):
    jb = j - PROJ_NA
    n_glu_q = (P_W - P_GLU) // PROJ_TN
    tile_b = jnp.where(jb < n_glu_q, P_GLU // PROJ_TN + jb, P_GATE // PROJ_TN + jb - n_glu_q)
    return jnp.where(j < PROJ_NA, j, tile_b)


def _proj(x2d, ln_g, ln_b, w_in_t):
    m = x2d.shape[0]
    tm = min(PROJ_TM, m)
    grid = (m // tm, PROJ_NA + PROJ_NB)
    return pl.pallas_call(
        _proj_kernel,
        grid=grid,
        in_specs=[
            pl.BlockSpec((tm, D_MODEL), lambda i, j: (i, 0), pipeline_mode=pl.Buffered(1)),
            pl.BlockSpec((1, D_MODEL), lambda i, j: (0, 0)),
            pl.BlockSpec((1, D_MODEL), lambda i, j: (0, 0)),
            pl.BlockSpec((pl.Element(PROJ_TN), pl.Element(D_MODEL)), lambda i, j: (_proj_w_row(j), 0)),
            pl.BlockSpec((LANES, D_MODEL), lambda i, j: (O_DT // LANES, 0)),
        ],
        out_specs=[
            pl.BlockSpec((tm, PROJ_TN), lambda i, j: (i, _proj_out_tile(j))),
            pl.BlockSpec((tm, LANES), lambda i, j: (i, 0)),
        ],
        out_shape=[jax.ShapeDtypeStruct((m, P_W), F32), jax.ShapeDtypeStruct((m, LANES), F32)],
        scratch_shapes=[pltpu.VMEM((tm, D_MODEL), BF16)],
        compiler_params=_cparams(2),
        name="proj",
    )(x2d, ln_g, ln_b, w_in_t, w_in_t)


def _matmul_kernel(x_ref, w_ref, out_ref):
    out_ref[...] = _dot(x_ref[...], w_ref[...])


def _matmul(x2d, w, tn=512):
    m, k = x2d.shape
    n = w.shape[1]
    return pl.pallas_call(
        _matmul_kernel,
        grid=(n // tn,),
        in_specs=[pl.BlockSpec((m, k), lambda j: (0, 0)), pl.BlockSpec((k, tn), lambda j: (0, j))],
        out_specs=pl.BlockSpec((m, tn), lambda j: (0, j)),
        out_shape=jax.ShapeDtypeStruct((m, n), F32),
        compiler_params=_cparams(1),
        name="memkv",
    )(x2d, w)


def _split3(v):
    hi = v.astype(BF16)
    r1 = v - hi.astype(F32)
    mid = r1.astype(BF16)
    lo = (r1 - mid.astype(F32)).astype(BF16)
    return jnp.concatenate([hi, mid, lo], axis=1)


def _expand_heads(pieces, onehot3):
    return jnp.dot(pieces, onehot3, preferred_element_type=F32)


def _conv4(pad_ref, u, hist_ref, hist_out_ref, w_ref, b_ref, first, l_q):
    @pl.when(first)
    def _():
        pad_ref[:, 5:8, :] = hist_ref[...]

    @pl.when(jnp.logical_not(first))
    def _():
        pad_ref[:, 5:8, :] = pad_ref[:, 5 + l_q:8 + l_q, :]

    pad_ref[:, 8:8 + l_q, :] = u
    full = pad_ref[...]
    acc = b_ref[...] + w_ref[SSD_CONV - 1:SSD_CONV, :] * u
    for d in range(1, SSD_CONV):
        shifted = pltpu.roll(full, d, axis=1)[:, 8:8 + l_q, :]
        acc = acc + w_ref[SSD_CONV - 1 - d:SSD_CONV - d, :] * shifted
    hist_out_ref[...] = pad_ref[:, 5 + l_q:8 + l_q, :]
    return _silu(acc)


def _ssd_kernel(z_ref, x_ref, bm_ref, cm_ref, dt_ref, hx_ref, hb_ref, hc_ref, h0_ref,
                wx_ref, wb_ref, wc_ref, bx_ref, bb_ref, bc_ref, dtb_ref, alog_ref, dsk_ref, ng_ref, oh_ref,
                y_ref, ox_ref, ob_ref, oc_ref, h_ref,
                px_ref, pb_ref, pc_ref, yoff_ref, *, n_seq, l_q, g_step, single_chunk):
    rows = n_seq * l_q
    gs = pl.program_id(1)
    first = pl.program_id(2) == 0

    xs = _conv4(px_ref, x_ref[...], hx_ref, ox_ref, wx_ref, bx_ref, first, l_q).reshape(rows, g_step * GROUP_W)
    bm = _conv4(pb_ref, bm_ref[...], hb_ref, ob_ref, wb_ref, bb_ref, first, l_q).reshape(rows, g_step * SSD_STATE)
    cm = _conv4(pc_ref, cm_ref[...], hc_ref, oc_ref, wc_ref, bc_ref, first, l_q).reshape(rows, g_step * SSD_STATE)

    if single_chunk:
        h_in_ref = h0_ref
    else:
        h_in_ref = h_ref

        @pl.when(first)
        def _():
            h_ref[...] = h0_ref[...]

    lane = lax.broadcasted_iota(jnp.int32, (1, LANES), 1)
    dt_raw = dt_ref[...].reshape(rows, LANES) + dtb_ref[...]
    dt = jnp.maximum(dt_raw, 0.0) + jnp.log1p(jnp.exp(-jnp.abs(dt_raw)))
    dt = jnp.where(lane < SSD_HEADS, dt, 0.0)
    a = dt * (-jnp.exp(alog_ref[...]))
    if g_step != SSD_GROUPS:
        src = jnp.bitwise_and(lax.broadcasted_iota(jnp.int32, (3 * LANES, LANES), 0), LANES - 1)
        dst = lax.broadcasted_iota(jnp.int32, (3 * LANES, LANES), 1)
        pick = (src == dst + gs * (g_step * SSD_HG)).astype(BF16)
        dt = _expand_heads(_split3(dt), pick)
        a = _expand_heads(_split3(a), pick)
    rq = lax.broadcasted_iota(jnp.int32, (rows, rows), 0)
    ck = lax.broadcasted_iota(jnp.int32, (rows, rows), 1)
    same = _iota_div((rows, rows), 0, l_q) == _iota_div((rows, rows), 1, l_q)
    causal = jnp.logical_and(same, ck <= rq)
    sums = jnp.dot(jnp.concatenate([causal, same], axis=0).astype(BF16), _split3(a), preferred_element_type=F32)
    sums = sums[:, 0:LANES] + sums[:, LANES:2 * LANES] + sums[:, 2 * LANES:3 * LANES]
    acs = sums[0:rows]
    tot = sums[rows:2 * rows]
    acs_t = acs.T
    seq_of_col = _iota_div((1, rows), 1, l_q)

    dt_p = _split3(dt)
    to_end_p = _split3(jnp.exp(tot - acs))
    from_start_p = _split3(jnp.exp(acs))
    head_of_col = _iota_div((1, GROUP_W), 1, SSD_HEADDIM)
    groups = range(g_step)
    cols_x = [slice(gi * GROUP_W, (gi + 1) * GROUP_W) for gi in groups]
    cols_n = [slice(gi * SSD_STATE, (gi + 1) * SSD_STATE) for gi in groups]

    dt_x = [_expand_heads(dt_p, oh_ref[:, c]) for c in cols_x]
    to_end_x = [_expand_heads(to_end_p, oh_ref[:, c]) for c in cols_x]
    from_start_x = [_expand_heads(from_start_p, oh_ref[:, c]) for c in cols_x]
    cbm = [jnp.where(causal, _dot_nt(cm[:, c], bm[:, c]), 0.0) for c in cols_n]

    m_cat, x_cat, xd_t = [], [], []
    for gi in groups:
        lane0 = gi * SSD_HG
        xdt = xs[:, cols_x[gi]] * dt_x[gi]
        xdt_b = xdt.astype(BF16)
        m_heads, x_heads = [], []
        for h in range(SSD_HG):
            diff = acs[:, lane0 + h:lane0 + h + 1] - acs_t[lane0 + h:lane0 + h + 1, :]
            m_heads.append((cbm[gi] * jnp.exp(jnp.where(causal, diff, 0.0))).astype(BF16))
            x_heads.append(jnp.where(head_of_col == h, xdt_b, jnp.zeros_like(xdt_b)))
        m_cat.append(jnp.concatenate(m_heads, axis=1))
        x_cat.append(jnp.concatenate(x_heads, axis=0))
        xd_t.append((xdt * to_end_x[gi]).T)

    y_in = [jnp.dot(m_cat[gi], x_cat[gi], preferred_element_type=F32) for gi in groups]

    for gi in groups:
        lane0 = gi * SSD_HG
        bg = bm[:, cols_n[gi]]
        cg = cm[:, cols_n[gi]]
        for s in range(n_seq):
            h_old = h_in_ref[s, lane0:lane0 + SSD_HG].reshape(GROUP_W, SSD_STATE)
            yoff_ref[s * l_q:(s + 1) * l_q, cols_x[gi]] = _dot_nt(cg[s * l_q:(s + 1) * l_q, :], h_old)
            xd_s = xd_t[gi] if n_seq == 1 else jnp.where(seq_of_col == s, xd_t[gi], 0.0)
            upd = _dot(xd_s, bg)
            dec = jnp.concatenate(
                [jnp.broadcast_to(jnp.exp(tot[s * l_q:s * l_q + 1, lane0 + h:lane0 + h + 1]),
                                  (SSD_HEADDIM, SSD_STATE)) for h in range(SSD_HG)], axis=0)
            h_ref[s, lane0:lane0 + SSD_HG] = (h_old * dec + upd).reshape(SSD_HG, SSD_HEADDIM, SSD_STATE)

    for gi in groups:
        c = cols_x[gi]
        y = y_in[gi] + yoff_ref[:, c] * from_start_x[gi] + dsk_ref[:, c] * xs[:, c]
        v = y * _silu(z_ref[:, :, c].reshape(rows, GROUP_W))
        v = v * lax.rsqrt(jnp.mean(v * v, axis=-1, keepdims=True) + EPS)
        y_ref[:, c] = (v * ng_ref[:, c]).astype(BF16)


def _head_onehot():
    lane = jnp.arange(3 * LANES)[:, None] % LANES
    head = jnp.arange(D_MODEL)[None, :] // SSD_HEADDIM
    return (lane == head).astype(BF16)


SSD_N_IN = 20
SSD_N_OUT = 5


def _ssd_cast_kernel(*refs, n_cast, **kw):
    ssd_in = refs[:SSD_N_IN]
    cast_in = refs[SSD_N_IN:SSD_N_IN + n_cast]
    ssd_out = refs[SSD_N_IN + n_cast:SSD_N_IN + n_cast + SSD_N_OUT]
    cast_out = refs[SSD_N_IN + n_cast + SSD_N_OUT:SSD_N_IN + 2 * n_cast + SSD_N_OUT]
    scratch = refs[SSD_N_IN + 2 * n_cast + SSD_N_OUT:]
    for src, dst in zip(cast_in, cast_out):
        dst[...] = src[...].astype(BF16)
    _ssd_kernel(*ssd_in, *ssd_out, *scratch, **kw)


def _ssd(proj3, dt3, hist, h0, p, *, n_seq, l_q, g_step, cast_weights=()):
    nb, l_seq, _ = proj3.shape
    wx = g_step * GROUP_W
    wn = g_step * SSD_STATE
    grid = (nb // n_seq, SSD_GROUPS // g_step, l_seq // l_q)
    ob = (O_XBC + D_MODEL) // wn
    oc = ob + (SSD_GROUPS * SSD_STATE) // wn
    cb = D_MODEL // wn
    cc = cb + (SSD_GROUPS * SSD_STATE) // wn

    def rows_spec(width, first_block):
        return pl.BlockSpec((n_seq, l_q, width), lambda n, g, c: (n, c, first_block + g))

    def hist_spec(width, first_block):
        return pl.BlockSpec((n_seq, SSD_CONV - 1, width), lambda n, g, c: (n, 0, first_block + g))

    def par_spec(r, width, first_block):
        return pl.BlockSpec((r, width), lambda n, g, c: (0, first_block + g))

    state_spec = pl.BlockSpec((n_seq, g_step * SSD_HG, SSD_HEADDIM, SSD_STATE), lambda n, g, c: (n, g, 0, 0))
    full_lane = pl.BlockSpec((1, LANES), lambda n, g, c: (0, 0))
    n_steps = grid[0] * grid[1] * grid[2]
    def cast_specs():
        specs = []
        for w in cast_weights:
            assert w.shape[0] % (n_steps * BF16_ROWS) == 0, w.shape
            specs.append(pl.BlockSpec((w.shape[0] // n_steps, w.shape[1]),
                                      lambda n, g, c: ((n * grid[1] + g) * grid[2] + c, 0)))
        return specs
    kern = functools.partial(_ssd_cast_kernel, n_cast=len(cast_weights), n_seq=n_seq, l_q=l_q, g_step=g_step,
                             single_chunk=grid[2] == 1)
    return pl.pallas_call(
        kern,
        grid=grid,
        in_specs=[
            rows_spec(wx, 0), rows_spec(wx, O_XBC // wx), rows_spec(wn, ob), rows_spec(wn, oc),
            pl.BlockSpec((n_seq, l_q, LANES), lambda n, g, c: (n, c, 0)),
            hist_spec(wx, 0), hist_spec(wn, cb), hist_spec(wn, cc),
            state_spec,
            par_spec(SSD_CONV, wx, 0), par_spec(SSD_CONV, wn, cb), par_spec(SSD_CONV, wn, cc),
            par_spec(1, wx, 0), par_spec(1, wn, cb), par_spec(1, wn, cc),
            full_lane, full_lane, par_spec(1, wx, 0), par_spec(1, wx, 0),
            pl.BlockSpec((3 * LANES, wx), lambda n, g, c: (0, 0)),
        ] + cast_specs(),
        out_specs=[
            pl.BlockSpec((SSD_ROWS, wx), lambda n, g, c: (n * grid[2] + c, g)),
            hist_spec(wx, 0), hist_spec(wn, 0), hist_spec(wn, 0),
            state_spec,
        ] + cast_specs(),
        out_shape=[
            jax.ShapeDtypeStruct((nb * l_seq, D_MODEL), BF16),
            jax.ShapeDtypeStruct((nb, SSD_CONV - 1, D_MODEL), F32),
            jax.ShapeDtypeStruct((nb, SSD_CONV - 1, SSD_GROUPS * SSD_STATE), F32),
            jax.ShapeDtypeStruct((nb, SSD_CONV - 1, SSD_GROUPS * SSD_STATE), F32),
            jax.ShapeDtypeStruct((nb, SSD_HEADS, SSD_HEADDIM, SSD_STATE), F32),
        ] + [jax.ShapeDtypeStruct(w.shape, BF16) for w in cast_weights],
        scratch_shapes=[
            pltpu.VMEM((n_seq, 8 + l_q, wx), F32),
            pltpu.VMEM((n_seq, 8 + l_q, wn), F32),
            pltpu.VMEM((n_seq, 8 + l_q, wn), F32),
            pltpu.VMEM((SSD_ROWS, wx), F32),
        ],
        compiler_params=_cparams(3),
        name="ssd",
    )(proj3, proj3, proj3, proj3, dt3, hist, hist, hist, h0,
      p["conv_w"], p["conv_w"], p["conv_w"], p["conv_b"], p["conv_b"], p["conv_b"],
      p["dt_bias"], p["a_log"], p["d_skip"], p["norm_g"], p["head_onehot"], *cast_weights)


CONF_HIST = CONF_KERNEL - 1
CONF_PAD0 = 32


SUBLANES = 8
CONF_NORM_ROWS = 128


def _conf_kernel(a_ref, b_ref, hist_ref, w_ref, bias_ref, g_ref, beta_ref, c_ref, hist_out_ref,
                 pad_ref, ph_ref, cf_ref, wrep_ref, *, n_seq, l_t, rb):
    first = pl.program_id(1) == 0
    lo = CONF_PAD0 - CONF_HIST
    rows = CONF_PAD0 + l_t

    @pl.when(first)
    def _():
        pad_ref[:, lo:CONF_PAD0, :] = hist_ref[...]

    @pl.when(jnp.logical_not(first))
    def _():
        pad_ref[:, lo:CONF_PAD0, :] = pad_ref[:, lo + l_t:CONF_PAD0 + l_t, :]

    pad_ref[:, CONF_PAD0:CONF_PAD0 + l_t, :] = a_ref[...] * _sigmoid(b_ref[...])
    hist_out_ref[...] = pad_ref[:, lo + l_t:CONF_PAD0 + l_t, :]

    full = pad_ref[...]
    for p in range(1, SUBLANES):
        ph_ref[p - 1] = pltpu.roll(full, rows - p, axis=1)

    for j in range(CONF_KERNEL):
        wrep_ref[j] = jnp.broadcast_to(w_ref[j:j + 1, :], (SUBLANES, D_CONV))

    n_rb = l_t // rb

    def conv_block(i, carry):
        s = i // n_rb
        r0 = (i % n_rb) * rb
        n_tiles = rb // SUBLANES
        out_row = pl.multiple_of(s * l_t + r0, SUBLANES)
        for c0 in range(0, D_CONV, LANES):
            lanes = slice(c0, c0 + LANES)
            acc = [jnp.broadcast_to(bias_ref[:, lanes], (SUBLANES, LANES))] * n_tiles
            for j in range(CONF_KERNEL):
                p = (lo + j) % SUBLANES
                src = pad_ref if p == 0 else ph_ref.at[p - 1]
                wj = wrep_ref[j, :, lanes]
                for k in range(n_tiles):
                    start = pl.multiple_of(r0 + (lo + j - p) + k * SUBLANES, SUBLANES)
                    acc[k] = acc[k] + wj * src[s, pl.ds(start, SUBLANES), lanes]
            cf_ref[pl.ds(out_row, rb), lanes] = jnp.concatenate(acc, axis=0)
        return carry

    lax.fori_loop(0, n_seq * n_rb, conv_block, 0)

    nb_rows = min(CONF_NORM_ROWS, n_seq * l_t)

    def norm_block(i, carry):
        r0 = pl.multiple_of(i * nb_rows, nb_rows)
        v = _silu(_layernorm(cf_ref[pl.ds(r0, nb_rows), :], g_ref[...], beta_ref[...]))
        c_ref[pl.ds(r0, nb_rows), :] = v.astype(BF16)
        return carry

    lax.fori_loop(0, (n_seq * l_t) // nb_rows, norm_block, 0)


def _conf(proj3, hist, p, *, n_seq, l_t):
    nb, l_seq, _ = proj3.shape
    rb = min(32, l_t)
    grid = (nb // n_seq, l_seq // l_t)
    par = lambda r: pl.BlockSpec((r, D_CONV), lambda n, t: (0, 0))
    hist_spec = pl.BlockSpec((n_seq, CONF_HIST, D_CONV), lambda n, t: (n, 0, 0))
    kern = functools.partial(_conf_kernel, n_seq=n_seq, l_t=l_t, rb=rb)
    return pl.pallas_call(
        kern,
        grid=grid,
        in_specs=[
            pl.BlockSpec((n_seq, l_t, D_CONV), lambda n, t: (n, t, P_GLU // D_CONV)),
            pl.BlockSpec((n_seq, l_t, D_CONV), lambda n, t: (n, t, P_GLU // D_CONV + 1)),
            hist_spec, par(CONF_KERNEL), par(1), par(1), par(1),
        ],
        out_specs=[pl.BlockSpec((n_seq * l_t, D_CONV), lambda n, t: (n * grid[1] + t, 0)), hist_spec],
        out_shape=[jax.ShapeDtypeStruct((nb * l_seq, D_CONV), BF16),
                   jax.ShapeDtypeStruct((nb, CONF_HIST, D_CONV), F32)],
        scratch_shapes=[pltpu.VMEM((n_seq, CONF_PAD0 + l_t, D_CONV), F32),
                        pltpu.VMEM((SUBLANES - 1, n_seq, CONF_PAD0 + l_t, D_CONV), F32),
                        pltpu.VMEM((n_seq * l_t, D_CONV), F32),
                        pltpu.VMEM((CONF_KERNEL, SUBLANES, D_CONV), F32)],
        compiler_params=_cparams(2),
        name="conf",
    )(proj3, proj3, hist, p["conf_w"], p["conf_b"], p["conf_ln_g"], p["conf_ln_b"])


def _attn_kernel(q_ref, k_ref, v_ref, o_ref, *, n_seq):
    scale = MEM_HEAD_DIM ** -0.5
    few_rows = q_ref.shape[1] < LANES
    for h in range(MEM_HEADS):
        cols = slice(h * MEM_HEAD_DIM, (h + 1) * MEM_HEAD_DIM)
        if few_rows:
            scores = [_dot_nt(k_ref[s, :, cols], q_ref[s, :, cols]) * scale for s in range(n_seq)]
            axis = 0
        else:
            scores = [_dot_nt(q_ref[s, :, cols], k_ref[s, :, cols]) * scale for s in range(n_seq)]
            axis = 1
        probs = []
        for sc in scores:
            e = jnp.exp(sc - jnp.max(sc, axis=axis, keepdims=True))
            probs.append((e / jnp.sum(e, axis=axis, keepdims=True)).astype(BF16))
        if few_rows:
            outs = [lax.dot_general(pr, v_ref[s, :, cols].astype(BF16), (((0,), (0,)), ((), ())),
                                    preferred_element_type=F32) for s, pr in enumerate(probs)]
        else:
            outs = [_dot(pr, v_ref[s, :, cols]) for s, pr in enumerate(probs)]
        o_ref[:, cols] = jnp.concatenate(outs, axis=0).astype(BF16)


def _attn(proj3, mem_k, mem_v, *, n_seq, l_t):
    nb, l_seq, _ = proj3.shape
    grid = (nb // n_seq, l_seq // l_t)
    kv_spec = pl.BlockSpec((n_seq, MEM_LEN, ATT_W), lambda n, t: (n, 0, 0))
    return pl.pallas_call(
        functools.partial(_attn_kernel, n_seq=n_seq),
        grid=grid,
        in_specs=[pl.BlockSpec((n_seq, l_t, ATT_W), lambda n, t: (n, t, P_Q // ATT_W)), kv_spec, kv_spec],
        out_specs=pl.BlockSpec((n_seq * l_t, ATT_W), lambda n, t: (n * grid[1] + t, 0)),
        out_shape=jax.ShapeDtypeStruct((nb * l_seq, ATT_W), BF16),
        compiler_params=_cparams(2),
        name="attn",
    )(proj3, mem_k, mem_v)


KV_HALVES = MEM_HEAD_DIM // LANES
KV_ROWS = MEM_LEN * MEM_HEADS * KV_HALVES


def _attn_rows_kernel(q_ref, k_ref, v_ref, o_ref, *, n_seq, l_t):
    scale = MEM_HEAD_DIM ** -0.5
    nq = MEM_HEADS * l_t
    col = lax.broadcasted_iota(jnp.int32, (nq, KV_ROWS), 1)
    valid = jnp.bitwise_and(col, MEM_HEADS * KV_HALVES - 1) == _iota_div((nq, KV_ROWS), 0, l_t)
    parts = []
    for s in range(n_seq):
        q2 = jnp.concatenate([q_ref[s, :, (h * KV_HALVES + c) * LANES:(h * KV_HALVES + c + 1) * LANES]
                              for c in range(KV_HALVES) for h in range(MEM_HEADS)], axis=0)
        parts.append(_dot_nt(q2, k_ref[s]))
    probs = []
    for part in parts:
        sc = (part[0:nq] + pltpu.roll(part[nq:2 * nq], KV_ROWS - MEM_HEADS, axis=1)) * scale
        sc = jnp.where(valid, sc, -jnp.inf)
        e = jnp.exp(sc - jnp.max(sc, axis=1, keepdims=True))
        pr = e / jnp.sum(e, axis=1, keepdims=True)
        probs.append(jnp.concatenate([pr, pltpu.roll(pr, MEM_HEADS, axis=1)], axis=0).astype(BF16))
    outs = [_dot(p2, v_ref[s]) for s, p2 in enumerate(probs)]
    for c in range(KV_HALVES):
        for h in range(MEM_HEADS):
            r0 = (c * MEM_HEADS + h) * l_t
            piece = jnp.concatenate([o[r0:r0 + l_t] for o in outs], axis=0)
            o_ref[:, (h * KV_HALVES + c) * LANES:(h * KV_HALVES + c + 1) * LANES] = piece.astype(BF16)


def _attn_rows(proj3, k_rows, v_rows, *, n_seq):
    nb, l_t, _ = proj3.shape
    assert l_t % SUBLANES == 0 and KV_HALVES == 2
    kv_spec = pl.BlockSpec((n_seq, KV_ROWS, LANES), lambda n: (n, 0, 0))
    return pl.pallas_call(
        functools.partial(_attn_rows_kernel, n_seq=n_seq, l_t=l_t),
        grid=(nb // n_seq,),
        in_specs=[pl.BlockSpec((n_seq, l_t, ATT_W), lambda n: (n, 0, P_Q // ATT_W)), kv_spec, kv_spec],
        out_specs=pl.BlockSpec((n_seq * l_t, ATT_W), lambda n: (n, 0)),
        out_shape=jax.ShapeDtypeStruct((nb * l_t, ATT_W), BF16),
        compiler_params=_cparams(1),
        name="attn_rows",
    )(proj3, k_rows, v_rows)


def _interleave_kv(kv):
    b = kv.shape[0]
    kv = kv.reshape(b, MEM_LEN, MEM_HEADS, KV_HALVES, LANES)
    return jnp.transpose(kv, (0, 1, 3, 2, 4)).reshape(b, KV_ROWS, LANES)


MERGE_TM = 256


def _merge_kernel(x_ref, y_ref, c_ref, o_ref, g0_ref, g1_ref, g2_ref, bg_ref, wssd_ref, wconf_ref, wmem_ref,
                  wout_ref, lg_ref, lb_ref, l1g_ref, l1b_ref, h1_ref):
    merged = _sigmoid(g0_ref[...] + bg_ref[0:1, :]) * _dot(y_ref[...], wssd_ref[...])
    merged = merged + _sigmoid(g1_ref[...] + bg_ref[1:2, :]) * _dot(c_ref[...], wconf_ref[...])
    merged = merged + _sigmoid(g2_ref[...] + bg_ref[2:3, :]) * _dot(o_ref[...], wmem_ref[...])
    mix = _dot(merged, wout_ref[...])
    h = _layernorm(x_ref[...], lg_ref[...], lb_ref[...])
    h1_ref[...] = _layernorm(ALPHA * h + mix, l1g_ref[...], l1b_ref[...])


def _merge(x2d, y2d, c2d, o2d, proj2d, p, wb):
    m = x2d.shape[0]
    tm = MERGE_TM
    rows = lambda w, blk: pl.BlockSpec((tm, w), lambda i: (i, blk))
    whole = lambda a: pl.BlockSpec(a.shape, lambda i: (0, 0), pipeline_mode=pl.Buffered(1))
    vec = pl.BlockSpec((1, D_MODEL), lambda i: (0, 0))
    g_blk = P_GATE // D_MODEL
    return pl.pallas_call(
        _merge_kernel,
        grid=(m // tm,),
        in_specs=[
            rows(D_MODEL, 0), rows(D_MODEL, 0), rows(D_CONV, 0), rows(ATT_W, 0),
            rows(D_MODEL, g_blk), rows(D_MODEL, g_blk + 1), rows(D_MODEL, g_blk + 2),
            pl.BlockSpec((3, D_MODEL), lambda i: (0, 0)),
            whole(wb["w_br_ssd"]), whole(wb["w_br_conf"]), whole(wb["w_br_mem"]), whole(wb["w_out"]),
            vec, vec, vec, vec,
        ],
        out_specs=rows(D_MODEL, 0),
        out_shape=jax.ShapeDtypeStruct((m, D_MODEL), F32),
        compiler_params=_cparams(1),
        name="merge",
    )(x2d, y2d, c2d, o2d, proj2d, proj2d, proj2d, p["b_gate"],
      wb["w_br_ssd"], wb["w_br_conf"], wb["w_br_mem"], wb["w_out"],
      p["ln_in_g"], p["ln_in_b"], p["ln1_g"], p["ln1_b"])


MLP_TM = 1024
MLP_TF = 512


def _mlp_kernel(h_ref, w1_ref, b1_ref, w2_ref, b2_ref, g_ref, b_ref, out_ref, hb_ref):
    f = pl.program_id(1)

    @pl.when(f == 0)
    def _():
        hb_ref[...] = h_ref[...].astype(BF16)
        out_ref[...] = jnp.zeros_like(out_ref)

    a = jnp.maximum(jnp.dot(hb_ref[...], w1_ref[...], preferred_element_type=F32) + b1_ref[...], 0.0)
    out_ref[...] += _dot(a * a, w2_ref[...])

    @pl.when(f == pl.num_programs(1) - 1)
    def _():
        out_ref[...] = _layernorm(ALPHA * h_ref[...] + (out_ref[...] + b2_ref[...]), g_ref[...], b_ref[...])


def _mlp(h2d, p, wb):
    m = h2d.shape[0]
    tm, tf = MLP_TM, MLP_TF
    vec = pl.BlockSpec((1, D_MODEL), lambda i, f: (0, 0))
    return pl.pallas_call(
        _mlp_kernel,
        grid=(m // tm, D_FF // tf),
        in_specs=[
            pl.BlockSpec((tm, D_MODEL), lambda i, f: (i, 0), pipeline_mode=pl.Buffered(1)),
            pl.BlockSpec((D_MODEL, tf), lambda i, f: (0, f)),
            pl.BlockSpec((1, tf), lambda i, f: (0, f)),
            pl.BlockSpec((tf, D_MODEL), lambda i, f: (f, 0)),
            vec, vec, vec,
        ],
        out_specs=pl.BlockSpec((tm, D_MODEL), lambda i, f: (i, 0)),
        out_shape=jax.ShapeDtypeStruct((m, D_MODEL), F32),
        scratch_shapes=[pltpu.VMEM((tm, D_MODEL), BF16)],
        compiler_params=_cparams(2),
        name="mlp",
    )(h2d, wb["w_ff1"], p["b_ff1"], wb["w_ff2"], p["b_ff2"], p["ln2_g"], p["ln2_b"])


CAST_WEIGHTS = ("w_ff1", "w_ff2", "w_br_ssd", "w_out", "w_br_conf", "w_br_mem")


def _trunk(x, ssd_hist, ssm0, conf_hist, mem_k, mem_v, p, wb, *, ssd_cfg, conf_cfg, attn_fn):
    nb, l_seq, _ = x.shape
    x2d = x.reshape(nb * l_seq, D_MODEL)
    proj2d, dt2d = _proj(x2d, p["ln_in_g"], p["ln_in_b"], p["w_in_t"])
    proj3 = proj2d.reshape(nb, l_seq, P_W)
    dt3 = dt2d.reshape(nb, l_seq, LANES)
    if wb is None:
        y, hx, hb, hc, ssm1, *cast = _ssd(proj3, dt3, ssd_hist, ssm0, p, cast_weights=[p[n] for n in CAST_WEIGHTS],
                                          **ssd_cfg)
        wb = dict(zip(CAST_WEIGHTS, cast))
    else:
        y, hx, hb, hc, ssm1 = _ssd(proj3, dt3, ssd_hist, ssm0, p, **ssd_cfg)
    c, conf_hist1 = _conf(proj3, conf_hist, p, **conf_cfg)
    o = attn_fn(proj3, mem_k, mem_v)
    h1 = _merge(x2d, y, c, o, proj2d, p, wb)
    out = _mlp(h1, p, wb).reshape(nb, l_seq, D_MODEL)
    return out, ssm1, jnp.concatenate([hx, hb, hc], axis=-1), conf_hist1, wb


def kernel(x_prompt, x_sample, mem_prompt, state_ssm, state_ssd_conv, state_conf_conv, cache_mem_k, cache_mem_v, ln_in_g, ln_in_b, w_in, b_gate, ssd_conv_w, ssd_conv_b, ssd_dt_bias, ssd_a_log, ssd_d, ssd_norm_g, conf_dw_w, conf_dw_b, conf_ln_g, conf_ln_b, w_mem_k, w_mem_v, w_br_ssd, w_br_conf, w_br_mem, w_out, ln1_g, ln1_b, w_ff1, b_ff1, w_ff2, b_ff2, ln2_g, ln2_b):
    layer = 0
    row = lambda v: v.reshape(1, -1)
    lane_pad = lambda v: jnp.pad(v.reshape(1, -1), ((0, 0), (0, LANES - v.size)))
    p = {
        "ln_in_g": row(ln_in_g), "ln_in_b": row(ln_in_b),
        "w_in_t": jnp.swapaxes(w_in[layer], 0, 1),
        "b_gate": b_gate[layer],
        "conv_w": ssd_conv_w[layer], "conv_b": row(ssd_conv_b[layer]),
        "dt_bias": lane_pad(ssd_dt_bias[layer]), "a_log": lane_pad(ssd_a_log[layer]),
        "d_skip": row(jnp.repeat(ssd_d[layer], SSD_HEADDIM)), "norm_g": row(ssd_norm_g[layer]),
        "head_onehot": _head_onehot(),
        "conf_w": conf_dw_w[layer], "conf_b": row(conf_dw_b[layer]),
        "conf_ln_g": row(conf_ln_g[layer]), "conf_ln_b": row(conf_ln_b[layer]),
        "w_br_ssd": w_br_ssd[layer], "w_br_conf": w_br_conf[layer],
        "w_br_mem": w_br_mem[layer], "w_out": w_out[layer],
        "ln1_g": row(ln1_g[layer]), "ln1_b": row(ln1_b[layer]),
        "w_ff1": w_ff1[layer], "b_ff1": row(b_ff1[layer]),
        "w_ff2": w_ff2[layer], "b_ff2": row(b_ff2[layer]),
        "ln2_g": row(ln2_g[layer]), "ln2_b": row(ln2_b[layer]),
    }
    n_p, l_p, _ = x_prompt.shape
    n_s, l_s, _ = x_sample.shape

    mem2d = mem_prompt.reshape(n_p * MEM_LEN, D_MODEL)
    p_mem_k = _matmul(mem2d, w_mem_k[layer]).reshape(n_p, MEM_LEN, ATT_W)
    p_mem_v = _matmul(mem2d, w_mem_v[layer]).reshape(n_p, MEM_LEN, ATT_W)
    y_prompt, p_ssm, p_ssd_conv, p_cc, wb = _trunk(
        x_prompt,
        jnp.zeros((n_p, SSD_CONV - 1, CONV_DIM), F32),
        jnp.zeros((n_p, SSD_HEADS, SSD_HEADDIM, SSD_STATE), F32),
        jnp.zeros((n_p, CONF_HIST, D_CONV), F32),
        p_mem_k, p_mem_v, p, None,
        ssd_cfg=dict(n_seq=1, l_q=SSD_ROWS, g_step=SSD_GROUPS),
        conf_cfg=dict(n_seq=1, l_t=512),
        attn_fn=functools.partial(_attn, n_seq=1, l_t=512))

    y_sample, s_ssm, s_ssd_conv, s_cc, _ = _trunk(
        x_sample, state_ssd_conv[layer], state_ssm[layer], state_conf_conv[layer],
        _interleave_kv(cache_mem_k[layer]), _interleave_kv(cache_mem_v[layer]), p, wb,
        ssd_cfg=dict(n_seq=SSD_ROWS // l_s, l_q=l_s, g_step=4),
        conf_cfg=dict(n_seq=8, l_t=l_s),
        attn_fn=functools.partial(_attn_rows, n_seq=8))

    kv_shape = (DEPTH, n_p, MEM_LEN, MEM_HEADS, MEM_HEAD_DIM)
    return (y_prompt, y_sample, p_ssm[None], p_ssd_conv[None], p_cc[None],
            p_mem_k.reshape(kv_shape), p_mem_v.reshape(kv_shape),
            s_ssm[None], s_ssd_conv[None], s_cc[None])
```

```python
import functools
import math

import jax
import jax.numpy as jnp
from jax import lax
from jax.experimental import pallas as pl
from jax.experimental.pallas import tpu as pltpu

F32 = jnp.float32
BF16 = jnp.bfloat16

D_MODEL = 2048
SSD_HEADDIM = 64
SSD_HEADS = 32
SSD_GROUPS = 8
SSD_HG = 4
SSD_STATE = 128
SSD_CONV = 4
CONV_DIM = 4096
D_CONV = 1024
CONF_KERNEL = 31
MEM_LEN = 256
MEM_HEADS = 4
MEM_HEAD_DIM = 256
ATT_W = 1024
D_FF = 8192
DEPTH = 1
ALPHA = (2.0 * DEPTH) ** 0.25
EPS = 1e-5

O_XBC = 2048
O_DT = 6144
O_GLU = 6176
P_GATE = 6144
P_GLU = 12288
P_Q = 14336
P_W = 15360

GROUP_W = SSD_HG * SSD_HEADDIM
SSD_ROWS = 128
LANES = 128
BF16_ROWS = 16
VMEM_LIMIT = 56 * 1024 * 1024


def _cparams(n_axes):
    return pltpu.CompilerParams(dimension_semantics=("arbitrary",) * n_axes, vmem_limit_bytes=VMEM_LIMIT)


def _layernorm(x, g, b):
    mu = jnp.mean(x, axis=-1, keepdims=True)
    xc = x - mu
    var = jnp.mean(xc * xc, axis=-1, keepdims=True)
    return xc * lax.rsqrt(var + EPS) * g + b


def _sigmoid(x):
    return 0.5 * jnp.tanh(0.5 * x) + 0.5


def _silu(x):
    return x * _sigmoid(x)


def _dot(a, b):
    return jnp.dot(a.astype(BF16), b.astype(BF16), preferred_element_type=F32)


def _dot_nt(a, b):
    return lax.dot_general(a.astype(BF16), b.astype(BF16), (((1,), (1,)), ((), ())), preferred_element_type=F32)


def _iota_div(shape, axis, divisor):
    shift = divisor.bit_length() - 1
    assert 1 << shift == divisor
    return lax.shift_right_logical(lax.broadcasted_iota(jnp.int32, shape, axis), shift)


PROJ_TM = 1024
PROJ_TN = 1024
PROJ_LN_ROWS = 256
PROJ_NA = O_DT // PROJ_TN
PROJ_NB = (P_W - O_DT) // PROJ_TN


def _proj_kernel(x_ref, g_ref, b_ref, wt_ref, wdt_ref, out_ref, dt_ref, xn_ref):
    @pl.when(pl.program_id(1) == 0)
    def _():
        def ln_rows(r, carry):
            r0 = pl.multiple_of(r * PROJ_LN_ROWS, PROJ_LN_ROWS)
            xn = _layernorm(x_ref[pl.ds(r0, PROJ_LN_ROWS), :], g_ref[...], b_ref[...])
            xn_ref[pl.ds(r0, PROJ_LN_ROWS), :] = xn.astype(BF16)
            return carry
        lax.fori_loop(0, x_ref.shape[0] // PROJ_LN_ROWS, ln_rows, 0)
        dt_ref[...] = _dot_nt(xn_ref[...], wdt_ref[...])

    out_ref[...] = _dot_nt(xn_ref[...], wt_ref[...])


def _proj_w_row(j):
    t = BF16_ROWS
    assert O_GLU % t == 0 and PROJ_TN % t == 0
    return t * jnp.where(j < PROJ_NA, j * (PROJ_TN // t), O_GLU // t + (j - PROJ_NA) * (PROJ_TN // t))


def _proj_out_tile(j):
    jb = j - PROJ_NA
    n_glu_q = (P_W - P_GLU) // PROJ_TN
    tile_b = jnp.where(jb < n_glu_q, P_GLU // PROJ_TN + jb, P_GATE // PROJ_TN + jb - n_glu_q)
    return jnp.where(j < PROJ_NA, j, tile_b)


def _proj(x2d, ln_g, ln_b, w_in_t):
    m = x2d.shape[0]
    tm = min(PROJ_TM, m)
    grid = (m // tm, PROJ_NA + PROJ_NB)
    return pl.pallas_call(
        _proj_kernel,
        grid=grid,
        in_specs=[
            pl.BlockSpec((tm, D_MODEL), lambda i, j: (i, 0), pipeline_mode=pl.Buffered(1)),
            pl.BlockSpec((1, D_MODEL), lambda i, j: (0, 0)),
            pl.BlockSpec((1, D_MODEL), lambda i, j: (0, 0)),
            pl.BlockSpec((pl.Element(PROJ_TN), pl.Element(D_MODEL)), lambda i, j: (_proj_w_row(j), 0)),
            pl.BlockSpec((LANES, D_MODEL), lambda i, j: (O_DT // LANES, 0)),
        ],
        out_specs=[
            pl.BlockSpec((tm, PROJ_TN), lambda i, j: (i, _proj_out_tile(j))),
            pl.BlockSpec((tm, LANES), lambda i, j: (i, 0)),
        ],
        out_shape=[jax.ShapeDtypeStruct((m, P_W), F32), jax.ShapeDtypeStruct((m, LANES), F32)],
        scratch_shapes=[pltpu.VMEM((tm, D_MODEL), BF16)],
        compiler_params=_cparams(2),
        name="proj",
    )(x2d, ln_g, ln_b, w_in_t, w_in_t)


def _matmul_kernel(x_ref, w_ref, out_ref):
    out_ref[...] = _dot(x_ref[...], w_ref[...])


def _matmul(x2d, w, tn=512):
    m, k = x2d.shape
    n = w.shape[1]
    return pl.pallas_call(
        _matmul_kernel,
        grid=(n // tn,),
        in_specs=[pl.BlockSpec((m, k), lambda j: (0, 0)), pl.BlockSpec((k, tn), lambda j: (0, j))],
        out_specs=pl.BlockSpec((m, tn), lambda j: (0, j)),
        out_shape=jax.ShapeDtypeStruct((m, n), F32),
        compiler_params=_cparams(1),
        name="memkv",
    )(x2d, w)


def _split3(v):
    hi = v.astype(BF16)
    r1 = v - hi.astype(F32)
    mid = r1.astype(BF16)
    lo = (r1 - mid.astype(F32)).astype(BF16)
    return jnp.concatenate([hi, mid, lo], axis=1)


def _expand_heads(pieces, onehot3):
    return jnp.dot(pieces, onehot3, preferred_element_type=F32)


def _conv4(pad_ref, u, hist_out_ref, w_ref, b_ref, l_q):
    pad_ref[:, 8:8 + l_q, :] = u
    full = pad_ref[...]
    acc = b_ref[...] + w_ref[SSD_CONV - 1:SSD_CONV, :] * u
    for d in range(1, SSD_CONV):
        shifted = pltpu.roll(full, d, axis=1)[:, 8:8 + l_q, :]
        acc = acc + w_ref[SSD_CONV - 1 - d:SSD_CONV - d, :] * shifted
    last3 = pad_ref[:, 5 + l_q:8 + l_q, :]
    hist_out_ref[...] = last3
    pad_ref[:, 5:8, :] = last3
    return _silu(acc)


def _ssd_kernel(z_ref, x_ref, bm_ref, cm_ref, dt_ref, hx_ref, hb_ref, hc_ref, h0_ref,
                wx_ref, wb_ref, wc_ref, bx_ref, bb_ref, bc_ref, dtb_ref, alog_ref, dsk_ref, ng_ref, oh_ref,
                y_ref, ox_ref, ob_ref, oc_ref, h_ref,
                px_ref, pb_ref, pc_ref, yoff_ref, *, n_seq, l_q, g_step, single_chunk):
    rows = n_seq * l_q
    gs = pl.program_id(1)
    first = pl.program_id(2) == 0

    h_in_ref = h0_ref if single_chunk else h_ref

    @pl.when(first)
    def _():
        px_ref[:, 5:8, :] = hx_ref[...]
        pb_ref[:, 5:8, :] = hb_ref[...]
        pc_ref[:, 5:8, :] = hc_ref[...]
        if not single_chunk:
            h_ref[...] = h0_ref[...]

    xs = _conv4(px_ref, x_ref[...], ox_ref, wx_ref, bx_ref, l_q).reshape(rows, g_step * GROUP_W)
    bm = _conv4(pb_ref, bm_ref[...], ob_ref, wb_ref, bb_ref, l_q).reshape(rows, g_step * SSD_STATE)
    cm = _conv4(pc_ref, cm_ref[...], oc_ref, wc_ref, bc_ref, l_q).reshape(rows, g_step * SSD_STATE)

    lane = lax.broadcasted_iota(jnp.int32, (1, LANES), 1)
    dt_raw = dt_ref[...].reshape(rows, LANES) + dtb_ref[...]
    dt = jnp.maximum(dt_raw, 0.0) + jnp.log1p(jnp.exp(-jnp.abs(dt_raw)))
    dt = jnp.where(lane < SSD_HEADS, dt, 0.0)
    a = dt * (-jnp.exp(alog_ref[...]))
    if g_step != SSD_GROUPS:
        src = jnp.bitwise_and(lax.broadcasted_iota(jnp.int32, (3 * LANES, LANES), 0), LANES - 1)
        dst = lax.broadcasted_iota(jnp.int32, (3 * LANES, LANES), 1)
        pick = (src == dst + gs * (g_step * SSD_HG)).astype(BF16)
        dt = _expand_heads(_split3(dt), pick)
        a = _expand_heads(_split3(a), pick)
    rq = lax.broadcasted_iota(jnp.int32, (rows, rows), 0)
    ck = lax.broadcasted_iota(jnp.int32, (rows, rows), 1)
    same = _iota_div((rows, rows), 0, l_q) == _iota_div((rows, rows), 1, l_q)
    causal = jnp.logical_and(same, ck <= rq)
    sums = jnp.dot(jnp.concatenate([causal, same], axis=0).astype(BF16), _split3(a), preferred_element_type=F32)
    sums = sums[:, 0:LANES] + sums[:, LANES:2 * LANES] + sums[:, 2 * LANES:3 * LANES]
    acs = sums[0:rows]
    tot = sums[rows:2 * rows]
    acs_t = acs.T
    seq_of_col = _iota_div((1, rows), 1, l_q)

    dt_p = _split3(dt)
    to_end_p = _split3(jnp.exp(tot - acs))
    from_start_p = _split3(jnp.exp(acs))
    head_of_col = _iota_div((1, GROUP_W), 1, SSD_HEADDIM)
    groups = range(g_step)
    cols_x = [slice(gi * GROUP_W, (gi + 1) * GROUP_W) for gi in groups]
    cols_n = [slice(gi * SSD_STATE, (gi + 1) * SSD_STATE) for gi in groups]

    dt_x = [_expand_heads(dt_p, oh_ref[:, c]) for c in cols_x]
    to_end_x = [_expand_heads(to_end_p, oh_ref[:, c]) for c in cols_x]
    from_start_x = [_expand_heads(from_start_p, oh_ref[:, c]) for c in cols_x]
    cbm = [jnp.where(causal, _dot_nt(cm[:, c], bm[:, c]), 0.0) for c in cols_n]

    m_cat, x_cat, xd_t = [], [], []
    for gi in groups:
        lane0 = gi * SSD_HG
        xdt = xs[:, cols_x[gi]] * dt_x[gi]
        xdt_b = xdt.astype(BF16)
        m_heads, x_heads = [], []
        for h in range(SSD_HG):
            diff = acs[:, lane0 + h:lane0 + h + 1] - acs_t[lane0 + h:lane0 + h + 1, :]
            m_heads.append((cbm[gi] * jnp.exp(jnp.where(causal, diff, 0.0))).astype(BF16))
            x_heads.append(jnp.where(head_of_col == h, xdt_b, jnp.zeros_like(xdt_b)))
        m_cat.append(jnp.concatenate(m_heads, axis=1))
        x_cat.append(jnp.concatenate(x_heads, axis=0))
        xd_t.append((xdt * to_end_x[gi]).T)

    y_in = [jnp.dot(m_cat[gi], x_cat[gi], preferred_element_type=F32) for gi in groups]

    for gi in groups:
        lane0 = gi * SSD_HG
        bg = bm[:, cols_n[gi]]
        cg = cm[:, cols_n[gi]]
        for s in range(n_seq):
            h_old = h_in_ref[s, lane0:lane0 + SSD_HG].reshape(GROUP_W, SSD_STATE)
            yoff_ref[s * l_q:(s + 1) * l_q, cols_x[gi]] = _dot_nt(cg[s * l_q:(s + 1) * l_q, :], h_old)
            xd_s = xd_t[gi] if n_seq == 1 else jnp.where(seq_of_col == s, xd_t[gi], 0.0)
            upd = _dot(xd_s, bg)
            dec = jnp.concatenate(
                [jnp.broadcast_to(jnp.exp(tot[s * l_q:s * l_q + 1, lane0 + h:lane0 + h + 1]),
                                  (SSD_HEADDIM, SSD_STATE)) for h in range(SSD_HG)], axis=0)
            h_ref[s, lane0:lane0 + SSD_HG] = (h_old * dec + upd).reshape(SSD_HG, SSD_HEADDIM, SSD_STATE)

    for gi in groups:
        c = cols_x[gi]
        y = y_in[gi] + yoff_ref[:, c] * from_start_x[gi] + dsk_ref[:, c] * xs[:, c]
        v = y * _silu(z_ref[:, :, c].reshape(rows, GROUP_W))
        v = v * lax.rsqrt(jnp.mean(v * v, axis=-1, keepdims=True) + EPS)
        y_ref[:, c] = (v * ng_ref[:, c]).astype(BF16)


def _head_onehot():
    lane = jnp.arange(3 * LANES)[:, None] % LANES
    head = jnp.arange(D_MODEL)[None, :] // SSD_HEADDIM
    return (lane == head).astype(BF16)


SSD_N_IN = 20
SSD_N_OUT = 5


def _ssd_cast_kernel(*refs, n_cast, **kw):
    ssd_in = refs[:SSD_N_IN]
    cast_in = refs[SSD_N_IN:SSD_N_IN + n_cast]
    ssd_out = refs[SSD_N_IN + n_cast:SSD_N_IN + n_cast + SSD_N_OUT]
    cast_out = refs[SSD_N_IN + n_cast + SSD_N_OUT:SSD_N_IN + 2 * n_cast + SSD_N_OUT]
    scratch = refs[SSD_N_IN + 2 * n_cast + SSD_N_OUT:]
    for src, dst in zip(cast_in, cast_out):
        dst[...] = src[...].astype(BF16)
    _ssd_kernel(*ssd_in, *ssd_out, *scratch, **kw)


def _ssd(proj3, dt3, hist, h0, p, *, n_seq, l_q, g_step, cast_weights=()):
    nb, l_seq, _ = proj3.shape
    wx = g_step * GROUP_W
    wn = g_step * SSD_STATE
    grid = (nb // n_seq, SSD_GROUPS // g_step, l_seq // l_q)
    ob = (O_XBC + D_MODEL) // wn
    oc = ob + (SSD_GROUPS * SSD_STATE) // wn
    cb = D_MODEL // wn
    cc = cb + (SSD_GROUPS * SSD_STATE) // wn

    def rows_spec(width, first_block):
        return pl.BlockSpec((n_seq, l_q, width), lambda n, g, c: (n, c, first_block + g))

    def hist_spec(width, first_block):
        return pl.BlockSpec((n_seq, SSD_CONV - 1, width), lambda n, g, c: (n, 0, first_block + g))

    def par_spec(r, width, first_block):
        return pl.BlockSpec((r, width), lambda n, g, c: (0, first_block + g))

    state_spec = pl.BlockSpec((n_seq, g_step * SSD_HG, SSD_HEADDIM, SSD_STATE), lambda n, g, c: (n, g, 0, 0))
    full_lane = pl.BlockSpec((1, LANES), lambda n, g, c: (0, 0))
    n_steps = grid[0] * grid[1] * grid[2]
    def cast_specs():
        specs = []
        for w in cast_weights:
            assert w.shape[0] % (n_steps * BF16_ROWS) == 0, w.shape
            specs.append(pl.BlockSpec((w.shape[0] // n_steps, w.shape[1]),
                                      lambda n, g, c: ((n * grid[1] + g) * grid[2] + c, 0)))
        return specs
    kern = functools.partial(_ssd_cast_kernel, n_cast=len(cast_weights), n_seq=n_seq, l_q=l_q, g_step=g_step,
                             single_chunk=grid[2] == 1)
    return pl.pallas_call(
        kern,
        grid=grid,
        in_specs=[
            rows_spec(wx, 0), rows_spec(wx, O_XBC // wx), rows_spec(wn, ob), rows_spec(wn, oc),
            pl.BlockSpec((n_seq, l_q, LANES), lambda n, g, c: (n, c, 0)),
            hist_spec(wx, 0), hist_spec(wn, cb), hist_spec(wn, cc),
            state_spec,
            par_spec(SSD_CONV, wx, 0), par_spec(SSD_CONV, wn, cb), par_spec(SSD_CONV, wn, cc),
            par_spec(1, wx, 0), par_spec(1, wn, cb), par_spec(1, wn, cc),
            full_lane, full_lane, par_spec(1, wx, 0), par_spec(1, wx, 0),
            pl.BlockSpec((3 * LANES, wx), lambda n, g, c: (0, 0)),
        ] + cast_specs(),
        out_specs=[
            pl.BlockSpec((SSD_ROWS, wx), lambda n, g, c: (n * grid[2] + c, g)),
            hist_spec(wx, 0), hist_spec(wn, 0), hist_spec(wn, 0),
            state_spec,
        ] + cast_specs(),
        out_shape=[
            jax.ShapeDtypeStruct((nb * l_seq, D_MODEL), BF16),
            jax.ShapeDtypeStruct((nb, SSD_CONV - 1, D_MODEL), F32),
            jax.ShapeDtypeStruct((nb, SSD_CONV - 1, SSD_GROUPS * SSD_STATE), F32),
            jax.ShapeDtypeStruct((nb, SSD_CONV - 1, SSD_GROUPS * SSD_STATE), F32),
            jax.ShapeDtypeStruct((nb, SSD_HEADS, SSD_HEADDIM, SSD_STATE), F32),
        ] + [jax.ShapeDtypeStruct(w.shape, BF16) for w in cast_weights],
        scratch_shapes=[
            pltpu.VMEM((n_seq, 8 + l_q, wx), F32),
            pltpu.VMEM((n_seq, 8 + l_q, wn), F32),
            pltpu.VMEM((n_seq, 8 + l_q, wn), F32),
            pltpu.VMEM((SSD_ROWS, wx), F32),
        ],
        compiler_params=_cparams(3),
        name="ssd",
    )(proj3, proj3, proj3, proj3, dt3, hist, hist, hist, h0,
      p["conv_w"], p["conv_w"], p["conv_w"], p["conv_b"], p["conv_b"], p["conv_b"],
      p["dt_bias"], p["a_log"], p["d_skip"], p["norm_g"], p["head_onehot"], *cast_weights)


CONF_HIST = CONF_KERNEL - 1
CONF_PAD0 = 32


SUBLANES = 8
CONF_NORM_ROWS = 128


def _conf_kernel(a_ref, b_ref, hist_ref, w_ref, bias_ref, g_ref, beta_ref, c_ref, hist_out_ref,
                 pad_ref, ph_ref, cf_ref, wrep_ref, *, n_seq, l_t, rb):
    first = pl.program_id(1) == 0
    lo = CONF_PAD0 - CONF_HIST
    rows = CONF_PAD0 + l_t

    @pl.when(first)
    def _():
        pad_ref[:, lo:CONF_PAD0, :] = hist_ref[...]

    @pl.when(jnp.logical_not(first))
    def _():
        pad_ref[:, lo:CONF_PAD0, :] = pad_ref[:, lo + l_t:CONF_PAD0 + l_t, :]

    pad_ref[:, CONF_PAD0:CONF_PAD0 + l_t, :] = a_ref[...] * _sigmoid(b_ref[...])
    hist_out_ref[...] = pad_ref[:, lo + l_t:CONF_PAD0 + l_t, :]

    full = pad_ref[...]
    for p in range(1, SUBLANES):
        ph_ref[p - 1] = pltpu.roll(full, rows - p, axis=1)

    for j in range(CONF_KERNEL):
        wrep_ref[j] = jnp.broadcast_to(w_ref[j:j + 1, :], (SUBLANES, D_CONV))

    n_rb = l_t // rb

    def conv_block(i, carry):
        s = i // n_rb
        r0 = (i % n_rb) * rb
        n_tiles = rb // SUBLANES
        out_row = pl.multiple_of(s * l_t + r0, SUBLANES)
        for c0 in range(0, D_CONV, LANES):
            lanes = slice(c0, c0 + LANES)
            acc = [jnp.broadcast_to(bias_ref[:, lanes], (SUBLANES, LANES))] * n_tiles
            for j in range(CONF_KERNEL):
                p = (lo + j) % SUBLANES
                src = pad_ref if p == 0 else ph_ref.at[p - 1]
                wj = wrep_ref[j, :, lanes]
                for k in range(n_tiles):
                    start = pl.multiple_of(r0 + (lo + j - p) + k * SUBLANES, SUBLANES)
                    acc[k] = acc[k] + wj * src[s, pl.ds(start, SUBLANES), lanes]
            cf_ref[pl.ds(out_row, rb), lanes] = jnp.concatenate(acc, axis=0)
        return carry

    lax.fori_loop(0, n_seq * n_rb, conv_block, 0)

    nb_rows = min(CONF_NORM_ROWS, n_seq * l_t)

    def norm_block(i, carry):
        r0 = pl.multiple_of(i * nb_rows, nb_rows)
        v = _silu(_layernorm(cf_ref[pl.ds(r0, nb_rows), :], g_ref[...], beta_ref[...]))
        c_ref[pl.ds(r0, nb_rows), :] = v.astype(BF16)
        return carry

    lax.fori_loop(0, (n_seq * l_t) // nb_rows, norm_block, 0)


def _conf(proj3, hist, p, *, n_seq, l_t):
    nb, l_seq, _ = proj3.shape
    rb = min(32, l_t)
    grid = (nb // n_seq, l_seq // l_t)
    par = lambda r: pl.BlockSpec((r, D_CONV), lambda n, t: (0, 0))
    hist_spec = pl.BlockSpec((n_seq, CONF_HIST, D_CONV), lambda n, t: (n, 0, 0))
    kern = functools.partial(_conf_kernel, n_seq=n_seq, l_t=l_t, rb=rb)
    return pl.pallas_call(
        kern,
        grid=grid,
        in_specs=[
            pl.BlockSpec((n_seq, l_t, D_CONV), lambda n, t: (n, t, P_GLU // D_CONV)),
            pl.BlockSpec((n_seq, l_t, D_CONV), lambda n, t: (n, t, P_GLU // D_CONV + 1)),
            hist_spec, par(CONF_KERNEL), par(1), par(1), par(1),
        ],
        out_specs=[pl.BlockSpec((n_seq * l_t, D_CONV), lambda n, t: (n * grid[1] + t, 0)), hist_spec],
        out_shape=[jax.ShapeDtypeStruct((nb * l_seq, D_CONV), BF16),
                   jax.ShapeDtypeStruct((nb, CONF_HIST, D_CONV), F32)],
        scratch_shapes=[pltpu.VMEM((n_seq, CONF_PAD0 + l_t, D_CONV), F32),
                        pltpu.VMEM((SUBLANES - 1, n_seq, CONF_PAD0 + l_t, D_CONV), F32),
                        pltpu.VMEM((n_seq * l_t, D_CONV), F32),
                        pltpu.VMEM((CONF_KERNEL, SUBLANES, D_CONV), F32)],
        compiler_params=_cparams(2),
        name="conf",
    )(proj3, proj3, hist, p["conf_w"], p["conf_b"], p["conf_ln_g"], p["conf_ln_b"])


def _attn_kernel(q_ref, k_ref, v_ref, o_ref, *, n_seq):
    scale = MEM_HEAD_DIM ** -0.5
    few_rows = q_ref.shape[1] < LANES
    for h in range(MEM_HEADS):
        cols = slice(h * MEM_HEAD_DIM, (h + 1) * MEM_HEAD_DIM)
        if few_rows:
            scores = [_dot_nt(k_ref[s, :, cols], q_ref[s, :, cols]) * scale for s in range(n_seq)]
            axis = 0
        else:
            scores = [_dot_nt(q_ref[s, :, cols], k_ref[s, :, cols]) * scale for s in range(n_seq)]
            axis = 1
        probs = []
        for sc in scores:
            e = jnp.exp(sc - jnp.max(sc, axis=axis, keepdims=True))
            probs.append((e / jnp.sum(e, axis=axis, keepdims=True)).astype(BF16))
        if few_rows:
            outs = [lax.dot_general(pr, v_ref[s, :, cols].astype(BF16), (((0,), (0,)), ((), ())),
                                    preferred_element_type=F32) for s, pr in enumerate(probs)]
        else:
            outs = [_dot(pr, v_ref[s, :, cols]) for s, pr in enumerate(probs)]
        o_ref[:, cols] = jnp.concatenate(outs, axis=0).astype(BF16)


def _attn(proj3, mem_k, mem_v, *, n_seq, l_t):
    nb, l_seq, _ = proj3.shape
    grid = (nb // n_seq, l_seq // l_t)
    kv_spec = pl.BlockSpec((n_seq, MEM_LEN, ATT_W), lambda n, t: (n, 0, 0))
    return pl.pallas_call(
        functools.partial(_attn_kernel, n_seq=n_seq),
        grid=grid,
        in_specs=[pl.BlockSpec((n_seq, l_t, ATT_W), lambda n, t: (n, t, P_Q // ATT_W)), kv_spec, kv_spec],
        out_specs=pl.BlockSpec((n_seq * l_t, ATT_W), lambda n, t: (n * grid[1] + t, 0)),
        out_shape=jax.ShapeDtypeStruct((nb * l_seq, ATT_W), BF16),
        compiler_params=_cparams(2),
        name="attn",
    )(proj3, mem_k, mem_v)


KV_HALVES = MEM_HEAD_DIM // LANES
KV_ROWS = MEM_LEN * MEM_HEADS * KV_HALVES


def _attn_rows_kernel(q_ref, k_ref, v_ref, o_ref, *, n_seq, l_t):
    scale = MEM_HEAD_DIM ** -0.5
    nq = MEM_HEADS * l_t
    col = lax.broadcasted_iota(jnp.int32, (nq, KV_ROWS), 1)
    valid = jnp.bitwise_and(col, MEM_HEADS * KV_HALVES - 1) == _iota_div((nq, KV_ROWS), 0, l_t)
    parts = []
    for s in range(n_seq):
        q2 = jnp.concatenate([q_ref[s, :, (h * KV_HALVES + c) * LANES:(h * KV_HALVES + c + 1) * LANES]
                              for c in range(KV_HALVES) for h in range(MEM_HEADS)], axis=0)
        parts.append(_dot_nt(q2, k_ref[s]))
    probs = []
    for part in parts:
        sc = (part[0:nq] + pltpu.roll(part[nq:2 * nq], KV_ROWS - MEM_HEADS, axis=1)) * scale
        sc = jnp.where(valid, sc, -jnp.inf)
        e = jnp.exp(sc - jnp.max(sc, axis=1, keepdims=True))
        pr = e / jnp.sum(e, axis=1, keepdims=True)
        probs.append(jnp.concatenate([pr, pltpu.roll(pr, MEM_HEADS, axis=1)], axis=0).astype(BF16))
    outs = [_dot(p2, v_ref[s]) for s, p2 in enumerate(probs)]
    for c in range(KV_HALVES):
        for h in range(MEM_HEADS):
            r0 = (c * MEM_HEADS + h) * l_t
            piece = jnp.concatenate([o[r0:r0 + l_t] for o in outs], axis=0)
            o_ref[:, (h * KV_HALVES + c) * LANES:(h * KV_HALVES + c + 1) * LANES] = piece.astype(BF16)


def _attn_rows(proj3, k_rows, v_rows, *, n_seq):
    nb, l_t, _ = proj3.shape
    assert l_t % SUBLANES == 0 and KV_HALVES == 2
    kv_spec = pl.BlockSpec((n_seq, KV_ROWS, LANES), lambda n: (n, 0, 0))
    return pl.pallas_call(
        functools.partial(_attn_rows_kernel, n_seq=n_seq, l_t=l_t),
        grid=(nb // n_seq,),
        in_specs=[pl.BlockSpec((n_seq, l_t, ATT_W), lambda n: (n, 0, P_Q // ATT_W)), kv_spec, kv_spec],
        out_specs=pl.BlockSpec((n_seq * l_t, ATT_W), lambda n: (n, 0)),
        out_shape=jax.ShapeDtypeStruct((nb * l_t, ATT_W), BF16),
        compiler_params=_cparams(1),
        name="attn_rows",
    )(proj3, k_rows, v_rows)


def _interleave_kv(kv):
    b = kv.shape[0]
    kv = kv.reshape(b, MEM_LEN, MEM_HEADS, KV_HALVES, LANES)
    return jnp.transpose(kv, (0, 1, 3, 2, 4)).reshape(b, KV_ROWS, LANES)


MERGE_TM = 256


def _merge_kernel(x_ref, y_ref, c_ref, o_ref, g0_ref, g1_ref, g2_ref, bg_ref, wssd_ref, wconf_ref, wmem_ref,
                  wout_ref, lg_ref, lb_ref, l1g_ref, l1b_ref, h1_ref):
    merged = _sigmoid(g0_ref[...] + bg_ref[0:1, :]) * _dot(y_ref[...], wssd_ref[...])
    merged = merged + _sigmoid(g1_ref[...] + bg_ref[1:2, :]) * _dot(c_ref[...], wconf_ref[...])
    merged = merged + _sigmoid(g2_ref[...] + bg_ref[2:3, :]) * _dot(o_ref[...], wmem_ref[...])
    mix = _dot(merged, wout_ref[...])
    h = _layernorm(x_ref[...], lg_ref[...], lb_ref[...])
    h1_ref[...] = _layernorm(ALPHA * h + mix, l1g_ref[...], l1b_ref[...])


def _merge(x2d, y2d, c2d, o2d, proj2d, p, wb):
    m = x2d.shape[0]
    tm = MERGE_TM
    rows = lambda w, blk: pl.BlockSpec((tm, w), lambda i: (i, blk))
    whole = lambda a: pl.BlockSpec(a.shape, lambda i: (0, 0), pipeline_mode=pl.Buffered(1))
    vec = pl.BlockSpec((1, D_MODEL), lambda i: (0, 0))
    g_blk = P_GATE // D_MODEL
    return pl.pallas_call(
        _merge_kernel,
        grid=(m // tm,),
        in_specs=[
            rows(D_MODEL, 0), rows(D_MODEL, 0), rows(D_CONV, 0), rows(ATT_W, 0),
            rows(D_MODEL, g_blk), rows(D_MODEL, g_blk + 1), rows(D_MODEL, g_blk + 2),
            pl.BlockSpec((3, D_MODEL), lambda i: (0, 0)),
            whole(wb["w_br_ssd"]), whole(wb["w_br_conf"]), whole(wb["w_br_mem"]), whole(wb["w_out"]),
            vec, vec, vec, vec,
        ],
        out_specs=rows(D_MODEL, 0),
        out_shape=jax.ShapeDtypeStruct((m, D_MODEL), F32),
        compiler_params=_cparams(1),
        name="merge",
    )(x2d, y2d, c2d, o2d, proj2d, proj2d, proj2d, p["b_gate"],
      wb["w_br_ssd"], wb["w_br_conf"], wb["w_br_mem"], wb["w_out"],
      p["ln_in_g"], p["ln_in_b"], p["ln1_g"], p["ln1_b"])


MLP_TM = 512
MLP_TF = 1024


def _mlp_kernel(h_ref, w1_ref, b1_ref, w2_ref, b2_ref, g_ref, b_ref, out_ref, hb_ref):
    f = pl.program_id(1)

    @pl.when(f == 0)
    def _():
        hb_ref[...] = h_ref[...].astype(BF16)
        out_ref[...] = jnp.zeros_like(out_ref)

    a = jnp.maximum(jnp.dot(hb_ref[...], w1_ref[...], preferred_element_type=F32) + b1_ref[...], 0.0)
    out_ref[...] += _dot(a * a, w2_ref[...])

    @pl.when(f == pl.num_programs(1) - 1)
    def _():
        out_ref[...] = _layernorm(ALPHA * h_ref[...] + (out_ref[...] + b2_ref[...]), g_ref[...], b_ref[...])


def _mlp(h2d, p, wb):
    m = h2d.shape[0]
    tm, tf = MLP_TM, MLP_TF
    vec = pl.BlockSpec((1, D_MODEL), lambda i, f: (0, 0))
    return pl.pallas_call(
        _mlp_kernel,
        grid=(m // tm, D_FF // tf),
        in_specs=[
            pl.BlockSpec((tm, D_MODEL), lambda i, f: (i, 0)),
            pl.BlockSpec((D_MODEL, tf), lambda i, f: (0, f)),
            pl.BlockSpec((1, tf), lambda i, f: (0, f)),
            pl.BlockSpec((tf, D_MODEL), lambda i, f: (f, 0)),
            vec, vec, vec,
        ],
        out_specs=pl.BlockSpec((tm, D_MODEL), lambda i, f: (i, 0)),
        out_shape=jax.ShapeDtypeStruct((m, D_MODEL), F32),
        scratch_shapes=[pltpu.VMEM((tm, D_MODEL), BF16)],
        compiler_params=_cparams(2),
        name="mlp",
    )(h2d, wb["w_ff1"], p["b_ff1"], wb["w_ff2"], p["b_ff2"], p["ln2_g"], p["ln2_b"])


CAST_WEIGHTS = ("w_ff1", "w_ff2", "w_br_ssd", "w_out", "w_br_conf", "w_br_mem")


def _trunk(x, ssd_hist, ssm0, conf_hist, mem_k, mem_v, p, wb, *, ssd_cfg, conf_cfg, attn_fn):
    nb, l_seq, _ = x.shape
    x2d = x.reshape(nb * l_seq, D_MODEL)
    proj2d, dt2d = _proj(x2d, p["ln_in_g"], p["ln_in_b"], p["w_in_t"])
    proj3 = proj2d.reshape(nb, l_seq, P_W)
    dt3 = dt2d.reshape(nb, l_seq, LANES)
    if wb is None:
        y, hx, hb, hc, ssm1, *cast = _ssd(proj3, dt3, ssd_hist, ssm0, p, cast_weights=[p[n] for n in CAST_WEIGHTS],
                                          **ssd_cfg)
        wb = dict(zip(CAST_WEIGHTS, cast))
    else:
        y, hx, hb, hc, ssm1 = _ssd(proj3, dt3, ssd_hist, ssm0, p, **ssd_cfg)
    c, conf_hist1 = _conf(proj3, conf_hist, p, **conf_cfg)
    o = attn_fn(proj3, mem_k, mem_v)
    h1 = _merge(x2d, y, c, o, proj2d, p, wb)
    out = _mlp(h1, p, wb).reshape(nb, l_seq, D_MODEL)
    return out, ssm1, jnp.concatenate([hx, hb, hc], axis=-1), conf_hist1, wb


def kernel(x_prompt, x_sample, mem_prompt, state_ssm, state_ssd_conv, state_conf_conv, cache_mem_k, cache_mem_v, ln_in_g, ln_in_b, w_in, b_gate, ssd_conv_w, ssd_conv_b, ssd_dt_bias, ssd_a_log, ssd_d, ssd_norm_g, conf_dw_w, conf_dw_b, conf_ln_g, conf_ln_b, w_mem_k, w_mem_v, w_br_ssd, w_br_conf, w_br_mem, w_out, ln1_g, ln1_b, w_ff1, b_ff1, w_ff2, b_ff2, ln2_g, ln2_b):
    layer = 0
    row = lambda v: v.reshape(1, -1)
    lane_pad = lambda v: jnp.pad(v.reshape(1, -1), ((0, 0), (0, LANES - v.size)))
    p = {
        "ln_in_g": row(ln_in_g), "ln_in_b": row(ln_in_b),
        "w_in_t": jnp.swapaxes(w_in[layer], 0, 1),
        "b_gate": b_gate[layer],
        "conv_w": ssd_conv_w[layer], "conv_b": row(ssd_conv_b[layer]),
        "dt_bias": lane_pad(ssd_dt_bias[layer]), "a_log": lane_pad(ssd_a_log[layer]),
        "d_skip": row(jnp.repeat(ssd_d[layer], SSD_HEADDIM)), "norm_g": row(ssd_norm_g[layer]),
        "head_onehot": _head_onehot(),
        "conf_w": conf_dw_w[layer], "conf_b": row(conf_dw_b[layer]),
        "conf_ln_g": row(conf_ln_g[layer]), "conf_ln_b": row(conf_ln_b[layer]),
        "w_br_ssd": w_br_ssd[layer], "w_br_conf": w_br_conf[layer],
        "w_br_mem": w_br_mem[layer], "w_out": w_out[layer],
        "ln1_g": row(ln1_g[layer]), "ln1_b": row(ln1_b[layer]),
        "w_ff1": w_ff1[layer], "b_ff1": row(b_ff1[layer]),
        "w_ff2": w_ff2[layer], "b_ff2": row(b_ff2[layer]),
        "ln2_g": row(ln2_g[layer]), "ln2_b": row(ln2_b[layer]),
    }
    n_p, l_p, _ = x_prompt.shape
    n_s, l_s, _ = x_sample.shape

    mem2d = mem_prompt.reshape(n_p * MEM_LEN, D_MODEL)
    p_mem_k = _matmul(mem2d, w_mem_k[layer]).reshape(n_p, MEM_LEN, ATT_W)
    p_mem_v = _matmul(mem2d, w_mem_v[layer]).reshape(n_p, MEM_LEN, ATT_W)
    y_prompt, p_ssm, p_ssd_conv, p_cc, wb = _trunk(
        x_prompt,
        jnp.zeros((n_p, SSD_CONV - 1, CONV_DIM), F32),
        jnp.zeros((n_p, SSD_HEADS, SSD_HEADDIM, SSD_STATE), F32),
        jnp.zeros((n_p, CONF_HIST, D_CONV), F32),
        p_mem_k, p_mem_v, p, None,
        ssd_cfg=dict(n_seq=1, l_q=SSD_ROWS, g_step=SSD_GROUPS),
        conf_cfg=dict(n_seq=1, l_t=512),
        attn_fn=functools.partial(_attn, n_seq=1, l_t=512))

    y_sample, s_ssm, s_ssd_conv, s_cc, _ = _trunk(
        x_sample, state_ssd_conv[layer], state_ssm[layer], state_conf_conv[layer],
        _interleave_kv(cache_mem_k[layer]), _interleave_kv(cache_mem_v[layer]), p, wb,
        ssd_cfg=dict(n_seq=SSD_ROWS // l_s, l_q=l_s, g_step=4),
        conf_cfg=dict(n_seq=16, l_t=l_s),
        attn_fn=functools.partial(_attn_rows, n_seq=8))

    kv_shape = (DEPTH, n_p, MEM_LEN, MEM_HEADS, MEM_HEAD_DIM)
    return (y_prompt, y_sample, p_ssm[None], p_ssd_conv[None], p_cc[None],
            p_mem_k.reshape(kv_shape), p_mem_v.reshape(kv_shape),
            s_ssm[None], s_ssd_conv[None], s_cc[None])
```

```python
import functools
import math

import jax
import jax.numpy as jnp
from jax import lax
from jax.experimental import pallas as pl
from jax.experimental.pallas import tpu as pltpu

F32 = jnp.float32
BF16 = jnp.bfloat16

D_MODEL = 2048
SSD_HEADDIM = 64
SSD_HEADS = 32
SSD_GROUPS = 8
SSD_HG = 4
SSD_STATE = 128
SSD_CONV = 4
CONV_DIM = 4096
D_CONV = 1024
CONF_KERNEL = 31
MEM_LEN = 256
MEM_HEADS = 4
MEM_HEAD_DIM = 256
ATT_W = 1024
D_FF = 8192
DEPTH = 1
ALPHA = (2.0 * DEPTH) ** 0.25
EPS = 1e-5

O_XBC = 2048
O_DT = 6144
O_GLU = 6176
P_GATE = 6144
P_GLU = 12288
P_Q = 14336
P_W = 15360

GROUP_W = SSD_HG * SSD_HEADDIM
SSD_ROWS = 128
LANES = 128
BF16_ROWS = 16
VMEM_LIMIT = 56 * 1024 * 1024


def _cparams(n_axes):
    return pltpu.CompilerParams(dimension_semantics=("arbitrary",) * n_axes, vmem_limit_bytes=VMEM_LIMIT)


def _layernorm(x, g, b):
    mu = jnp.mean(x, axis=-1, keepdims=True)
    xc = x - mu
    var = jnp.mean(xc * xc, axis=-1, keepdims=True)
    return xc * lax.rsqrt(var + EPS) * g + b


def _sigmoid(x):
    return 0.5 * jnp.tanh(0.5 * x) + 0.5


def _silu(x):
    return x * _sigmoid(x)


def _dot(a, b):
    return jnp.dot(a.astype(BF16), b.astype(BF16), preferred_element_type=F32)


def _dot_nt(a, b):
    return lax.dot_general(a.astype(BF16), b.astype(BF16), (((1,), (1,)), ((), ())), preferred_element_type=F32)


def _iota_div(shape, axis, divisor):
    shift = divisor.bit_length() - 1
    assert 1 << shift == divisor
    return lax.shift_right_logical(lax.broadcasted_iota(jnp.int32, shape, axis), shift)


PROJ_TM = 1024
PROJ_TN = 1024
PROJ_LN_ROWS = 256
PROJ_NA = O_DT // PROJ_TN
PROJ_NB = (P_W - O_DT) // PROJ_TN


def _proj_kernel(x_ref, g_ref, b_ref, wt_ref, wdt_ref, out_ref, dt_ref, xn_ref):
    @pl.when(pl.program_id(1) == 0)
    def _():
        def ln_rows(r, carry):
            r0 = pl.multiple_of(r * PROJ_LN_ROWS, PROJ_LN_ROWS)
            xn = _layernorm(x_ref[pl.ds(r0, PROJ_LN_ROWS), :], g_ref[...], b_ref[...])
            xn_ref[pl.ds(r0, PROJ_LN_ROWS), :] = xn.astype(BF16)
            return carry
        lax.fori_loop(0, x_ref.shape[0] // PROJ_LN_ROWS, ln_rows, 0)
        dt_ref[...] = _dot_nt(xn_ref[...], wdt_ref[...])

    out_ref[...] = _dot_nt(xn_ref[...], wt_ref[...])


def _proj_w_row(j):
    t = BF16_ROWS
    assert O_GLU % t == 0 and PROJ_TN % t == 0
    return t * jnp.where(j < PROJ_NA, j * (PROJ_TN // t), O_GLU // t + (j - PROJ_NA) * (PROJ_TN // t))


def _proj_out_tile(j):
    jb = j - PROJ_NA
    n_glu_q = (P_W - P_GLU) // PROJ_TN
    tile_b = jnp.where(jb < n_glu_q, P_GLU // PROJ_TN + jb, P_GATE // PROJ_TN + jb - n_glu_q)
    return jnp.where(j < PROJ_NA, j, tile_b)


def _proj(x2d, ln_g, ln_b, w_in_t):
    m = x2d.shape[0]
    tm = min(PROJ_TM, m)
    grid = (m // tm, PROJ_NA + PROJ_NB)
    return pl.pallas_call(
        _proj_kernel,
        grid=grid,
        in_specs=[
            pl.BlockSpec((tm, D_MODEL), lambda i, j: (i, 0), pipeline_mode=pl.Buffered(1)),
            pl.BlockSpec((1, D_MODEL), lambda i, j: (0, 0)),
            pl.BlockSpec((1, D_MODEL), lambda i, j: (0, 0)),
            pl.BlockSpec((pl.Element(PROJ_TN), pl.Element(D_MODEL)), lambda i, j: (_proj_w_row(j), 0)),
            pl.BlockSpec((LANES, D_MODEL), lambda i, j: (O_DT // LANES, 0)),
        ],
        out_specs=[
            pl.BlockSpec((tm, PROJ_TN), lambda i, j: (i, _proj_out_tile(j))),
            pl.BlockSpec((tm, LANES), lambda i, j: (i, 0)),
        ],
        out_shape=[jax.ShapeDtypeStruct((m, P_W), F32), jax.ShapeDtypeStruct((m, LANES), F32)],
        scratch_shapes=[pltpu.VMEM((tm, D_MODEL), BF16)],
        compiler_params=_cparams(2),
        name="proj",
    )(x2d, ln_g, ln_b, w_in_t, w_in_t)


def _matmul_kernel(x_ref, w_ref, out_ref):
    out_ref[...] = _dot(x_ref[...], w_ref[...])


def _matmul(x2d, w, tn=512):
    m, k = x2d.shape
    n = w.shape[1]
    return pl.pallas_call(
        _matmul_kernel,
        grid=(n // tn,),
        in_specs=[pl.BlockSpec((m, k), lambda j: (0, 0)), pl.BlockSpec((k, tn), lambda j: (0, j))],
        out_specs=pl.BlockSpec((m, tn), lambda j: (0, j)),
        out_shape=jax.ShapeDtypeStruct((m, n), F32),
        compiler_params=_cparams(1),
        name="memkv",
    )(x2d, w)


def _split3(v):
    hi = v.astype(BF16)
    r1 = v - hi.astype(F32)
    mid = r1.astype(BF16)
    lo = (r1 - mid.astype(F32)).astype(BF16)
    return jnp.concatenate([hi, mid, lo], axis=1)


def _expand_heads(pieces, onehot3):
    return jnp.dot(pieces, onehot3, preferred_element_type=F32)


def _conv4(pad_ref, u, hist_out_ref, w_ref, b_ref, l_q):
    pad_ref[:, 8:8 + l_q, :] = u
    full = pad_ref[...]
    acc = b_ref[...] + w_ref[SSD_CONV - 1:SSD_CONV, :] * u
    for d in range(1, SSD_CONV):
        shifted = pltpu.roll(full, d, axis=1)[:, 8:8 + l_q, :]
        acc = acc + w_ref[SSD_CONV - 1 - d:SSD_CONV - d, :] * shifted
    last3 = pad_ref[:, 5 + l_q:8 + l_q, :]
    hist_out_ref[...] = last3
    pad_ref[:, 5:8, :] = last3
    return _silu(acc)


def _ssd_kernel(z_ref, x_ref, bm_ref, cm_ref, dt_ref, hx_ref, hb_ref, hc_ref, h0_ref,
                wx_ref, wb_ref, wc_ref, bx_ref, bb_ref, bc_ref, dtb_ref, alog_ref, dsk_ref, ng_ref, oh_ref,
                y_ref, ox_ref, ob_ref, oc_ref, h_ref,
                px_ref, pb_ref, pc_ref, yoff_ref, *, n_seq, l_q, g_step, single_chunk):
    rows = n_seq * l_q
    gs = pl.program_id(1)
    first = pl.program_id(2) == 0

    h_in_ref = h0_ref if single_chunk else h_ref

    def load_history():
        px_ref[:, 5:8, :] = hx_ref[...]
        pb_ref[:, 5:8, :] = hb_ref[...]
        pc_ref[:, 5:8, :] = hc_ref[...]

    if single_chunk:
        load_history()
    else:
        @pl.when(first)
        def _():
            load_history()
            h_ref[...] = h0_ref[...]

    xs = _conv4(px_ref, x_ref[...], ox_ref, wx_ref, bx_ref, l_q).reshape(rows, g_step * GROUP_W)
    bm = _conv4(pb_ref, bm_ref[...], ob_ref, wb_ref, bb_ref, l_q).reshape(rows, g_step * SSD_STATE)
    cm = _conv4(pc_ref, cm_ref[...], oc_ref, wc_ref, bc_ref, l_q).reshape(rows, g_step * SSD_STATE)

    lane = lax.broadcasted_iota(jnp.int32, (1, LANES), 1)
    dt_raw = dt_ref[...].reshape(rows, LANES) + dtb_ref[...]
    dt = jnp.maximum(dt_raw, 0.0) + jnp.log1p(jnp.exp(-jnp.abs(dt_raw)))
    dt = jnp.where(lane < SSD_HEADS, dt, 0.0)
    a = dt * (-jnp.exp(alog_ref[...]))
    if g_step != SSD_GROUPS:
        src = jnp.bitwise_and(lax.broadcasted_iota(jnp.int32, (3 * LANES, LANES), 0), LANES - 1)
        dst = lax.broadcasted_iota(jnp.int32, (3 * LANES, LANES), 1)
        pick = (src == dst + gs * (g_step * SSD_HG)).astype(BF16)
        dt = _expand_heads(_split3(dt), pick)
        a = _expand_heads(_split3(a), pick)
    rq = lax.broadcasted_iota(jnp.int32, (rows, rows), 0)
    ck = lax.broadcasted_iota(jnp.int32, (rows, rows), 1)
    same = _iota_div((rows, rows), 0, l_q) == _iota_div((rows, rows), 1, l_q)
    causal = jnp.logical_and(same, ck <= rq)
    sums = jnp.dot(jnp.concatenate([causal, same], axis=0).astype(BF16), _split3(a), preferred_element_type=F32)
    sums = sums[:, 0:LANES] + sums[:, LANES:2 * LANES] + sums[:, 2 * LANES:3 * LANES]
    acs = sums[0:rows]
    tot = sums[rows:2 * rows]
    acs_t = acs.T
    seq_of_col = _iota_div((1, rows), 1, l_q)

    dt_p = _split3(dt)
    to_end_p = _split3(jnp.exp(tot - acs))
    from_start_p = _split3(jnp.exp(acs))
    head_of_col = _iota_div((1, GROUP_W), 1, SSD_HEADDIM)
    groups = range(g_step)
    cols_x = [slice(gi * GROUP_W, (gi + 1) * GROUP_W) for gi in groups]
    cols_n = [slice(gi * SSD_STATE, (gi + 1) * SSD_STATE) for gi in groups]

    dt_x = [_expand_heads(dt_p, oh_ref[:, c]) for c in cols_x]
    to_end_x = [_expand_heads(to_end_p, oh_ref[:, c]) for c in cols_x]
    from_start_x = [_expand_heads(from_start_p, oh_ref[:, c]) for c in cols_x]
    cbm = [jnp.where(causal, _dot_nt(cm[:, c], bm[:, c]), 0.0) for c in cols_n]

    m_cat, x_cat, xd_t = [], [], []
    for gi in groups:
        lane0 = gi * SSD_HG
        xdt = xs[:, cols_x[gi]] * dt_x[gi]
        xdt_b = xdt.astype(BF16)
        m_heads, x_heads = [], []
        for h in range(SSD_HG):
            diff = acs[:, lane0 + h:lane0 + h + 1] - acs_t[lane0 + h:lane0 + h + 1, :]
            m_heads.append((cbm[gi] * jnp.exp(jnp.where(causal, diff, 0.0))).astype(BF16))
            x_heads.append(jnp.where(head_of_col == h, xdt_b, jnp.zeros_like(xdt_b)))
        m_cat.append(jnp.concatenate(m_heads, axis=1))
        x_cat.append(jnp.concatenate(x_heads, axis=0))
        xd_t.append((xdt * to_end_x[gi]).T)

    y_in = [jnp.dot(m_cat[gi], x_cat[gi], preferred_element_type=F32) for gi in groups]

    for gi in groups:
        lane0 = gi * SSD_HG
        bg = bm[:, cols_n[gi]]
        cg = cm[:, cols_n[gi]]
        for s in range(n_seq):
            h_old = h_in_ref[s, lane0:lane0 + SSD_HG].reshape(GROUP_W, SSD_STATE)
            yoff_ref[s * l_q:(s + 1) * l_q, cols_x[gi]] = _dot_nt(cg[s * l_q:(s + 1) * l_q, :], h_old)
            xd_s = xd_t[gi] if n_seq == 1 else jnp.where(seq_of_col == s, xd_t[gi], 0.0)
            upd = _dot(xd_s, bg)
            dec = jnp.concatenate(
                [jnp.broadcast_to(jnp.exp(tot[s * l_q:s * l_q + 1, lane0 + h:lane0 + h + 1]),
                                  (SSD_HEADDIM, SSD_STATE)) for h in range(SSD_HG)], axis=0)
            h_ref[s, lane0:lane0 + SSD_HG] = (h_old * dec + upd).reshape(SSD_HG, SSD_HEADDIM, SSD_STATE)

    for gi in groups:
        c = cols_x[gi]
        y = y_in[gi] + yoff_ref[:, c] * from_start_x[gi] + dsk_ref[:, c] * xs[:, c]
        v = y * _silu(z_ref[:, :, c].reshape(rows, GROUP_W))
        v = v * lax.rsqrt(jnp.mean(v * v, axis=-1, keepdims=True) + EPS)
        y_ref[:, c] = (v * ng_ref[:, c]).astype(BF16)


def _head_onehot():
    lane = jnp.arange(3 * LANES)[:, None] % LANES
    head = jnp.arange(D_MODEL)[None, :] // SSD_HEADDIM
    return (lane == head).astype(BF16)


SSD_N_IN = 20
SSD_N_OUT = 5


def _ssd_cast_kernel(*refs, n_cast, **kw):
    ssd_in = refs[:SSD_N_IN]
    cast_in = refs[SSD_N_IN:SSD_N_IN + n_cast]
    ssd_out = refs[SSD_N_IN + n_cast:SSD_N_IN + n_cast + SSD_N_OUT]
    cast_out = refs[SSD_N_IN + n_cast + SSD_N_OUT:SSD_N_IN + 2 * n_cast + SSD_N_OUT]
    scratch = refs[SSD_N_IN + 2 * n_cast + SSD_N_OUT:]
    for src, dst in zip(cast_in, cast_out):
        dst[...] = src[...].astype(BF16)
    _ssd_kernel(*ssd_in, *ssd_out, *scratch, **kw)


def _ssd(proj3, dt3, hist, h0, p, *, n_seq, l_q, g_step, cast_weights=()):
    nb, l_seq, _ = proj3.shape
    wx = g_step * GROUP_W
    wn = g_step * SSD_STATE
    grid = (nb // n_seq, SSD_GROUPS // g_step, l_seq // l_q)
    ob = (O_XBC + D_MODEL) // wn
    oc = ob + (SSD_GROUPS * SSD_STATE) // wn
    cb = D_MODEL // wn
    cc = cb + (SSD_GROUPS * SSD_STATE) // wn

    def rows_spec(width, first_block):
        return pl.BlockSpec((n_seq, l_q, width), lambda n, g, c: (n, c, first_block + g))

    def hist_spec(width, first_block):
        return pl.BlockSpec((n_seq, SSD_CONV - 1, width), lambda n, g, c: (n, 0, first_block + g))

    def par_spec(r, width, first_block):
        return pl.BlockSpec((r, width), lambda n, g, c: (0, first_block + g))

    state_spec = pl.BlockSpec((n_seq, g_step * SSD_HG, SSD_HEADDIM, SSD_STATE), lambda n, g, c: (n, g, 0, 0))
    full_lane = pl.BlockSpec((1, LANES), lambda n, g, c: (0, 0))
    n_steps = grid[0] * grid[1] * grid[2]
    def cast_specs():
        specs = []
        for w in cast_weights:
            assert w.shape[0] % (n_steps * BF16_ROWS) == 0, w.shape
            specs.append(pl.BlockSpec((w.shape[0] // n_steps, w.shape[1]),
                                      lambda n, g, c: ((n * grid[1] + g) * grid[2] + c, 0)))
        return specs
    kern = functools.partial(_ssd_cast_kernel, n_cast=len(cast_weights), n_seq=n_seq, l_q=l_q, g_step=g_step,
                             single_chunk=grid[2] == 1)
    return pl.pallas_call(
        kern,
        grid=grid,
        in_specs=[
            rows_spec(wx, 0), rows_spec(wx, O_XBC // wx), rows_spec(wn, ob), rows_spec(wn, oc),
            pl.BlockSpec((n_seq, l_q, LANES), lambda n, g, c: (n, c, 0)),
            hist_spec(wx, 0), hist_spec(wn, cb), hist_spec(wn, cc),
            state_spec,
            par_spec(SSD_CONV, wx, 0), par_spec(SSD_CONV, wn, cb), par_spec(SSD_CONV, wn, cc),
            par_spec(1, wx, 0), par_spec(1, wn, cb), par_spec(1, wn, cc),
            full_lane, full_lane, par_spec(1, wx, 0), par_spec(1, wx, 0),
            pl.BlockSpec((3 * LANES, wx), lambda n, g, c: (0, 0)),
        ] + cast_specs(),
        out_specs=[
            pl.BlockSpec((SSD_ROWS, wx), lambda n, g, c: (n * grid[2] + c, g)),
            hist_spec(wx, 0), hist_spec(wn, 0), hist_spec(wn, 0),
            state_spec,
        ] + cast_specs(),
        out_shape=[
            jax.ShapeDtypeStruct((nb * l_seq, D_MODEL), BF16),
            jax.ShapeDtypeStruct((nb, SSD_CONV - 1, D_MODEL), F32),
            jax.ShapeDtypeStruct((nb, SSD_CONV - 1, SSD_GROUPS * SSD_STATE), F32),
            jax.ShapeDtypeStruct((nb, SSD_CONV - 1, SSD_GROUPS * SSD_STATE), F32),
            jax.ShapeDtypeStruct((nb, SSD_HEADS, SSD_HEADDIM, SSD_STATE), F32),
        ] + [jax.ShapeDtypeStruct(w.shape, BF16) for w in cast_weights],
        scratch_shapes=[
            pltpu.VMEM((n_seq, 8 + l_q, wx), F32),
            pltpu.VMEM((n_seq, 8 + l_q, wn), F32),
            pltpu.VMEM((n_seq, 8 + l_q, wn), F32),
            pltpu.VMEM((SSD_ROWS, wx), F32),
        ],
        compiler_params=_cparams(3),
        name="ssd",
    )(proj3, proj3, proj3, proj3, dt3, hist, hist, hist, h0,
      p["conv_w"], p["conv_w"], p["conv_w"], p["conv_b"], p["conv_b"], p["conv_b"],
      p["dt_bias"], p["a_log"], p["d_skip"], p["norm_g"], p["head_onehot"], *cast_weights)


CONF_HIST = CONF_KERNEL - 1
CONF_PAD0 = 32


SUBLANES = 8
CONF_NORM_ROWS = 128


def _conf_kernel(a_ref, b_ref, hist_ref, w_ref, bias_ref, g_ref, beta_ref, c_ref, hist_out_ref,
                 pad_ref, ph_ref, cf_ref, wrep_ref, *, n_seq, l_t, rb, single_chunk):
    first = pl.program_id(1) == 0
    lo = CONF_PAD0 - CONF_HIST
    rows = CONF_PAD0 + l_t

    if single_chunk:
        pad_ref[:, lo:CONF_PAD0, :] = hist_ref[...]
    else:
        @pl.when(first)
        def _():
            pad_ref[:, lo:CONF_PAD0, :] = hist_ref[...]

    pad_ref[:, CONF_PAD0:CONF_PAD0 + l_t, :] = a_ref[...] * _sigmoid(b_ref[...])
    hist_out_ref[...] = pad_ref[:, lo + l_t:CONF_PAD0 + l_t, :]

    full = pad_ref[...]
    for p in range(1, SUBLANES):
        ph_ref[p - 1] = pltpu.roll(full, rows - p, axis=1)

    for j in range(CONF_KERNEL):
        wrep_ref[j] = jnp.broadcast_to(w_ref[j:j + 1, :], (SUBLANES, D_CONV))

    n_rb = l_t // rb

    def conv_block(i, carry):
        s = i // n_rb
        r0 = (i % n_rb) * rb
        n_tiles = rb // SUBLANES
        out_row = pl.multiple_of(s * l_t + r0, SUBLANES)
        for c0 in range(0, D_CONV, LANES):
            lanes = slice(c0, c0 + LANES)
            acc = [jnp.broadcast_to(bias_ref[:, lanes], (SUBLANES, LANES))] * n_tiles
            for j in range(CONF_KERNEL):
                p = (lo + j) % SUBLANES
                src = pad_ref if p == 0 else ph_ref.at[p - 1]
                wj = wrep_ref[j, :, lanes]
                for k in range(n_tiles):
                    start = pl.multiple_of(r0 + (lo + j - p) + k * SUBLANES, SUBLANES)
                    acc[k] = acc[k] + wj * src[s, pl.ds(start, SUBLANES), lanes]
            cf_ref[pl.ds(out_row, rb), lanes] = jnp.concatenate(acc, axis=0)
        return carry

    lax.fori_loop(0, n_seq * n_rb, conv_block, 0, unroll=2)

    nb_rows = min(CONF_NORM_ROWS, n_seq * l_t)

    def norm_block(i, carry):
        r0 = pl.multiple_of(i * nb_rows, nb_rows)
        v = _silu(_layernorm(cf_ref[pl.ds(r0, nb_rows), :], g_ref[...], beta_ref[...]))
        c_ref[pl.ds(r0, nb_rows), :] = v.astype(BF16)
        return carry

    lax.fori_loop(0, (n_seq * l_t) // nb_rows, norm_block, 0, unroll=2)

    if not single_chunk:
        pad_ref[:, lo:CONF_PAD0, :] = pad_ref[:, lo + l_t:CONF_PAD0 + l_t, :]


def _conf(proj3, hist, p, *, n_seq, l_t):
    nb, l_seq, _ = proj3.shape
    rb = min(32, l_t)
    grid = (nb // n_seq, l_seq // l_t)
    par = lambda r: pl.BlockSpec((r, D_CONV), lambda n, t: (0, 0))
    hist_spec = pl.BlockSpec((n_seq, CONF_HIST, D_CONV), lambda n, t: (n, 0, 0))
    kern = functools.partial(_conf_kernel, n_seq=n_seq, l_t=l_t, rb=rb, single_chunk=grid[1] == 1)
    return pl.pallas_call(
        kern,
        grid=grid,
        in_specs=[
            pl.BlockSpec((n_seq, l_t, D_CONV), lambda n, t: (n, t, P_GLU // D_CONV)),
            pl.BlockSpec((n_seq, l_t, D_CONV), lambda n, t: (n, t, P_GLU // D_CONV + 1)),
            hist_spec, par(CONF_KERNEL), par(1), par(1), par(1),
        ],
        out_specs=[pl.BlockSpec((n_seq * l_t, D_CONV), lambda n, t: (n * grid[1] + t, 0)), hist_spec],
        out_shape=[jax.ShapeDtypeStruct((nb * l_seq, D_CONV), BF16),
                   jax.ShapeDtypeStruct((nb, CONF_HIST, D_CONV), F32)],
        scratch_shapes=[pltpu.VMEM((n_seq, CONF_PAD0 + l_t, D_CONV), F32),
                        pltpu.VMEM((SUBLANES - 1, n_seq, CONF_PAD0 + l_t, D_CONV), F32),
                        pltpu.VMEM((n_seq * l_t, D_CONV), F32),
                        pltpu.VMEM((CONF_KERNEL, SUBLANES, D_CONV), F32)],
        compiler_params=_cparams(2),
        name="conf",
    )(proj3, proj3, hist, p["conf_w"], p["conf_b"], p["conf_ln_g"], p["conf_ln_b"])


def _attn_kernel(q_ref, k_ref, v_ref, o_ref, *, n_seq):
    scale = MEM_HEAD_DIM ** -0.5
    few_rows = q_ref.shape[1] < LANES
    for h in range(MEM_HEADS):
        cols = slice(h * MEM_HEAD_DIM, (h + 1) * MEM_HEAD_DIM)
        if few_rows:
            scores = [_dot_nt(k_ref[s, :, cols], q_ref[s, :, cols]) * scale for s in range(n_seq)]
            axis = 0
        else:
            scores = [_dot_nt(q_ref[s, :, cols], k_ref[s, :, cols]) * scale for s in range(n_seq)]
            axis = 1
        probs = []
        for sc in scores:
            e = jnp.exp(sc - jnp.max(sc, axis=axis, keepdims=True))
            probs.append((e / jnp.sum(e, axis=axis, keepdims=True)).astype(BF16))
        if few_rows:
            outs = [lax.dot_general(pr, v_ref[s, :, cols].astype(BF16), (((0,), (0,)), ((), ())),
                                    preferred_element_type=F32) for s, pr in enumerate(probs)]
        else:
            outs = [_dot(pr, v_ref[s, :, cols]) for s, pr in enumerate(probs)]
        o_ref[:, cols] = jnp.concatenate(outs, axis=0).astype(BF16)


def _attn(proj3, mem_k, mem_v, *, n_seq, l_t):
    nb, l_seq, _ = proj3.shape
    grid = (nb // n_seq, l_seq // l_t)
    kv_spec = pl.BlockSpec((n_seq, MEM_LEN, ATT_W), lambda n, t: (n, 0, 0))
    return pl.pallas_call(
        functools.partial(_attn_kernel, n_seq=n_seq),
        grid=grid,
        in_specs=[pl.BlockSpec((n_seq, l_t, ATT_W), lambda n, t: (n, t, P_Q // ATT_W)), kv_spec, kv_spec],
        out_specs=pl.BlockSpec((n_seq * l_t, ATT_W), lambda n, t: (n * grid[1] + t, 0)),
        out_shape=jax.ShapeDtypeStruct((nb * l_seq, ATT_W), BF16),
        compiler_params=_cparams(2),
        name="attn",
    )(proj3, mem_k, mem_v)


KV_HALVES = MEM_HEAD_DIM // LANES
KV_ROWS = MEM_LEN * MEM_HEADS * KV_HALVES


def _attn_rows_kernel(q_ref, k_ref, v_ref, o_ref, *, n_seq, l_t):
    scale = MEM_HEAD_DIM ** -0.5
    nq = MEM_HEADS * l_t
    col = lax.broadcasted_iota(jnp.int32, (nq, KV_ROWS), 1)
    valid = jnp.bitwise_and(col, MEM_HEADS * KV_HALVES - 1) == _iota_div((nq, KV_ROWS), 0, l_t)
    parts = []
    for s in range(n_seq):
        q2 = jnp.concatenate([q_ref[s, :, (h * KV_HALVES + c) * LANES:(h * KV_HALVES + c + 1) * LANES]
                              for c in range(KV_HALVES) for h in range(MEM_HEADS)], axis=0)
        parts.append(_dot_nt(q2, k_ref[s]))
    probs = []
    for part in parts:
        sc = (part[0:nq] + pltpu.roll(part[nq:2 * nq], KV_ROWS - MEM_HEADS, axis=1)) * scale
        sc = jnp.where(valid, sc, -jnp.inf)
        e = jnp.exp(sc - jnp.max(sc, axis=1, keepdims=True))
        pr = e / jnp.sum(e, axis=1, keepdims=True)
        probs.append(jnp.concatenate([pr, pltpu.roll(pr, MEM_HEADS, axis=1)], axis=0).astype(BF16))
    outs = [_dot(p2, v_ref[s]) for s, p2 in enumerate(probs)]
    for c in range(KV_HALVES):
        for h in range(MEM_HEADS):
            r0 = (c * MEM_HEADS + h) * l_t
            piece = jnp.concatenate([o[r0:r0 + l_t] for o in outs], axis=0)
            o_ref[:, (h * KV_HALVES + c) * LANES:(h * KV_HALVES + c + 1) * LANES] = piece.astype(BF16)


def _attn_rows(proj3, k_rows, v_rows, *, n_seq):
    nb, l_t, _ = proj3.shape
    assert l_t % SUBLANES == 0 and KV_HALVES == 2
    kv_spec = pl.BlockSpec((n_seq, KV_ROWS, LANES), lambda n: (n, 0, 0))
    return pl.pallas_call(
        functools.partial(_attn_rows_kernel, n_seq=n_seq, l_t=l_t),
        grid=(nb // n_seq,),
        in_specs=[pl.BlockSpec((n_seq, l_t, ATT_W), lambda n: (n, 0, P_Q // ATT_W)), kv_spec, kv_spec],
        out_specs=pl.BlockSpec((n_seq * l_t, ATT_W), lambda n: (n, 0)),
        out_shape=jax.ShapeDtypeStruct((nb * l_t, ATT_W), BF16),
        compiler_params=_cparams(1),
        name="attn_rows",
    )(proj3, k_rows, v_rows)


def _interleave_kv(kv):
    b = kv.shape[0]
    kv = kv.reshape(b, MEM_LEN, MEM_HEADS, KV_HALVES, LANES)
    return jnp.transpose(kv, (0, 1, 3, 2, 4)).reshape(b, KV_ROWS, LANES)


MERGE_TM = 256


def _merge_kernel(x_ref, y_ref, c_ref, o_ref, g0_ref, g1_ref, g2_ref, bg_ref, wssd_ref, wconf_ref, wmem_ref,
                  wout_ref, lg_ref, lb_ref, l1g_ref, l1b_ref, h1_ref):
    merged = _sigmoid(g0_ref[...] + bg_ref[0:1, :]) * _dot(y_ref[...], wssd_ref[...])
    merged = merged + _sigmoid(g1_ref[...] + bg_ref[1:2, :]) * _dot(c_ref[...], wconf_ref[...])
    merged = merged + _sigmoid(g2_ref[...] + bg_ref[2:3, :]) * _dot(o_ref[...], wmem_ref[...])
    mix = _dot(merged, wout_ref[...])
    h = _layernorm(x_ref[...], lg_ref[...], lb_ref[...])
    h1_ref[...] = _layernorm(ALPHA * h + mix, l1g_ref[...], l1b_ref[...])


def _merge(x2d, y2d, c2d, o2d, proj2d, p, wb):
    m = x2d.shape[0]
    tm = MERGE_TM
    rows = lambda w, blk: pl.BlockSpec((tm, w), lambda i: (i, blk))
    whole = lambda a: pl.BlockSpec(a.shape, lambda i: (0, 0), pipeline_mode=pl.Buffered(1))
    vec = pl.BlockSpec((1, D_MODEL), lambda i: (0, 0))
    g_blk = P_GATE // D_MODEL
    return pl.pallas_call(
        _merge_kernel,
        grid=(m // tm,),
        in_specs=[
            rows(D_MODEL, 0), rows(D_MODEL, 0), rows(D_CONV, 0), rows(ATT_W, 0),
            rows(D_MODEL, g_blk), rows(D_MODEL, g_blk + 1), rows(D_MODEL, g_blk + 2),
            pl.BlockSpec((3, D_MODEL), lambda i: (0, 0)),
            whole(wb["w_br_ssd"]), whole(wb["w_br_conf"]), whole(wb["w_br_mem"]), whole(wb["w_out"]),
            vec, vec, vec, vec,
        ],
        out_specs=rows(D_MODEL, 0),
        out_shape=jax.ShapeDtypeStruct((m, D_MODEL), F32),
        compiler_params=_cparams(1),
        name="merge",
    )(x2d, y2d, c2d, o2d, proj2d, proj2d, proj2d, p["b_gate"],
      wb["w_br_ssd"], wb["w_br_conf"], wb["w_br_mem"], wb["w_out"],
      p["ln_in_g"], p["ln_in_b"], p["ln1_g"], p["ln1_b"])


MLP_TM = 512
MLP_TF = 1024


def _mlp_kernel(h_ref, w1_ref, b1_ref, w2_ref, b2_ref, g_ref, b_ref, out_ref, hb_ref):
    f = pl.program_id(1)

    @pl.when(f == 0)
    def _():
        hb_ref[...] = h_ref[...].astype(BF16)
        out_ref[...] = jnp.zeros_like(out_ref)

    a = jnp.maximum(jnp.dot(hb_ref[...], w1_ref[...], preferred_element_type=F32) + b1_ref[...], 0.0)
    out_ref[...] += _dot(a * a, w2_ref[...])

    @pl.when(f == pl.num_programs(1) - 1)
    def _():
        out_ref[...] = _layernorm(ALPHA * h_ref[...] + (out_ref[...] + b2_ref[...]), g_ref[...], b_ref[...])


def _mlp(h2d, p, wb):
    m = h2d.shape[0]
    tm, tf = MLP_TM, MLP_TF
    vec = pl.BlockSpec((1, D_MODEL), lambda i, f: (0, 0))
    return pl.pallas_call(
        _mlp_kernel,
        grid=(m // tm, D_FF // tf),
        in_specs=[
            pl.BlockSpec((tm, D_MODEL), lambda i, f: (i, 0)),
            pl.BlockSpec((D_MODEL, tf), lambda i, f: (0, f)),
            pl.BlockSpec((1, tf), lambda i, f: (0, f)),
            pl.BlockSpec((tf, D_MODEL), lambda i, f: (f, 0)),
            vec, vec, vec,
        ],
        out_specs=pl.BlockSpec((tm, D_MODEL), lambda i, f: (i, 0)),
        out_shape=jax.ShapeDtypeStruct((m, D_MODEL), F32),
        scratch_shapes=[pltpu.VMEM((tm, D_MODEL), BF16)],
        compiler_params=_cparams(2),
        name="mlp",
    )(h2d, wb["w_ff1"], p["b_ff1"], wb["w_ff2"], p["b_ff2"], p["ln2_g"], p["ln2_b"])


CAST_WEIGHTS = ("w_ff1", "w_ff2", "w_br_ssd", "w_out", "w_br_conf", "w_br_mem")


def _trunk(x, ssd_hist, ssm0, conf_hist, mem_k, mem_v, p, wb, *, ssd_cfg, conf_cfg, attn_fn):
    nb, l_seq, _ = x.shape
    x2d = x.reshape(nb * l_seq, D_MODEL)
    proj2d, dt2d = _proj(x2d, p["ln_in_g"], p["ln_in_b"], p["w_in_t"])
    proj3 = proj2d.reshape(nb, l_seq, P_W)
    dt3 = dt2d.reshape(nb, l_seq, LANES)
    if wb is None:
        y, hx, hb, hc, ssm1, *cast = _ssd(proj3, dt3, ssd_hist, ssm0, p, cast_weights=[p[n] for n in CAST_WEIGHTS],
                                          **ssd_cfg)
        wb = dict(zip(CAST_WEIGHTS, cast))
    else:
        y, hx, hb, hc, ssm1 = _ssd(proj3, dt3, ssd_hist, ssm0, p, **ssd_cfg)
    c, conf_hist1 = _conf(proj3, conf_hist, p, **conf_cfg)
    o = attn_fn(proj3, mem_k, mem_v)
    h1 = _merge(x2d, y, c, o, proj2d, p, wb)
    out = _mlp(h1, p, wb).reshape(nb, l_seq, D_MODEL)
    return out, ssm1, jnp.concatenate([hx, hb, hc], axis=-1), conf_hist1, wb


def kernel(x_prompt, x_sample, mem_prompt, state_ssm, state_ssd_conv, state_conf_conv, cache_mem_k, cache_mem_v, ln_in_g, ln_in_b, w_in, b_gate, ssd_conv_w, ssd_conv_b, ssd_dt_bias, ssd_a_log, ssd_d, ssd_norm_g, conf_dw_w, conf_dw_b, conf_ln_g, conf_ln_b, w_mem_k, w_mem_v, w_br_ssd, w_br_conf, w_br_mem, w_out, ln1_g, ln1_b, w_ff1, b_ff1, w_ff2, b_ff2, ln2_g, ln2_b):
    layer = 0
    row = lambda v: v.reshape(1, -1)
    lane_pad = lambda v: jnp.pad(v.reshape(1, -1), ((0, 0), (0, LANES - v.size)))
    p = {
        "ln_in_g": row(ln_in_g), "ln_in_b": row(ln_in_b),
        "w_in_t": jnp.swapaxes(w_in[layer], 0, 1),
        "b_gate": b_gate[layer],
        "conv_w": ssd_conv_w[layer], "conv_b": row(ssd_conv_b[layer]),
        "dt_bias": lane_pad(ssd_dt_bias[layer]), "a_log": lane_pad(ssd_a_log[layer]),
        "d_skip": row(jnp.repeat(ssd_d[layer], SSD_HEADDIM)), "norm_g": row(ssd_norm_g[layer]),
        "head_onehot": _head_onehot(),
        "conf_w": conf_dw_w[layer], "conf_b": row(conf_dw_b[layer]),
        "conf_ln_g": row(conf_ln_g[layer]), "conf_ln_b": row(conf_ln_b[layer]),
        "w_br_ssd": w_br_ssd[layer], "w_br_conf": w_br_conf[layer],
        "w_br_mem": w_br_mem[layer], "w_out": w_out[layer],
        "ln1_g": row(ln1_g[layer]), "ln1_b": row(ln1_b[layer]),
        "w_ff1": w_ff1[layer], "b_ff1": row(b_ff1[layer]),
        "w_ff2": w_ff2[layer], "b_ff2": row(b_ff2[layer]),
        "ln2_g": row(ln2_g[layer]), "ln2_b": row(ln2_b[layer]),
    }
    n_p, l_p, _ = x_prompt.shape
    n_s, l_s, _ = x_sample.shape

    mem2d = mem_prompt.reshape(n_p * MEM_LEN, D_MODEL)
    p_mem_k = _matmul(mem2d, w_mem_k[layer]).reshape(n_p, MEM_LEN, ATT_W)
    p_mem_v = _matmul(mem2d, w_mem_v[layer]).reshape(n_p, MEM_LEN, ATT_W)
    y_prompt, p_ssm, p_ssd_conv, p_cc, wb = _trunk(
        x_prompt,
        jnp.zeros((n_p, SSD_CONV - 1, CONV_DIM), F32),
        jnp.zeros((n_p, SSD_HEADS, SSD_HEADDIM, SSD_STATE), F32),
        jnp.zeros((n_p, CONF_HIST, D_CONV), F32),
        p_mem_k, p_mem_v, p, None,
        ssd_cfg=dict(n_seq=1, l_q=SSD_ROWS, g_step=SSD_GROUPS),
        conf_cfg=dict(n_seq=1, l_t=512),
        attn_fn=functools.partial(_attn, n_seq=1, l_t=512))

    y_sample, s_ssm, s_ssd_conv, s_cc, _ = _trunk(
        x_sample, state_ssd_conv[layer], state_ssm[layer], state_conf_conv[layer],
        _interleave_kv(cache_mem_k[layer]), _interleave_kv(cache_mem_v[layer]), p, wb,
        ssd_cfg=dict(n_seq=SSD_ROWS // l_s, l_q=l_s, g_step=4),
        conf_cfg=dict(n_seq=16, l_t=l_s),
        attn_fn=functools.partial(_attn_rows, n_seq=8))

    kv_shape = (DEPTH, n_p, MEM_LEN, MEM_HEADS, MEM_HEAD_DIM)
    return (y_prompt, y_sample, p_ssm[None], p_ssd_conv[None], p_cc[None],
            p_mem_k.reshape(kv_shape), p_mem_v.reshape(kv_shape),
            s_ssm[None], s_ssd_conv[None], s_cc[None])
```

```python
import functools
import math

import jax
import jax.numpy as jnp
from jax import lax
from jax.experimental import pallas as pl
from jax.experimental.pallas import tpu as pltpu

F32 = jnp.float32
BF16 = jnp.bfloat16

D_MODEL = 2048
SSD_HEADDIM = 64
SSD_HEADS = 32
SSD_GROUPS = 8
SSD_HG = 4
SSD_STATE = 128
SSD_CONV = 4
CONV_DIM = 4096
D_CONV = 1024
CONF_KERNEL = 31
MEM_LEN = 256
MEM_HEADS = 4
MEM_HEAD_DIM = 256
ATT_W = 1024
D_FF = 8192
DEPTH = 1
ALPHA = (2.0 * DEPTH) ** 0.25
EPS = 1e-5

O_XBC = 2048
O_DT = 6144
O_GLU = 6176
P_GATE = 6144
P_GLU = 12288
P_Q = 14336
P_W = 15360

GROUP_W = SSD_HG * SSD_HEADDIM
SSD_ROWS = 128
LANES = 128
BF16_ROWS = 16
VMEM_LIMIT = 56 * 1024 * 1024


def _cparams(n_axes):
    return pltpu.CompilerParams(dimension_semantics=("arbitrary",) * n_axes, vmem_limit_bytes=VMEM_LIMIT)


def _layernorm(x, g, b):
    mu = jnp.mean(x, axis=-1, keepdims=True)
    xc = x - mu
    var = jnp.mean(xc * xc, axis=-1, keepdims=True)
    return xc * lax.rsqrt(var + EPS) * g + b


def _sigmoid(x):
    return 0.5 * jnp.tanh(0.5 * x) + 0.5


def _silu(x):
    return x * _sigmoid(x)


def _dot(a, b):
    return jnp.dot(a.astype(BF16), b.astype(BF16), preferred_element_type=F32)


def _dot_nt(a, b):
    return lax.dot_general(a.astype(BF16), b.astype(BF16), (((1,), (1,)), ((), ())), preferred_element_type=F32)


def _iota_div(shape, axis, divisor):
    shift = divisor.bit_length() - 1
    assert 1 << shift == divisor
    return lax.shift_right_logical(lax.broadcasted_iota(jnp.int32, shape, axis), shift)


PROJ_TM = 1024
PROJ_TN = 1024
PROJ_LN_ROWS = 256
PROJ_NA = O_DT // PROJ_TN
PROJ_NB = (P_W - O_DT) // PROJ_TN


def _proj_kernel(x_ref, g_ref, b_ref, wt_ref, wdt_ref, out_ref, dt_ref, xn_ref):
    @pl.when(pl.program_id(1) == 0)
    def _():
        def ln_rows(r, carry):
            r0 = pl.multiple_of(r * PROJ_LN_ROWS, PROJ_LN_ROWS)
            xn = _layernorm(x_ref[pl.ds(r0, PROJ_LN_ROWS), :], g_ref[...], b_ref[...])
            xn_ref[pl.ds(r0, PROJ_LN_ROWS), :] = xn.astype(BF16)
            return carry
        lax.fori_loop(0, x_ref.shape[0] // PROJ_LN_ROWS, ln_rows, 0)
        dt_ref[...] = _dot_nt(xn_ref[...], wdt_ref[...])

    out_ref[...] = _dot_nt(xn_ref[...], wt_ref[...])


def _proj_w_row(j):
    t = BF16_ROWS
    assert O_GLU % t == 0 and PROJ_TN % t == 0
    return t * jnp.where(j < PROJ_NA, j * (PROJ_TN // t), O_GLU // t + (j - PROJ_NA) * (PROJ_TN // t))


def _proj_out_tile(j):
    jb = j - PROJ_NA
    n_glu_q = (P_W - P_GLU) // PROJ_TN
    tile_b = jnp.where(jb < n_glu_q, P_GLU // PROJ_TN + jb, P_GATE // PROJ_TN + jb - n_glu_q)
    return jnp.where(j < PROJ_NA, j, tile_b)


def _proj(x2d, ln_g, ln_b, w_in_t):
    m = x2d.shape[0]
    tm = min(PROJ_TM, m)
    grid = (m // tm, PROJ_NA + PROJ_NB)
    return pl.pallas_call(
        _proj_kernel,
        grid=grid,
        in_specs=[
            pl.BlockSpec((tm, D_MODEL), lambda i, j: (i, 0), pipeline_mode=pl.Buffered(1)),
            pl.BlockSpec((1, D_MODEL), lambda i, j: (0, 0)),
            pl.BlockSpec((1, D_MODEL), lambda i, j: (0, 0)),
            pl.BlockSpec((pl.Element(PROJ_TN), pl.Element(D_MODEL)), lambda i, j: (_proj_w_row(j), 0)),
            pl.BlockSpec((LANES, D_MODEL), lambda i, j: (O_DT // LANES, 0)),
        ],
        out_specs=[
            pl.BlockSpec((tm, PROJ_TN), lambda i, j: (i, _proj_out_tile(j))),
            pl.BlockSpec((tm, LANES), lambda i, j: (i, 0)),
        ],
        out_shape=[jax.ShapeDtypeStruct((m, P_W), F32), jax.ShapeDtypeStruct((m, LANES), F32)],
        scratch_shapes=[pltpu.VMEM((tm, D_MODEL), BF16)],
        compiler_params=_cparams(2),
        name="proj",
    )(x2d, ln_g, ln_b, w_in_t, w_in_t)


def _matmul_kernel(x_ref, w_ref, out_ref):
    out_ref[...] = _dot(x_ref[...], w_ref[...])


def _matmul(x2d, w, tn=512):
    m, k = x2d.shape
    n = w.shape[1]
    return pl.pallas_call(
        _matmul_kernel,
        grid=(n // tn,),
        in_specs=[pl.BlockSpec((m, k), lambda j: (0, 0)), pl.BlockSpec((k, tn), lambda j: (0, j))],
        out_specs=pl.BlockSpec((m, tn), lambda j: (0, j)),
        out_shape=jax.ShapeDtypeStruct((m, n), F32),
        compiler_params=_cparams(1),
        name="memkv",
    )(x2d, w)


def _split3(v):
    hi = v.astype(BF16)
    r1 = v - hi.astype(F32)
    mid = r1.astype(BF16)
    lo = (r1 - mid.astype(F32)).astype(BF16)
    return jnp.concatenate([hi, mid, lo], axis=1)


def _expand_heads(pieces, onehot3):
    return jnp.dot(pieces, onehot3, preferred_element_type=F32)


def _conv4(pad_ref, u, hist_out_ref, w_ref, b_ref, l_q):
    pad_ref[:, 8:8 + l_q, :] = u
    full = pad_ref[...]
    acc = b_ref[...] + w_ref[SSD_CONV - 1:SSD_CONV, :] * u
    for d in range(1, SSD_CONV):
        shifted = pltpu.roll(full, d, axis=1)[:, 8:8 + l_q, :]
        acc = acc + w_ref[SSD_CONV - 1 - d:SSD_CONV - d, :] * shifted
    last3 = pad_ref[:, 5 + l_q:8 + l_q, :]
    hist_out_ref[...] = last3
    pad_ref[:, 5:8, :] = last3
    return _silu(acc)


def _ssd_kernel(z_ref, x_ref, bm_ref, cm_ref, dt_ref, hx_ref, hb_ref, hc_ref, h0_ref,
                wx_ref, wb_ref, wc_ref, bx_ref, bb_ref, bc_ref, dtb_ref, alog_ref, dsk_ref, ng_ref, oh_ref,
                y_ref, ox_ref, ob_ref, oc_ref, h_ref,
                px_ref, pb_ref, pc_ref, yoff_ref, *, n_seq, l_q, g_step, single_chunk):
    rows = n_seq * l_q
    gs = pl.program_id(1)
    first = pl.program_id(2) == 0

    h_in_ref = h0_ref if single_chunk else h_ref

    def load_history():
        px_ref[:, 5:8, :] = hx_ref[...]
        pb_ref[:, 5:8, :] = hb_ref[...]
        pc_ref[:, 5:8, :] = hc_ref[...]

    if single_chunk:
        load_history()
    else:
        @pl.when(first)
        def _():
            load_history()
            h_ref[...] = h0_ref[...]

    xs = _conv4(px_ref, x_ref[...], ox_ref, wx_ref, bx_ref, l_q).reshape(rows, g_step * GROUP_W)
    bm = _conv4(pb_ref, bm_ref[...], ob_ref, wb_ref, bb_ref, l_q).reshape(rows, g_step * SSD_STATE)
    cm = _conv4(pc_ref, cm_ref[...], oc_ref, wc_ref, bc_ref, l_q).reshape(rows, g_step * SSD_STATE)

    lane = lax.broadcasted_iota(jnp.int32, (1, LANES), 1)
    dt_raw = dt_ref[...].reshape(rows, LANES) + dtb_ref[...]
    dt = jnp.maximum(dt_raw, 0.0) + jnp.log1p(jnp.exp(-jnp.abs(dt_raw)))
    dt = jnp.where(lane < SSD_HEADS, dt, 0.0)
    a = dt * (-jnp.exp(alog_ref[...]))
    if g_step != SSD_GROUPS:
        src = jnp.bitwise_and(lax.broadcasted_iota(jnp.int32, (3 * LANES, LANES), 0), LANES - 1)
        dst = lax.broadcasted_iota(jnp.int32, (3 * LANES, LANES), 1)
        pick = (src == dst + gs * (g_step * SSD_HG)).astype(BF16)
        dt = _expand_heads(_split3(dt), pick)
        a = _expand_heads(_split3(a), pick)
    rq = lax.broadcasted_iota(jnp.int32, (rows, rows), 0)
    ck = lax.broadcasted_iota(jnp.int32, (rows, rows), 1)
    same = _iota_div((rows, rows), 0, l_q) == _iota_div((rows, rows), 1, l_q)
    causal = jnp.logical_and(same, ck <= rq)
    sums = jnp.dot(jnp.concatenate([causal, same], axis=0).astype(BF16), _split3(a), preferred_element_type=F32)
    sums = sums[:, 0:LANES] + sums[:, LANES:2 * LANES] + sums[:, 2 * LANES:3 * LANES]
    acs = sums[0:rows]
    tot = sums[rows:2 * rows]
    acs_t = acs.T
    seq_of_col = _iota_div((1, rows), 1, l_q)

    dt_p = _split3(dt)
    to_end_p = _split3(jnp.exp(tot - acs))
    from_start_p = _split3(jnp.exp(acs))
    head_of_col = _iota_div((1, GROUP_W), 1, SSD_HEADDIM)
    groups = range(g_step)
    cols_x = [slice(gi * GROUP_W, (gi + 1) * GROUP_W) for gi in groups]
    cols_n = [slice(gi * SSD_STATE, (gi + 1) * SSD_STATE) for gi in groups]

    dt_x = [_expand_heads(dt_p, oh_ref[:, c]) for c in cols_x]
    to_end_x = [_expand_heads(to_end_p, oh_ref[:, c]) for c in cols_x]
    from_start_x = [_expand_heads(from_start_p, oh_ref[:, c]) for c in cols_x]
    cbm = [jnp.where(causal, _dot_nt(cm[:, c], bm[:, c]), 0.0) for c in cols_n]

    m_cat, x_cat, xd_t = [], [], []
    for gi in groups:
        lane0 = gi * SSD_HG
        xdt = xs[:, cols_x[gi]] * dt_x[gi]
        xdt_b = xdt.astype(BF16)
        m_heads, x_heads = [], []
        for h in range(SSD_HG):
            diff = acs[:, lane0 + h:lane0 + h + 1] - acs_t[lane0 + h:lane0 + h + 1, :]
            m_heads.append((cbm[gi] * jnp.exp(jnp.where(causal, diff, 0.0))).astype(BF16))
            x_heads.append(jnp.where(head_of_col == h, xdt_b, jnp.zeros_like(xdt_b)))
        m_cat.append(jnp.concatenate(m_heads, axis=1))
        x_cat.append(jnp.concatenate(x_heads, axis=0))
        xd_t.append((xdt * to_end_x[gi]).T)

    y_in = [jnp.dot(m_cat[gi], x_cat[gi], preferred_element_type=F32) for gi in groups]

    for gi in groups:
        lane0 = gi * SSD_HG
        bg = bm[:, cols_n[gi]]
        cg = cm[:, cols_n[gi]]
        for s in range(n_seq):
            h_old = h_in_ref[s, lane0:lane0 + SSD_HG].reshape(GROUP_W, SSD_STATE)
            yoff_ref[s * l_q:(s + 1) * l_q, cols_x[gi]] = _dot_nt(cg[s * l_q:(s + 1) * l_q, :], h_old)
            xd_s = xd_t[gi] if n_seq == 1 else jnp.where(seq_of_col == s, xd_t[gi], 0.0)
            upd = _dot(xd_s, bg)
            dec = jnp.concatenate(
                [jnp.broadcast_to(jnp.exp(tot[s * l_q:s * l_q + 1, lane0 + h:lane0 + h + 1]),
                                  (SSD_HEADDIM, SSD_STATE)) for h in range(SSD_HG)], axis=0)
            h_ref[s, lane0:lane0 + SSD_HG] = (h_old * dec + upd).reshape(SSD_HG, SSD_HEADDIM, SSD_STATE)

    for gi in groups:
        c = cols_x[gi]
        y = y_in[gi] + yoff_ref[:, c] * from_start_x[gi] + dsk_ref[:, c] * xs[:, c]
        v = y * _silu(z_ref[:, :, c].reshape(rows, GROUP_W))
        v = v * lax.rsqrt(jnp.mean(v * v, axis=-1, keepdims=True) + EPS)
        y_ref[:, c] = (v * ng_ref[:, c]).astype(BF16)


def _head_onehot():
    lane = jnp.arange(3 * LANES)[:, None] % LANES
    head = jnp.arange(D_MODEL)[None, :] // SSD_HEADDIM
    return (lane == head).astype(BF16)


SSD_N_IN = 20
SSD_N_OUT = 5


def _ssd_cast_kernel(*refs, n_cast, **kw):
    ssd_in = refs[:SSD_N_IN]
    cast_in = refs[SSD_N_IN:SSD_N_IN + n_cast]
    ssd_out = refs[SSD_N_IN + n_cast:SSD_N_IN + n_cast + SSD_N_OUT]
    cast_out = refs[SSD_N_IN + n_cast + SSD_N_OUT:SSD_N_IN + 2 * n_cast + SSD_N_OUT]
    scratch = refs[SSD_N_IN + 2 * n_cast + SSD_N_OUT:]
    for src, dst in zip(cast_in, cast_out):
        dst[...] = src[...].astype(BF16)
    _ssd_kernel(*ssd_in, *ssd_out, *scratch, **kw)


def _ssd(proj3, dt3, hist, h0, p, *, n_seq, l_q, g_step, cast_weights=()):
    nb, l_seq, _ = proj3.shape
    wx = g_step * GROUP_W
    wn = g_step * SSD_STATE
    grid = (nb // n_seq, SSD_GROUPS // g_step, l_seq // l_q)
    ob = (O_XBC + D_MODEL) // wn
    oc = ob + (SSD_GROUPS * SSD_STATE) // wn
    cb = D_MODEL // wn
    cc = cb + (SSD_GROUPS * SSD_STATE) // wn

    def rows_spec(width, first_block):
        return pl.BlockSpec((n_seq, l_q, width), lambda n, g, c: (n, c, first_block + g))

    def hist_spec(width, first_block):
        return pl.BlockSpec((n_seq, SSD_CONV - 1, width), lambda n, g, c: (n, 0, first_block + g))

    def par_spec(r, width, first_block):
        return pl.BlockSpec((r, width), lambda n, g, c: (0, first_block + g))

    state_spec = pl.BlockSpec((n_seq, g_step * SSD_HG, SSD_HEADDIM, SSD_STATE), lambda n, g, c: (n, g, 0, 0))
    full_lane = pl.BlockSpec((1, LANES), lambda n, g, c: (0, 0))
    n_steps = grid[0] * grid[1] * grid[2]
    def cast_specs():
        specs = []
        for w in cast_weights:
            assert w.shape[0] % (n_steps * BF16_ROWS) == 0, w.shape
            specs.append(pl.BlockSpec((w.shape[0] // n_steps, w.shape[1]),
                                      lambda n, g, c: ((n * grid[1] + g) * grid[2] + c, 0)))
        return specs
    kern = functools.partial(_ssd_cast_kernel, n_cast=len(cast_weights), n_seq=n_seq, l_q=l_q, g_step=g_step,
                             single_chunk=grid[2] == 1)
    return pl.pallas_call(
        kern,
        grid=grid,
        in_specs=[
            rows_spec(wx, 0), rows_spec(wx, O_XBC // wx), rows_spec(wn, ob), rows_spec(wn, oc),
            pl.BlockSpec((n_seq, l_q, LANES), lambda n, g, c: (n, c, 0)),
            hist_spec(wx, 0), hist_spec(wn, cb), hist_spec(wn, cc),
            state_spec,
            par_spec(SSD_CONV, wx, 0), par_spec(SSD_CONV, wn, cb), par_spec(SSD_CONV, wn, cc),
            par_spec(1, wx, 0), par_spec(1, wn, cb), par_spec(1, wn, cc),
            full_lane, full_lane, par_spec(1, wx, 0), par_spec(1, wx, 0),
            pl.BlockSpec((3 * LANES, wx), lambda n, g, c: (0, 0)),
        ] + cast_specs(),
        out_specs=[
            pl.BlockSpec((SSD_ROWS, wx), lambda n, g, c: (n * grid[2] + c, g)),
            hist_spec(wx, 0), hist_spec(wn, 0), hist_spec(wn, 0),
            state_spec,
        ] + cast_specs(),
        out_shape=[
            jax.ShapeDtypeStruct((nb * l_seq, D_MODEL), BF16),
            jax.ShapeDtypeStruct((nb, SSD_CONV - 1, D_MODEL), F32),
            jax.ShapeDtypeStruct((nb, SSD_CONV - 1, SSD_GROUPS * SSD_STATE), F32),
            jax.ShapeDtypeStruct((nb, SSD_CONV - 1, SSD_GROUPS * SSD_STATE), F32),
            jax.ShapeDtypeStruct((nb, SSD_HEADS, SSD_HEADDIM, SSD_STATE), F32),
        ] + [jax.ShapeDtypeStruct(w.shape, BF16) for w in cast_weights],
        scratch_shapes=[
            pltpu.VMEM((n_seq, 8 + l_q, wx), F32),
            pltpu.VMEM((n_seq, 8 + l_q, wn), F32),
            pltpu.VMEM((n_seq, 8 + l_q, wn), F32),
            pltpu.VMEM((SSD_ROWS, wx), F32),
        ],
        compiler_params=_cparams(3),
        name="ssd",
    )(proj3, proj3, proj3, proj3, dt3, hist, hist, hist, h0,
      p["conv_w"], p["conv_w"], p["conv_w"], p["conv_b"], p["conv_b"], p["conv_b"],
      p["dt_bias"], p["a_log"], p["d_skip"], p["norm_g"], p["head_onehot"], *cast_weights)


CONF_HIST = CONF_KERNEL - 1
CONF_PAD0 = 32


SUBLANES = 8
CONF_NORM_ROWS = 128


def _conf_kernel(a_ref, b_ref, hist_ref, w_ref, bias_ref, g_ref, beta_ref, c_ref, hist_out_ref,
                 pad_ref, ph_ref, cf_ref, wrep_ref, *, n_seq, l_t, rb, single_chunk):
    first = pl.program_id(1) == 0
    lo = CONF_PAD0 - CONF_HIST
    rows = CONF_PAD0 + l_t

    if single_chunk:
        pad_ref[:, lo:CONF_PAD0, :] = hist_ref[...]
    else:
        @pl.when(first)
        def _():
            pad_ref[:, lo:CONF_PAD0, :] = hist_ref[...]

    pad_ref[:, CONF_PAD0:CONF_PAD0 + l_t, :] = a_ref[...] * _sigmoid(b_ref[...])
    hist_out_ref[...] = pad_ref[:, lo + l_t:CONF_PAD0 + l_t, :]

    full = pad_ref[...]
    for p in range(1, SUBLANES):
        ph_ref[p - 1] = pltpu.roll(full, rows - p, axis=1)

    for j in range(CONF_KERNEL):
        wrep_ref[j] = jnp.broadcast_to(w_ref[j:j + 1, :], (SUBLANES, D_CONV))

    n_rb = l_t // rb

    def conv_block(i, carry):
        s = i // n_rb
        r0 = (i % n_rb) * rb
        n_tiles = rb // SUBLANES
        out_row = pl.multiple_of(s * l_t + r0, SUBLANES)
        for c0 in range(0, D_CONV, LANES):
            lanes = slice(c0, c0 + LANES)
            acc = [jnp.broadcast_to(bias_ref[:, lanes], (SUBLANES, LANES))] * n_tiles
            for j in range(CONF_KERNEL):
                p = (lo + j) % SUBLANES
                src = pad_ref if p == 0 else ph_ref.at[p - 1]
                wj = wrep_ref[j, :, lanes]
                for k in range(n_tiles):
                    start = pl.multiple_of(r0 + (lo + j - p) + k * SUBLANES, SUBLANES)
                    acc[k] = acc[k] + wj * src[s, pl.ds(start, SUBLANES), lanes]
            cf_ref[pl.ds(out_row, rb), lanes] = jnp.concatenate(acc, axis=0)
        return carry

    lax.fori_loop(0, n_seq * n_rb, conv_block, 0, unroll=2)

    nb_rows = min(CONF_NORM_ROWS, n_seq * l_t)

    def norm_block(i, carry):
        r0 = pl.multiple_of(i * nb_rows, nb_rows)
        v = _silu(_layernorm(cf_ref[pl.ds(r0, nb_rows), :], g_ref[...], beta_ref[...]))
        c_ref[pl.ds(r0, nb_rows), :] = v.astype(BF16)
        return carry

    lax.fori_loop(0, (n_seq * l_t) // nb_rows, norm_block, 0, unroll=2)

    if not single_chunk:
        pad_ref[:, lo:CONF_PAD0, :] = pad_ref[:, lo + l_t:CONF_PAD0 + l_t, :]


def _conf(proj3, hist, p, *, n_seq, l_t):
    nb, l_seq, _ = proj3.shape
    rb = min(32, l_t)
    grid = (nb // n_seq, l_seq // l_t)
    par = lambda r: pl.BlockSpec((r, D_CONV), lambda n, t: (0, 0))
    hist_spec = pl.BlockSpec((n_seq, CONF_HIST, D_CONV), lambda n, t: (n, 0, 0))
    kern = functools.partial(_conf_kernel, n_seq=n_seq, l_t=l_t, rb=rb, single_chunk=grid[1] == 1)
    return pl.pallas_call(
        kern,
        grid=grid,
        in_specs=[
            pl.BlockSpec((n_seq, l_t, D_CONV), lambda n, t: (n, t, P_GLU // D_CONV)),
            pl.BlockSpec((n_seq, l_t, D_CONV), lambda n, t: (n, t, P_GLU // D_CONV + 1)),
            hist_spec, par(CONF_KERNEL), par(1), par(1), par(1),
        ],
        out_specs=[pl.BlockSpec((n_seq * l_t, D_CONV), lambda n, t: (n * grid[1] + t, 0)), hist_spec],
        out_shape=[jax.ShapeDtypeStruct((nb * l_seq, D_CONV), BF16),
                   jax.ShapeDtypeStruct((nb, CONF_HIST, D_CONV), F32)],
        scratch_shapes=[pltpu.VMEM((n_seq, CONF_PAD0 + l_t, D_CONV), F32),
                        pltpu.VMEM((SUBLANES - 1, n_seq, CONF_PAD0 + l_t, D_CONV), F32),
                        pltpu.VMEM((n_seq * l_t, D_CONV), F32),
                        pltpu.VMEM((CONF_KERNEL, SUBLANES, D_CONV), F32)],
        compiler_params=_cparams(2),
        name="conf",
    )(proj3, proj3, hist, p["conf_w"], p["conf_b"], p["conf_ln_g"], p["conf_ln_b"])


def _attn_kernel(q_ref, k_ref, v_ref, o_ref, *, n_seq):
    scale = MEM_HEAD_DIM ** -0.5
    few_rows = q_ref.shape[1] < LANES
    for h in range(MEM_HEADS):
        cols = slice(h * MEM_HEAD_DIM, (h + 1) * MEM_HEAD_DIM)
        if few_rows:
            scores = [_dot_nt(k_ref[s, :, cols], q_ref[s, :, cols]) * scale for s in range(n_seq)]
            axis = 0
        else:
            scores = [_dot_nt(q_ref[s, :, cols], k_ref[s, :, cols]) * scale for s in range(n_seq)]
            axis = 1
        probs = []
        for sc in scores:
            e = jnp.exp(sc - jnp.max(sc, axis=axis, keepdims=True))
            probs.append((e / jnp.sum(e, axis=axis, keepdims=True)).astype(BF16))
        if few_rows:
            outs = [lax.dot_general(pr, v_ref[s, :, cols].astype(BF16), (((0,), (0,)), ((), ())),
                                    preferred_element_type=F32) for s, pr in enumerate(probs)]
        else:
            outs = [_dot(pr, v_ref[s, :, cols]) for s, pr in enumerate(probs)]
        o_ref[:, cols] = jnp.concatenate(outs, axis=0).astype(BF16)


def _attn(proj3, mem_k, mem_v, *, n_seq, l_t):
    nb, l_seq, _ = proj3.shape
    grid = (nb // n_seq, l_seq // l_t)
    kv_spec = pl.BlockSpec((n_seq, MEM_LEN, ATT_W), lambda n, t: (n, 0, 0))
    return pl.pallas_call(
        functools.partial(_attn_kernel, n_seq=n_seq),
        grid=grid,
        in_specs=[pl.BlockSpec((n_seq, l_t, ATT_W), lambda n, t: (n, t, P_Q // ATT_W)), kv_spec, kv_spec],
        out_specs=pl.BlockSpec((n_seq * l_t, ATT_W), lambda n, t: (n * grid[1] + t, 0)),
        out_shape=jax.ShapeDtypeStruct((nb * l_seq, ATT_W), BF16),
        compiler_params=_cparams(2),
        name="attn",
    )(proj3, mem_k, mem_v)


KV_HALVES = MEM_HEAD_DIM // LANES
KV_ROWS = MEM_LEN * MEM_HEADS * KV_HALVES


def _rows_scores(q_ref, k_ref, s):
    q2 = jnp.concatenate([q_ref[s, :, (h * KV_HALVES + c) * LANES:(h * KV_HALVES + c + 1) * LANES]
                          for c in range(KV_HALVES) for h in range(MEM_HEADS)], axis=0)
    return _dot_nt(q2, k_ref[s])


def _rows_probs(part, l_t):
    nq = MEM_HEADS * l_t
    col = lax.broadcasted_iota(jnp.int32, (nq, KV_ROWS), 1)
    valid = jnp.bitwise_and(col, MEM_HEADS * KV_HALVES - 1) == _iota_div((nq, KV_ROWS), 0, l_t)
    sc = (part[0:nq] + pltpu.roll(part[nq:2 * nq], KV_ROWS - MEM_HEADS, axis=1)) * (MEM_HEAD_DIM ** -0.5)
    sc = jnp.where(valid, sc, -jnp.inf)
    e = jnp.exp(sc - jnp.max(sc, axis=1, keepdims=True))
    pr = e / jnp.sum(e, axis=1, keepdims=True)
    return jnp.concatenate([pr, pltpu.roll(pr, MEM_HEADS, axis=1)], axis=0).astype(BF16)


def _rows_store(o_ref, outs, l_t):
    for c in range(KV_HALVES):
        for h in range(MEM_HEADS):
            r0 = (c * MEM_HEADS + h) * l_t
            piece = jnp.concatenate([o[r0:r0 + l_t] for o in outs], axis=0)
            o_ref[:, (h * KV_HALVES + c) * LANES:(h * KV_HALVES + c + 1) * LANES] = piece.astype(o_ref.dtype)


def _attn_rows_kernel(q_ref, k_ref, v_ref, o_ref, *, n_seq, l_t):
    parts = [_rows_scores(q_ref, k_ref, s) for s in range(n_seq)]
    probs = [_rows_probs(part, l_t) for part in parts]
    outs = [_dot(p2, v_ref[s]) for s, p2 in enumerate(probs)]
    _rows_store(o_ref, outs, l_t)


def _attn_rows(proj3, k_rows, v_rows, *, n_seq):
    nb, l_t, _ = proj3.shape
    assert l_t % SUBLANES == 0 and KV_HALVES == 2
    kv_spec = pl.BlockSpec((n_seq, KV_ROWS, LANES), lambda n: (n, 0, 0))
    return pl.pallas_call(
        functools.partial(_attn_rows_kernel, n_seq=n_seq, l_t=l_t),
        grid=(nb // n_seq,),
        in_specs=[pl.BlockSpec((n_seq, l_t, ATT_W), lambda n: (n, 0, P_Q // ATT_W)), kv_spec, kv_spec],
        out_specs=pl.BlockSpec((n_seq * l_t, ATT_W), lambda n: (n, 0)),
        out_shape=jax.ShapeDtypeStruct((nb * l_t, ATT_W), BF16),
        compiler_params=_cparams(1),
        name="attn_rows",
    )(proj3, k_rows, v_rows)


def _interleave_kv(kv):
    b = kv.shape[0]
    kv = kv.reshape(b, MEM_LEN, MEM_HEADS, KV_HALVES, LANES)
    return jnp.transpose(kv, (0, 1, 3, 2, 4)).reshape(b, KV_ROWS, LANES)


MERGE_TM = 256


def _merge_kernel(x_ref, y_ref, c_ref, o_ref, g0_ref, g1_ref, g2_ref, bg_ref, wssd_ref, wconf_ref, wmem_ref,
                  wout_ref, lg_ref, lb_ref, l1g_ref, l1b_ref, h1_ref):
    merged = _sigmoid(g0_ref[...] + bg_ref[0:1, :]) * _dot(y_ref[...], wssd_ref[...])
    merged = merged + _sigmoid(g1_ref[...] + bg_ref[1:2, :]) * _dot(c_ref[...], wconf_ref[...])
    merged = merged + _sigmoid(g2_ref[...] + bg_ref[2:3, :]) * _dot(o_ref[...], wmem_ref[...])
    mix = _dot(merged, wout_ref[...])
    h = _layernorm(x_ref[...], lg_ref[...], lb_ref[...])
    h1_ref[...] = _layernorm(ALPHA * h + mix, l1g_ref[...], l1b_ref[...])


def _merge(x2d, y2d, c2d, o2d, proj2d, p, wb):
    m = x2d.shape[0]
    tm = MERGE_TM
    rows = lambda w, blk: pl.BlockSpec((tm, w), lambda i: (i, blk))
    whole = lambda a: pl.BlockSpec(a.shape, lambda i: (0, 0), pipeline_mode=pl.Buffered(1))
    vec = pl.BlockSpec((1, D_MODEL), lambda i: (0, 0))
    g_blk = P_GATE // D_MODEL
    return pl.pallas_call(
        _merge_kernel,
        grid=(m // tm,),
        in_specs=[
            rows(D_MODEL, 0), rows(D_MODEL, 0), rows(D_CONV, 0), rows(ATT_W, 0),
            rows(D_MODEL, g_blk), rows(D_MODEL, g_blk + 1), rows(D_MODEL, g_blk + 2),
            pl.BlockSpec((3, D_MODEL), lambda i: (0, 0)),
            whole(wb["w_br_ssd"]), whole(wb["w_br_conf"]), whole(wb["w_br_mem"]), whole(wb["w_out"]),
            vec, vec, vec, vec,
        ],
        out_specs=rows(D_MODEL, 0),
        out_shape=jax.ShapeDtypeStruct((m, D_MODEL), F32),
        compiler_params=_cparams(1),
        name="merge",
    )(x2d, y2d, c2d, o2d, proj2d, proj2d, proj2d, p["b_gate"],
      wb["w_br_ssd"], wb["w_br_conf"], wb["w_br_mem"], wb["w_out"],
      p["ln_in_g"], p["ln_in_b"], p["ln1_g"], p["ln1_b"])


MLP_TM = 512
MLP_TF = 1024


def _mlp_kernel(h_ref, w1_ref, b1_ref, w2_ref, b2_ref, g_ref, b_ref, out_ref, hb_ref):
    f = pl.program_id(1)

    @pl.when(f == 0)
    def _():
        hb_ref[...] = h_ref[...].astype(BF16)
        out_ref[...] = jnp.zeros_like(out_ref)

    a = jnp.maximum(jnp.dot(hb_ref[...], w1_ref[...], preferred_element_type=F32) + b1_ref[...], 0.0)
    out_ref[...] += _dot(a * a, w2_ref[...])

    @pl.when(f == pl.num_programs(1) - 1)
    def _():
        out_ref[...] = _layernorm(ALPHA * h_ref[...] + (out_ref[...] + b2_ref[...]), g_ref[...], b_ref[...])


def _mlp_attn_kernel(h_ref, w1_ref, b1_ref, w2_ref, b2_ref, g_ref, b_ref, q_ref, k_ref, v_ref,
                     out_ref, o_ref, hb_ref):
    f = pl.program_id(1)
    l_t = q_ref.shape[1]

    @pl.when(f == 0)
    def _():
        hb_ref[...] = h_ref[...].astype(BF16)
        out_ref[...] = jnp.zeros_like(out_ref)

    part = _rows_scores(q_ref, k_ref, 0)
    a = jnp.maximum(jnp.dot(hb_ref[...], w1_ref[...], preferred_element_type=F32) + b1_ref[...], 0.0)
    side = _dot(_rows_probs(part, l_t), v_ref[0])
    out_ref[...] += _dot(a * a, w2_ref[...])
    _rows_store(o_ref, [side], l_t)

    @pl.when(f == pl.num_programs(1) - 1)
    def _():
        out_ref[...] = _layernorm(ALPHA * h_ref[...] + (out_ref[...] + b2_ref[...]), g_ref[...], b_ref[...])


def _mlp(h2d, p, wb, side=None):
    m = h2d.shape[0]
    tm, tf = MLP_TM, MLP_TF
    grid = (m // tm, D_FF // tf)
    vec = pl.BlockSpec((1, D_MODEL), lambda i, f: (0, 0))
    in_specs = [
        pl.BlockSpec((tm, D_MODEL), lambda i, f: (i, 0)),
        pl.BlockSpec((D_MODEL, tf), lambda i, f: (0, f)),
        pl.BlockSpec((1, tf), lambda i, f: (0, f)),
        pl.BlockSpec((tf, D_MODEL), lambda i, f: (f, 0)),
        vec, vec, vec,
    ]
    out_spec = pl.BlockSpec((tm, D_MODEL), lambda i, f: (i, 0))
    out_shape = jax.ShapeDtypeStruct((m, D_MODEL), F32)
    args = (h2d, wb["w_ff1"], p["b_ff1"], wb["w_ff2"], p["b_ff2"], p["ln2_g"], p["ln2_b"])
    if side is None:
        return pl.pallas_call(
            _mlp_kernel, grid=grid, in_specs=in_specs, out_specs=out_spec, out_shape=out_shape,
            scratch_shapes=[pltpu.VMEM((tm, D_MODEL), BF16)], compiler_params=_cparams(2), name="mlp",
        )(*args)
    proj3, k_rows, v_rows = side
    nb, l_t, _ = proj3.shape
    assert nb == grid[0] * grid[1] and l_t % SUBLANES == 0 and KV_HALVES == 2
    step = lambda i, f: i * grid[1] + f
    kv_spec = pl.BlockSpec((1, KV_ROWS, LANES), lambda i, f: (step(i, f), 0, 0))
    return pl.pallas_call(
        _mlp_attn_kernel,
        grid=grid,
        in_specs=in_specs + [pl.BlockSpec((1, l_t, ATT_W), lambda i, f: (step(i, f), 0, P_Q // ATT_W)),
                             kv_spec, kv_spec],
        out_specs=[out_spec, pl.BlockSpec((l_t, ATT_W), lambda i, f: (step(i, f), 0))],
        out_shape=[out_shape, jax.ShapeDtypeStruct((nb * l_t, ATT_W), F32)],
        scratch_shapes=[pltpu.VMEM((tm, D_MODEL), BF16)],
        compiler_params=_cparams(2),
        name="mlp_attn",
    )(*args, proj3, k_rows, v_rows)


CAST_WEIGHTS = ("w_ff1", "w_ff2", "w_br_ssd", "w_out", "w_br_conf", "w_br_mem")


def _trunk(x, proj, ssd_hist, ssm0, conf_hist, mem_k, mem_v, p, wb, *, ssd_cfg, conf_cfg, attn_fn, mlp_side=None):
    nb, l_seq, _ = x.shape
    x2d = x.reshape(nb * l_seq, D_MODEL)
    proj2d, dt2d = proj
    proj3 = proj2d.reshape(nb, l_seq, P_W)
    dt3 = dt2d.reshape(nb, l_seq, LANES)
    if wb is None:
        y, hx, hb, hc, ssm1, *cast = _ssd(proj3, dt3, ssd_hist, ssm0, p, cast_weights=[p[n] for n in CAST_WEIGHTS],
                                          **ssd_cfg)
        wb = dict(zip(CAST_WEIGHTS, cast))
    else:
        y, hx, hb, hc, ssm1 = _ssd(proj3, dt3, ssd_hist, ssm0, p, **ssd_cfg)
    c, conf_hist1 = _conf(proj3, conf_hist, p, **conf_cfg)
    o = mem_k if attn_fn is None else attn_fn(proj3, mem_k, mem_v)
    h1 = _merge(x2d, y, c, o, proj2d, p, wb)
    side_out = None
    if mlp_side is None:
        out = _mlp(h1, p, wb)
    else:
        out, side_out = _mlp(h1, p, wb, side=mlp_side)
    return out.reshape(nb, l_seq, D_MODEL), ssm1, jnp.concatenate([hx, hb, hc], axis=-1), conf_hist1, wb, side_out


def kernel(x_prompt, x_sample, mem_prompt, state_ssm, state_ssd_conv, state_conf_conv, cache_mem_k, cache_mem_v, ln_in_g, ln_in_b, w_in, b_gate, ssd_conv_w, ssd_conv_b, ssd_dt_bias, ssd_a_log, ssd_d, ssd_norm_g, conf_dw_w, conf_dw_b, conf_ln_g, conf_ln_b, w_mem_k, w_mem_v, w_br_ssd, w_br_conf, w_br_mem, w_out, ln1_g, ln1_b, w_ff1, b_ff1, w_ff2, b_ff2, ln2_g, ln2_b):
    layer = 0
    row = lambda v: v.reshape(1, -1)
    lane_pad = lambda v: jnp.pad(v.reshape(1, -1), ((0, 0), (0, LANES - v.size)))
    p = {
        "ln_in_g": row(ln_in_g), "ln_in_b": row(ln_in_b),
        "w_in_t": jnp.swapaxes(w_in[layer], 0, 1),
        "b_gate": b_gate[layer],
        "conv_w": ssd_conv_w[layer], "conv_b": row(ssd_conv_b[layer]),
        "dt_bias": lane_pad(ssd_dt_bias[layer]), "a_log": lane_pad(ssd_a_log[layer]),
        "d_skip": row(jnp.repeat(ssd_d[layer], SSD_HEADDIM)), "norm_g": row(ssd_norm_g[layer]),
        "head_onehot": _head_onehot(),
        "conf_w": conf_dw_w[layer], "conf_b": row(conf_dw_b[layer]),
        "conf_ln_g": row(conf_ln_g[layer]), "conf_ln_b": row(conf_ln_b[layer]),
        "w_br_ssd": w_br_ssd[layer], "w_br_conf": w_br_conf[layer],
        "w_br_mem": w_br_mem[layer], "w_out": w_out[layer],
        "ln1_g": row(ln1_g[layer]), "ln1_b": row(ln1_b[layer]),
        "w_ff1": w_ff1[layer], "b_ff1": row(b_ff1[layer]),
        "w_ff2": w_ff2[layer], "b_ff2": row(b_ff2[layer]),
        "ln2_g": row(ln2_g[layer]), "ln2_b": row(ln2_b[layer]),
    }
    n_p, l_p, _ = x_prompt.shape
    n_s, l_s, _ = x_sample.shape

    mem2d = mem_prompt.reshape(n_p * MEM_LEN, D_MODEL)
    p_mem_k = _matmul(mem2d, w_mem_k[layer]).reshape(n_p, MEM_LEN, ATT_W)
    p_mem_v = _matmul(mem2d, w_mem_v[layer]).reshape(n_p, MEM_LEN, ATT_W)
    project = lambda x: _proj(x.reshape(-1, D_MODEL), p["ln_in_g"], p["ln_in_b"], p["w_in_t"])
    proj_s = project(x_sample)
    sample_attn_in = (proj_s[0].reshape(n_s, l_s, P_W),
                      _interleave_kv(cache_mem_k[layer]), _interleave_kv(cache_mem_v[layer]))
    y_prompt, p_ssm, p_ssd_conv, p_cc, wb, o_sample = _trunk(
        x_prompt, project(x_prompt),
        jnp.zeros((n_p, SSD_CONV - 1, CONV_DIM), F32),
        jnp.zeros((n_p, SSD_HEADS, SSD_HEADDIM, SSD_STATE), F32),
        jnp.zeros((n_p, CONF_HIST, D_CONV), F32),
        p_mem_k, p_mem_v, p, None,
        ssd_cfg=dict(n_seq=1, l_q=SSD_ROWS, g_step=SSD_GROUPS),
        conf_cfg=dict(n_seq=1, l_t=512),
        attn_fn=functools.partial(_attn, n_seq=1, l_t=512),
        mlp_side=sample_attn_in)

    y_sample, s_ssm, s_ssd_conv, s_cc, _, _ = _trunk(
        x_sample, proj_s, state_ssd_conv[layer], state_ssm[layer], state_conf_conv[layer],
        o_sample, None, p, wb,
        ssd_cfg=dict(n_seq=SSD_ROWS // l_s, l_q=l_s, g_step=4),
        conf_cfg=dict(n_seq=16, l_t=l_s),
        attn_fn=None)

    kv_shape = (DEPTH, n_p, MEM_LEN, MEM_HEADS, MEM_HEAD_DIM)
    return (y_prompt, y_sample, p_ssm[None], p_ssd_conv[None], p_cc[None],
            p_mem_k.reshape(kv_shape), p_mem_v.reshape(kv_shape),
            s_ssm[None], s_ssd_conv[None], s_cc[None])
```
